```python
import math
import jax, jax.numpy as jnp
from jax import lax
import numpy as np

D_MODEL = 1024
BATCH = 16
SEQ = 2048
DEPTH = 1
DEC_BATCH = 8
DEC_SEQ = 16
PAST_LEN = 1024

CHUNK = 64
QUERY_BLOCK = CHUNK
N_HEADS = 8
HEAD_DIM = 64
ATT_WIDTH = N_HEADS * HEAD_DIM
N_IDX_HEADS = 8
IDX_DIM = 64
MAX_TOPK = 256
D_RNN = D_MODEL
N_RNN_BLOCKS = 8
RNN_BLOCK = D_RNN // N_RNN_BLOCKS
CONV_WIDTH = 4
LRU_C = 8.0
N_GROUPS = 4
EXPERTS_PER_GROUP = 4
N_EXPERTS = N_GROUPS * EXPERTS_PER_GROUP
TOP_K_EXPERTS = 2
D_EXPERT = 256
ROPE_THETA = 10000.0
LN_EPS = 1e-5
DN_ALPHA = (2.0 * DEPTH) ** 0.25
DN_BETA = (8.0 * DEPTH) ** -0.25

SPLITS = (ATT_WIDTH, ATT_WIDTH, ATT_WIDTH, N_IDX_HEADS * IDX_DIM, IDX_DIM, N_IDX_HEADS,
          D_RNN, D_RNN, D_MODEL, D_MODEL)
D_IN_PROJ = int(sum(SPLITS))
SPLIT_POINTS = tuple(int(v) for v in np.cumsum(SPLITS)[:-1])

kernel_name = 'hybrid_dsa_rglru_hmoe_stream'


def rope(x, pos):
    half = x.shape[-1] // 2
    freqs = ROPE_THETA ** (-jnp.arange(half, dtype=jnp.float32) / half)
    ang = pos.astype(jnp.float32)[:, None] * freqs[None, :]
    cos = jnp.cos(ang)[:, None, :]
    sin = jnp.sin(ang)[:, None, :]
    xf = x.astype(jnp.float32)
    x1, x2 = xf[..., :half], xf[..., half:]
    return jnp.concatenate([x1 * cos - x2 * sin, x1 * sin + x2 * cos], axis=-1).astype(x.dtype)


def layer_norm(x, g, b):
    xf = x.astype(jnp.float32)
    mu = jnp.mean(xf, axis=-1, keepdims=True)
    var = jnp.mean(jnp.square(xf - mu), axis=-1, keepdims=True)
    y = (xf - mu) * lax.rsqrt(var + LN_EPS) * g.astype(jnp.float32) + b.astype(jnp.float32)
    return y.astype(x.dtype)


def sparse_attention(q, qi, wi, qpos, k, v, ki, n_sel):
    L = k.shape[1]
    limit = jnp.minimum((qpos // CHUNK + 1) * CHUNK, L)
    admissible = jnp.arange(L)[None, :] < limit[:, None]
    dots = jnp.einsum('bqhd,bsd->bqhs', qi, ki).astype(jnp.float32) * (IDX_DIM ** -0.5)
    score = jnp.einsum('bqhs,bqh->bqs', jax.nn.relu(dots), wi.astype(jnp.float32))
    score = jnp.where(admissible[None], score, -jnp.inf)
    _, idx = lax.top_k(score, n_sel)
    valid = idx < limit[None, :, None]
    gather = jax.vmap(lambda t, i: t[i])
    k_sel = gather(k, idx)
    v_sel = gather(v, idx)
    logits = jnp.einsum('bqhd,bqnhd->bqhn', q, k_sel).astype(jnp.float32) * (HEAD_DIM ** -0.5)
    logits = jnp.where(valid[:, :, None, :], logits, -jnp.inf)
    p = jax.nn.softmax(logits, axis=-1).astype(v.dtype)
    return jnp.einsum('bqhn,bqnhd->bqhd', p, v_sel)


def rglru_branch(xr, gr, conv_state, h0, conv_w, conv_b, w_rg_a, b_rg_a, w_rg_x, b_rg_x, lru_lambda):
    B, S, _ = xr.shape
    xpad = jnp.concatenate([conv_state.astype(xr.dtype), xr], axis=1)
    xc = conv_b + xpad[:, 0:S] * conv_w[0]
    for j in range(1, CONV_WIDTH):
        xc = xc + xpad[:, j:j + S] * conv_w[j]
    xb = xc.reshape(B, S, N_RNN_BLOCKS, RNN_BLOCK)
    r = jax.nn.sigmoid(jnp.einsum('bsnc,ncd->bsnd', xb, w_rg_a).reshape(B, S, D_RNN) + b_rg_a)
    i = jax.nn.sigmoid(jnp.einsum('bsnc,ncd->bsnd', xb, w_rg_x).reshape(B, S, D_RNN) + b_rg_x)
    log_a = (-LRU_C * r.astype(jnp.float32)) * jax.nn.softplus(-lru_lambda.astype(jnp.float32))
    a = jnp.exp(log_a)
    u = jnp.sqrt(-jnp.expm1(2.0 * log_a)) * (i * xc).astype(jnp.float32)
    u = u.at[:, 0].add(a[:, 0] * h0.astype(jnp.float32))

    def combine(left, right):
        a1, b1 = left
        a2, b2 = right
        return a1 * a2, a2 * b1 + b2

    _, h = lax.associative_scan(combine, (a, u), axis=1)
    y = h.astype(xr.dtype) * jax.nn.gelu(gr)
    return y, xpad[:, -(CONV_WIDTH - 1):], h[:, -1].astype(h0.dtype)


def token_mixers(x, pos, past_k, past_v, past_ki, conv_state, h0,
                 w_in, conv_w, conv_b, w_rg_a, b_rg_a, w_rg_x, b_rg_x, lru_lambda,
                 w_branch_attn, w_branch_rnn, w_out):
    B, S, _ = x.shape
    z = x @ w_in
    q, k, v, qi, ki, wi, xr, gr, ga, gb = jnp.split(z, SPLIT_POINTS, axis=-1)
    q = rope(q.reshape(B, S, N_HEADS, HEAD_DIM), pos)
    k = rope(k.reshape(B, S, N_HEADS, HEAD_DIM), pos)
    v = v.reshape(B, S, N_HEADS, HEAD_DIM)
    qi = rope(qi.reshape(B, S, N_IDX_HEADS, IDX_DIM), pos)
    ki = rope(ki[:, :, None, :], pos)[:, :, 0, :]
    wi = wi * (N_IDX_HEADS ** -0.5)
    if past_k is None:
        k_all, v_all, ki_all = k, v, ki
    else:
        k_all = jnp.concatenate([past_k.astype(k.dtype), k], axis=1)
        v_all = jnp.concatenate([past_v.astype(v.dtype), v], axis=1)
        ki_all = jnp.concatenate([past_ki.astype(ki.dtype), ki], axis=1)
    n_sel = min(MAX_TOPK, k_all.shape[1] // 4)
    if S > QUERY_BLOCK:
        nb = S // QUERY_BLOCK

        def to_blocks(t):
            return jnp.swapaxes(t.reshape((B, nb, QUERY_BLOCK) + t.shape[2:]), 0, 1)

        blocks = (to_blocks(q), to_blocks(qi), to_blocks(wi), pos.reshape(nb, QUERY_BLOCK))
        out = lax.map(lambda blk: sparse_attention(blk[0], blk[1], blk[2], blk[3],
                                                   k_all, v_all, ki_all, n_sel), blocks)
        attn = jnp.swapaxes(out, 0, 1).reshape(B, S, ATT_WIDTH)
    else:
        attn = sparse_attention(q, qi, wi, pos, k_all, v_all, ki_all, n_sel).reshape(B, S, ATT_WIDTH)
    rnn, new_conv, h_last = rglru_branch(xr, gr, conv_state, h0, conv_w, conv_b,
                                         w_rg_a, b_rg_a, w_rg_x, b_rg_x, lru_lambda)
    merged = jax.nn.sigmoid(ga) * (attn @ w_branch_attn) + jax.nn.sigmoid(gb) * (rnn @ w_branch_rnn)
    return merged @ w_out, k, v, ki, new_conv, h_last


def hier_moe(x, w_router_group, b_router_group, w_router_expert, b_router_expert, w_gate_up, w_down):
    T = x.shape[0]
    g_logits = (x @ w_router_group + b_router_group).astype(jnp.float32)
    g_prob = jax.nn.softmax(g_logits, axis=-1)
    g_sel = jnp.argmax(g_logits, axis=-1)
    p_group = jnp.take_along_axis(g_prob, g_sel[:, None], axis=-1)
    e_logits = (x @ w_router_expert + b_router_expert).astype(jnp.float32)
    e_logits = e_logits.reshape(T, N_GROUPS, EXPERTS_PER_GROUP)
    e_logits = jnp.take_along_axis(e_logits, g_sel[:, None, None], axis=1)[:, 0]
    top_vals, top_idx = lax.top_k(e_logits, TOP_K_EXPERTS)
    p_exp = jax.nn.softmax(top_vals, axis=-1) * p_group
    expert_id = g_sel[:, None] * EXPERTS_PER_GROUP + top_idx
    gate = jnp.sum(jax.nn.one_hot(expert_id, N_EXPERTS, dtype=jnp.float32) * p_exp[..., None], axis=1)
    gate = gate.astype(x.dtype)
    y = jnp.zeros_like(x)
    for e in range(N_EXPERTS):
        g, u = jnp.split(x @ w_gate_up[e], 2, axis=-1)
        y = y + gate[:, e:e + 1] * ((jax.nn.silu(g) * u) @ w_down[e])
    return y


def setup_inputs(seed: int = 0) -> dict:
    key = jax.random.key(seed)
    ks = jax.random.split(key, 32)
    f32 = jnp.float32

    def nrm(k, shape, scale):
        return jax.random.normal(k, shape, f32) * scale

    u = jax.random.uniform(ks[10], (DEPTH, D_RNN), f32, 0.9, 0.999)
    s = u ** (1.0 / LRU_C)
    lru_lambda = jnp.log(s) - jnp.log1p(-s)
    return {
        'x_prompt': nrm(ks[0], (BATCH, SEQ, D_MODEL), 1.0),
        'x_sample': nrm(ks[1], (DEC_BATCH, DEC_SEQ, D_MODEL), 1.0),
        'cache_k': nrm(ks[2], (DEPTH, DEC_BATCH, PAST_LEN, N_HEADS, HEAD_DIM), 1.0),
        'cache_v': nrm(ks[3], (DEPTH, DEC_BATCH, PAST_LEN, N_HEADS, HEAD_DIM), 1.0),
        'cache_k_idx': nrm(ks[4], (DEPTH, DEC_BATCH, PAST_LEN, IDX_DIM), 1.0),
        'state_conv': nrm(ks[5], (DEPTH, DEC_BATCH, CONV_WIDTH - 1, D_RNN), 1.0),
        'state_h': nrm(ks[6], (DEPTH, DEC_BATCH, D_RNN), 0.5),
        'w_in': nrm(ks[7], (DEPTH, D_MODEL, D_IN_PROJ), D_MODEL ** -0.5),
        'conv_w': nrm(ks[8], (DEPTH, CONV_WIDTH, D_RNN), CONV_WIDTH ** -0.5),
        'conv_b': nrm(ks[9], (DEPTH, D_RNN), 0.01),
        'w_rg_a': nrm(ks[11], (DEPTH, N_RNN_BLOCKS, RNN_BLOCK, RNN_BLOCK), RNN_BLOCK ** -0.5),
        'b_rg_a': nrm(ks[12], (DEPTH, D_RNN), 0.01),
        'w_rg_x': nrm(ks[13], (DEPTH, N_RNN_BLOCKS, RNN_BLOCK, RNN_BLOCK), RNN_BLOCK ** -0.5),
        'b_rg_x': nrm(ks[14], (DEPTH, D_RNN), 0.01),
        'lru_lambda': lru_lambda,
        'w_branch_attn': nrm(ks[15], (DEPTH, ATT_WIDTH, D_MODEL), ATT_WIDTH ** -0.5),
        'w_branch_rnn': nrm(ks[16], (DEPTH, D_RNN, D_MODEL), D_RNN ** -0.5),
        'w_out': nrm(ks[17], (DEPTH, D_MODEL, D_MODEL), D_MODEL ** -0.5 * DN_BETA),
        'ln1_g': 1.0 + nrm(ks[18], (DEPTH, D_MODEL), 0.01),
        'ln1_b': nrm(ks[19], (DEPTH, D_MODEL), 0.01),
        'w_router_group': nrm(ks[20], (DEPTH, D_MODEL, N_GROUPS), D_MODEL ** -0.5),
        'b_router_group': nrm(ks[21], (DEPTH, N_GROUPS), 0.01),
        'w_router_expert': nrm(ks[22], (DEPTH, D_MODEL, N_EXPERTS), D_MODEL ** -0.5),
        'b_router_expert': nrm(ks[23], (DEPTH, N_EXPERTS), 0.01),
        'w_gate_up': nrm(ks[24], (DEPTH, N_EXPERTS, D_MODEL, 2 * D_EXPERT), D_MODEL ** -0.5),
        'w_down': nrm(ks[25], (DEPTH, N_EXPERTS, D_EXPERT, D_MODEL), D_EXPERT ** -0.5 * DN_BETA),
        'ln2_g': 1.0 + nrm(ks[26], (DEPTH, D_MODEL), 0.01),
        'ln2_b': nrm(ks[27], (DEPTH, D_MODEL), 0.01),
    }


def reference(x_prompt, x_sample, cache_k, cache_v, cache_k_idx, state_conv, state_h,
              w_in, conv_w, conv_b, w_rg_a, b_rg_a, w_rg_x, b_rg_x, lru_lambda,
              w_branch_attn, w_branch_rnn, w_out, ln1_g, ln1_b,
              w_router_group, b_router_group, w_router_expert, b_router_expert,
              w_gate_up, w_down, ln2_g, ln2_b):
    bp, sp, _ = x_prompt.shape
    bs, ss, _ = x_sample.shape
    past_len = cache_k.shape[2]
    pos_p = jnp.arange(sp, dtype=jnp.int32)
    pos_s = past_len + jnp.arange(ss, dtype=jnp.int32)
    n_p = bp * sp
    xp, xs = x_prompt, x_sample
    kp_l, vp_l, kip_l, cp_l, hp_l = [], [], [], [], []
    ks_l, vs_l, kis_l, cs_l, hs_l = [], [], [], [], []
    for l in range(DEPTH):
        mixer_w = (w_in[l], conv_w[l], conv_b[l], w_rg_a[l], b_rg_a[l], w_rg_x[l], b_rg_x[l],
                   lru_lambda[l], w_branch_attn[l], w_branch_rnn[l], w_out[l])
        zero_conv = jnp.zeros((bp, CONV_WIDTH - 1, D_RNN), xp.dtype)
        zero_h = jnp.zeros((bp, D_RNN), jnp.float32)
        mix_p, kp, vp, kip, cp, hp = token_mixers(xp, pos_p, None, None, None, zero_conv, zero_h, *mixer_w)
        mix_s, kss, vss, kis, cs, hs = token_mixers(xs, pos_s, cache_k[l], cache_v[l], cache_k_idx[l],
                                                    state_conv[l], state_h[l], *mixer_w)
        xp = layer_norm(DN_ALPHA * xp + mix_p, ln1_g[l], ln1_b[l])
        xs = layer_norm(DN_ALPHA * xs + mix_s, ln1_g[l], ln1_b[l])
        tokens = jnp.concatenate([xp.reshape(n_p, D_MODEL), xs.reshape(bs * ss, D_MODEL)], axis=0)
        ffn = hier_moe(tokens, w_router_group[l], b_router_group[l], w_router_expert[l],
                       b_router_expert[l], w_gate_up[l], w_down[l])
        tokens = layer_norm(DN_ALPHA * tokens + ffn, ln2_g[l], ln2_b[l])
        xp = tokens[:n_p].reshape(bp, sp, D_MODEL)
        xs = tokens[n_p:].reshape(bs, ss, D_MODEL)
        kp_l.append(kp); vp_l.append(vp); kip_l.append(kip); cp_l.append(cp); hp_l.append(hp)
        ks_l.append(kss); vs_l.append(vss); kis_l.append(kis); cs_l.append(cs); hs_l.append(hs)
    return (xp, xs,
            jnp.stack(kp_l), jnp.stack(vp_l), jnp.stack(kip_l), jnp.stack(cp_l), jnp.stack(hp_l),
            jnp.stack(ks_l), jnp.stack(vs_l), jnp.stack(kis_l), jnp.stack(cs_l), jnp.stack(hs_l))
```

```python
import functools

import jax
import jax.numpy as jnp
import numpy as np
from jax import lax
from jax.experimental import pallas as pl
from jax.experimental.pallas import tpu as pltpu

F32 = jnp.float32
BF16 = jnp.bfloat16
I32 = jnp.int32

D_MODEL = 1024
N_HEADS = 8
HEAD_DIM = 64
ATT_WIDTH = N_HEADS * HEAD_DIM
N_IDX_HEADS = 8
IDX_DIM = 64
MAX_TOPK = 256
CHUNK = 64
D_RNN = D_MODEL
N_RNN_BLOCKS = 8
RNN_BLOCK = D_RNN // N_RNN_BLOCKS
CONV_WIDTH = 4
LRU_C = 8.0
N_GROUPS = 4
EXPERTS_PER_GROUP = 4
N_EXPERTS = N_GROUPS * EXPERTS_PER_GROUP
D_EXPERT = 256
ROPE_THETA = 10000.0
LN_EPS = 1e-5
DEPTH = 1
DN_ALPHA = (2.0 * DEPTH) ** 0.25
SPLITS = (ATT_WIDTH, ATT_WIDTH, ATT_WIDTH, N_IDX_HEADS * IDX_DIM, IDX_DIM, N_IDX_HEADS,
          D_RNN, D_RNN, D_MODEL, D_MODEL)
SPLIT_POINTS = tuple(int(v) for v in np.cumsum(SPLITS)[:-1])

LANES = 128
SUBLANES = 8
VMEM_LIMIT = 56 * 1024 * 1024
Q_BLOCK = LANES
KEY_TILE = 256
INT_MIN = np.int32(-2 ** 31)
NEG_BIG = -1e30
WT_ROWS = 3 * ATT_WIDTH + 16
WN_COLS = 2 * ATT_WIDTH + LANES


def _params(semantics):
    return pltpu.CompilerParams(dimension_semantics=semantics, vmem_limit_bytes=VMEM_LIMIT)


def _dot(a, b):
    return jnp.dot(a, b, preferred_element_type=F32)


def _layer_norm(x, g, b):
    mu = jnp.mean(x, axis=-1, keepdims=True)
    xc = x - mu
    var = jnp.mean(xc * xc, axis=-1, keepdims=True)
    return xc * lax.rsqrt(var + LN_EPS) * g + b


def _proj_kernel(x_ref, wt_ref, wn_ref, cosn_ref, sinn_ref, cost_ref, sint_ref,
                 qT_ref, qiT_ref, vT_ref, wiT_ref, k_ref, kb_ref, v_ref, ki_ref, kib_ref):
    tm = x_ref.shape[1]
    kt_out = vT_ref.shape[3]
    xb = x_ref[0].astype(BF16)
    zt = lax.dot_general(wt_ref[...], xb, (((1,), (1,)), ((), ())), preferred_element_type=F32)
    zn = _dot(xb, wn_ref[...])
    cost = cost_ref[...]
    sint = sint_ref[...]
    half = HEAD_DIM // 2
    scale = HEAD_DIM ** -0.5
    for base, ref in ((0, qT_ref), (ATT_WIDTH, qiT_ref)):
        for h in range(N_HEADS):
            r = h * HEAD_DIM
            x1 = zt[base + r:base + r + half]
            x2 = zt[base + r + half:base + r + HEAD_DIM]
            ref[0, r:r + half, :] = ((x1 * cost - x2 * sint) * scale).astype(BF16)
            ref[0, r + half:r + HEAD_DIM, :] = ((x1 * sint + x2 * cost) * scale).astype(BF16)
    for c in range(tm // kt_out):
        vT_ref[0, c] = zt[2 * ATT_WIDTH:3 * ATT_WIDTH, c * kt_out:(c + 1) * kt_out].astype(BF16)
    wiT_ref[0] = zt[3 * ATT_WIDTH:3 * ATT_WIDTH + N_IDX_HEADS] * (N_IDX_HEADS ** -0.5)

    cosn = cosn_ref[...]
    sinn = sinn_ref[...]
    lane = lax.broadcasted_iota(I32, (tm, LANES), 1)
    first_half = (lane & half) == 0

    def rope_n(z):
        partner = jnp.where(first_half, pltpu.roll(z, LANES - half, 1), pltpu.roll(z, half, 1))
        return z * cosn + partner * sinn

    for g in range(ATT_WIDTH // LANES):
        kg = rope_n(zn[:, g * LANES:(g + 1) * LANES])
        k_ref[0, :, g * LANES:(g + 1) * LANES] = kg
        kb_ref[0, :, g * LANES:(g + 1) * LANES] = kg.astype(BF16)
    v_ref[0] = zn[:, ATT_WIDTH:2 * ATT_WIDTH]
    kig = rope_n(zn[:, 2 * ATT_WIDTH:2 * ATT_WIDTH + LANES])
    ki_ref[0] = kig[:, :IDX_DIM]
    kib_ref[0] = kig[:, :IDX_DIM].astype(BF16)


def _proj(x, wt, wn, tables, tm, kt_out):
    b, s, _ = x.shape
    cosn, sinn, cost, sint = tables
    ns = s // tm
    out_shape = (
        jax.ShapeDtypeStruct((b, ATT_WIDTH, s), BF16),
        jax.ShapeDtypeStruct((b, ATT_WIDTH, s), BF16),
        jax.ShapeDtypeStruct((b, s // kt_out, ATT_WIDTH, kt_out), BF16),
        jax.ShapeDtypeStruct((b, N_IDX_HEADS, s), F32),
        jax.ShapeDtypeStruct((b, s, ATT_WIDTH), F32),
        jax.ShapeDtypeStruct((b, s, ATT_WIDTH), BF16),
        jax.ShapeDtypeStruct((b, s, ATT_WIDTH), F32),
        jax.ShapeDtypeStruct((b, s, IDX_DIM), F32),
        jax.ShapeDtypeStruct((b, s, IDX_DIM), BF16),
    )
    tok = lambda w: pl.BlockSpec((1, tm, w), lambda i, j: (i, j, 0))
    feat = lambda r: pl.BlockSpec((1, r, tm), lambda i, j: (i, 0, j))
    return pl.pallas_call(
        _proj_kernel,
        grid=(b, ns),
        in_specs=[
            tok(D_MODEL),
            pl.BlockSpec((WT_ROWS, D_MODEL), lambda i, j: (0, 0)),
            pl.BlockSpec((D_MODEL, WN_COLS), lambda i, j: (0, 0)),
            pl.BlockSpec((tm, LANES), lambda i, j: (j, 0)),
            pl.BlockSpec((tm, LANES), lambda i, j: (j, 0)),
            pl.BlockSpec((HEAD_DIM // 2, tm), lambda i, j: (0, j)),
            pl.BlockSpec((HEAD_DIM // 2, tm), lambda i, j: (0, j)),
        ],
        out_specs=(
            feat(ATT_WIDTH), feat(ATT_WIDTH),
            pl.BlockSpec((1, tm // kt_out, ATT_WIDTH, kt_out), lambda i, j: (i, j, 0, 0)),
            feat(N_IDX_HEADS),
            tok(ATT_WIDTH), tok(ATT_WIDTH), tok(ATT_WIDTH), tok(IDX_DIM), tok(IDX_DIM),
        ),
        out_shape=out_shape,
        compiler_params=_params(("parallel", "parallel")),
        name="proj",
    )(x, wt, wn, cosn, sinn, cost, sint)


def _attn_kernel(nk_ref, lim_ref, qT_ref, qiT_ref, wiT_ref, k_ref, vT_ref, ki_ref, o_ref,
                 keys_scr, bias_scr):
    nk = nk_ref[pl.program_id(1)]
    lim = lim_ref[0]
    w = wiT_ref[0]
    qi = qiT_ref[0]
    q = qT_ref[0]

    def rows(kt):
        return pl.ds(pl.multiple_of(kt * KEY_TILE, KEY_TILE), KEY_TILE)

    qi_pairs = [jnp.concatenate([qi[(2 * p) * IDX_DIM:(2 * p + 1) * IDX_DIM],
                                 qi[(2 * p + 1) * IDX_DIM:(2 * p + 2) * IDX_DIM]], axis=1)
                for p in range(N_IDX_HEADS // 2)]
    row_iota = lax.broadcasted_iota(I32, (KEY_TILE, Q_BLOCK), 0)

    def score_tile(kt, carry):
        ki_t = ki_ref[0, rows(kt), :]
        acc = jnp.zeros((KEY_TILE, Q_BLOCK), F32)
        for p in range(N_IDX_HEADS // 2):
            d = _dot(ki_t, qi_pairs[p])
            acc = acc + jnp.maximum(d[:, :Q_BLOCK], 0.0) * w[2 * p:2 * p + 1]
            acc = acc + jnp.maximum(d[:, Q_BLOCK:], 0.0) * w[2 * p + 1:2 * p + 2]
        acc = jnp.where(acc == 0.0, 0.0, acc)
        bits = lax.bitcast_convert_type(acc, I32)
        key = jnp.where(bits < 0, bits ^ 0x7FFFFFFF, bits)
        key = jnp.where(row_iota + kt * KEY_TILE < lim, key, INT_MIN)
        keys_scr[rows(kt), :] = key
        return carry

    lax.fori_loop(0, nk, score_tile, 0)

    def count(pred):
        def body(kt, c):
            m = jnp.where(pred(keys_scr[rows(kt), :]), 1, 0)
            return c + jnp.sum(m.reshape(KEY_TILE // SUBLANES, SUBLANES, Q_BLOCK), axis=0)
        c = lax.fori_loop(0, nk, body, jnp.zeros((SUBLANES, Q_BLOCK), I32))
        return jnp.sum(c, axis=0, keepdims=True)

    def bisect(i, carry):
        t, cnt_t = carry
        cand = t + lax.shift_left(jnp.int32(1), 31 - i)
        cnt = count(lambda key: key >= cand)
        ok = cnt >= MAX_TOPK
        return jnp.where(ok, cand, t), jnp.where(ok, cnt, cnt_t)

    t0 = jnp.full((1, Q_BLOCK), INT_MIN, I32)
    thr, cnt_thr = lax.fori_loop(0, 32, bisect, (t0, jnp.full((1, Q_BLOCK), nk * KEY_TILE, I32)))

    thr_adm = jnp.maximum(thr, INT_MIN + 1)

    def bias_tile(kt, carry):
        bias_scr[rows(kt), :] = jnp.where(keys_scr[rows(kt), :] >= thr_adm, 0.0, NEG_BIG)
        return carry

    lax.fori_loop(0, nk, bias_tile, 0)

    tie = jnp.logical_and(cnt_thr > MAX_TOPK, thr > INT_MIN)
    any_tie = jnp.max(jnp.where(tie, 1.0, 0.0)) > 0.0

    @pl.when(any_tie)
    def _():
        need = (MAX_TOPK - count(lambda key: key > thr)).astype(F32)
        tri = (lax.broadcasted_iota(I32, (KEY_TILE, KEY_TILE), 1)
               <= lax.broadcasted_iota(I32, (KEY_TILE, KEY_TILE), 0))
        tri = jnp.where(tri, 1.0, 0.0).astype(BF16)

        def tie_tile(kt, before):
            key = keys_scr[rows(kt), :]
            eq = key == thr
            rank = _dot(tri, jnp.where(eq, 1.0, 0.0).astype(BF16)) + before
            sel = jnp.logical_or(key > thr, jnp.logical_and(eq, rank <= need))
            sel = jnp.logical_and(sel, key > INT_MIN)
            bias_scr[rows(kt), :] = jnp.where(sel, 0.0, NEG_BIG)
            return rank[KEY_TILE - 1:KEY_TILE, :]

        lax.fori_loop(0, nk, tie_tile, jnp.zeros((1, Q_BLOCK), F32))

    zeros_half = jnp.zeros((HEAD_DIM, Q_BLOCK), BF16)
    outs = []
    for h in range(N_HEADS):
        pair = h // 2
        qh = q[h * HEAD_DIM:(h + 1) * HEAD_DIM]
        rhs = jnp.concatenate([qh, zeros_half] if h % 2 == 0 else [zeros_half, qh], axis=0)

        def head_tile(kt, carry, pair=pair, rhs=rhs, h=h):
            m, l, acc = carry
            k_t = k_ref[0, rows(kt), pair * LANES:(pair + 1) * LANES]
            s = _dot(k_t, rhs) + bias_scr[rows(kt), :]
            m_new = jnp.maximum(m, jnp.max(s, axis=0, keepdims=True))
            alpha = jnp.exp(m - m_new)
            p = jnp.exp(s - m_new)
            l = alpha * l + jnp.sum(p, axis=0, keepdims=True)
            v_t = vT_ref[0, kt, h * HEAD_DIM:(h + 1) * HEAD_DIM, :]
            acc = alpha * acc + _dot(v_t, p.astype(BF16))
            return m_new, l, acc

        m0 = jnp.full((1, Q_BLOCK), NEG_BIG, F32)
        l0 = jnp.zeros((1, Q_BLOCK), F32)
        a0 = jnp.zeros((HEAD_DIM, Q_BLOCK), F32)
        _, l, acc = lax.fori_loop(0, nk, head_tile, (m0, l0, a0))
        outs.append(acc / l)
    o_ref[0] = jnp.concatenate(outs, axis=0).T.astype(BF16)


def _attn(nk, limits, qT, qiT, wiT, kb, vT, kib):
    b, _, sq = qT.shape
    nq = sq // Q_BLOCK
    lk = kb.shape[1]
    grid_spec = pltpu.PrefetchScalarGridSpec(
        num_scalar_prefetch=1,
        grid=(b, nq),
        in_specs=[
            pl.BlockSpec((1, 1, Q_BLOCK), lambda i, j, nk: (j, 0, 0)),
            pl.BlockSpec((1, ATT_WIDTH, Q_BLOCK), lambda i, j, nk: (i, 0, j)),
            pl.BlockSpec((1, ATT_WIDTH, Q_BLOCK), lambda i, j, nk: (i, 0, j)),
            pl.BlockSpec((1, N_IDX_HEADS, Q_BLOCK), lambda i, j, nk: (i, 0, j)),
            pl.BlockSpec((1, lk, ATT_WIDTH), lambda i, j, nk: (i, 0, 0)),
            pl.BlockSpec((1, lk // KEY_TILE, ATT_WIDTH, KEY_TILE), lambda i, j, nk: (i, 0, 0, 0)),
            pl.BlockSpec((1, lk, IDX_DIM), lambda i, j, nk: (i, 0, 0)),
        ],
        out_specs=pl.BlockSpec((1, Q_BLOCK, ATT_WIDTH), lambda i, j, nk: (i, j, 0)),
        scratch_shapes=[pltpu.VMEM((lk, Q_BLOCK), I32), pltpu.VMEM((lk, Q_BLOCK), F32)],
    )
    return pl.pallas_call(
        _attn_kernel,
        grid_spec=grid_spec,
        out_shape=jax.ShapeDtypeStruct((b, sq, ATT_WIDTH), BF16),
        compiler_params=_params(("parallel", "arbitrary")),
        name="attn",
    )(nk, limits, qT, qiT, wiT, kb, vT, kib)


def _rglru_kernel(x_ref, cs_ref, h0_ref, wxr_ref, wgr_ref, cw_ref, cb_ref, wa_ref, ba_ref,
                  wx_ref, bx_ref, lam_ref, y_ref, co_ref, hl_ref,
                  xbuf, a_scr, u_scr, h_scr, hc_scr):
    ts = x_ref.shape[1]

    @pl.when(pl.program_id(1) == 0)
    def _():
        xbuf[0:SUBLANES, :] = cs_ref[0]
        hc_scr[...] = h0_ref[0]

    xb = x_ref[0].astype(BF16)
    xr = _dot(xb, wxr_ref[...])
    gr = _dot(xb, wgr_ref[...])
    xbuf[SUBLANES:SUBLANES + ts, :] = xr
    cw = cw_ref[...]
    xc = cb_ref[...] + xbuf[5:5 + ts, :] * cw[0:1]
    xc = xc + xbuf[6:6 + ts, :] * cw[1:2]
    xc = xc + xbuf[7:7 + ts, :] * cw[2:3]
    xc = xc + xr * cw[3:4]
    tail = xbuf[ts:ts + SUBLANES, :]
    xbuf[0:SUBLANES, :] = tail
    co_ref[0] = tail

    xcb = xc.astype(BF16)

    def gate(w_ref, b_ref):
        parts = [_dot(xcb[:, n * RNN_BLOCK:(n + 1) * RNN_BLOCK], w_ref[n]) for n in range(N_RNN_BLOCKS)]
        return jax.nn.sigmoid(jnp.concatenate(parts, axis=1) + b_ref[...])

    r = gate(wa_ref, ba_ref)
    i = gate(wx_ref, bx_ref)
    nl = -lam_ref[...]
    softplus = jnp.maximum(nl, 0.0) + jnp.log1p(jnp.exp(-jnp.abs(nl)))
    log_a = (-LRU_C * r) * softplus
    a_scr[...] = jnp.exp(log_a)
    th = jnp.tanh(log_a)
    u_scr[...] = jnp.sqrt(-2.0 * th / (1.0 - th)) * (i * xc)

    row = lax.broadcasted_iota(I32, (SUBLANES, D_RNN), 0)

    def group(g, h_prev):
        rs = pl.ds(pl.multiple_of(g * SUBLANES, SUBLANES), SUBLANES)
        a = a_scr[rs, :]
        b = u_scr[rs, :]
        for d in (1, 2, 4):
            a_sh = jnp.where(row >= d, pltpu.roll(a, d, 0), 1.0)
            b_sh = jnp.where(row >= d, pltpu.roll(b, d, 0), 0.0)
            b = a * b_sh + b
            a = a * a_sh
        h = b + a * h_prev
        h_scr[rs, :] = h
        return h[SUBLANES - 1:SUBLANES, :]

    h_last = lax.fori_loop(0, ts // SUBLANES, group, hc_scr[...])
    hc_scr[...] = h_last
    hl_ref[0] = h_last
    y_ref[0] = (h_scr[...] * jax.nn.gelu(gr)).astype(BF16)


def _rglru(x, conv_state8, h0, wxr, wgr, conv_w, conv_b, wa, ba, wx, bx, lam, ts):
    b, s, _ = x.shape
    const2 = lambda shape: pl.BlockSpec(shape, lambda i, j: (0, 0))
    const3 = lambda shape: pl.BlockSpec(shape, lambda i, j: (0, 0, 0))
    per_b = lambda r: pl.BlockSpec((1, r, D_RNN), lambda i, j: (i, 0, 0))
    return pl.pallas_call(
        _rglru_kernel,
        grid=(b, s // ts),
        in_specs=[
            pl.BlockSpec((1, ts, D_MODEL), lambda i, j: (i, j, 0)),
            per_b(SUBLANES), per_b(1),
            const2((D_MODEL, D_RNN)), const2((D_MODEL, D_RNN)),
            const2((CONV_WIDTH, D_RNN)), const2((1, D_RNN)),
            const3((N_RNN_BLOCKS, RNN_BLOCK, RNN_BLOCK)), const2((1, D_RNN)),
            const3((N_RNN_BLOCKS, RNN_BLOCK, RNN_BLOCK)), const2((1, D_RNN)),
            const2((1, D_RNN)),
        ],
        out_specs=(
            pl.BlockSpec((1, ts, D_RNN), lambda i, j: (i, j, 0)),
            per_b(SUBLANES), per_b(1),
        ),
        out_shape=(
            jax.ShapeDtypeStruct((b, s, D_RNN), BF16),
            jax.ShapeDtypeStruct((b, SUBLANES, D_RNN), F32),
            jax.ShapeDtypeStruct((b, 1, D_RNN), F32),
        ),
        scratch_shapes=[
            pltpu.VMEM((ts + SUBLANES, D_RNN), F32),
            pltpu.VMEM((ts, D_RNN), F32), pltpu.VMEM((ts, D_RNN), F32), pltpu.VMEM((ts, D_RNN), F32),
            pltpu.VMEM((1, D_RNN), F32),
        ],
        compiler_params=_params(("parallel", "arbitrary")),
        name="rglru",
    )(x, conv_state8, h0, wxr, wgr, conv_w, conv_b, wa, ba, wx, bx, lam)


def _merge_kernel(x_ref, attn_ref, y_ref, wga_ref, wgb_ref, wba_ref, wbr_ref, wout_ref, g_ref, b_ref,
                  o_ref):
    x = x_ref[...]
    xb = x.astype(BF16)
    ga = _dot(xb, wga_ref[...])
    gb = _dot(xb, wgb_ref[...])
    merged = (jax.nn.sigmoid(ga) * _dot(attn_ref[...], wba_ref[...])
              + jax.nn.sigmoid(gb) * _dot(y_ref[...], wbr_ref[...]))
    mix = _dot(merged.astype(BF16), wout_ref[...])
    o_ref[...] = _layer_norm(DN_ALPHA * x + mix, g_ref[...], b_ref[...])


def _merge(x, attn, y, wga, wgb, wba, wbr, wout, g, b, tm):
    t = x.shape[0]
    tok = lambda w: pl.BlockSpec((tm, w), lambda i: (i, 0))
    const = lambda shape: pl.BlockSpec(shape, lambda i: (0, 0))
    return pl.pallas_call(
        _merge_kernel,
        grid=(t // tm,),
        in_specs=[tok(D_MODEL), tok(ATT_WIDTH), tok(D_RNN),
                  const((D_MODEL, D_MODEL)), const((D_MODEL, D_MODEL)),
                  const((ATT_WIDTH, D_MODEL)), const((D_RNN, D_MODEL)), const((D_MODEL, D_MODEL)),
                  const((1, D_MODEL)), const((1, D_MODEL))],
        out_specs=tok(D_MODEL),
        out_shape=jax.ShapeDtypeStruct((t, D_MODEL), F32),
        compiler_params=_params(("parallel",)),
        name="merge",
    )(x, attn, y, wga, wgb, wba, wbr, wout, g, b)


def _moe_kernel(x_ref, wr_ref, br_ref, wgu_ref, wd_ref, g_ref, b_ref, o_ref, gate_scr, acc_scr, xb_scr):
    e = pl.program_id(1)
    tm = x_ref.shape[0]
    lane = lax.broadcasted_iota(I32, (tm, LANES), 1)

    @pl.when(e == 0)
    def _():
        xb = x_ref[...].astype(BF16)
        xb_scr[...] = xb
        acc_scr[...] = jnp.zeros_like(acc_scr)
        logits = _dot(xb, wr_ref[...]) + br_ref[...]
        neg_inf = -jnp.inf
        gl = jnp.where(lane < N_GROUPS, logits, neg_inf)
        gmax = jnp.max(gl, axis=1, keepdims=True)
        p_group = 1.0 / jnp.sum(jnp.exp(gl - gmax), axis=1, keepdims=True)
        g_sel = jnp.min(jnp.where(gl == gmax, lane, LANES), axis=1, keepdims=True)
        lo = N_GROUPS + EXPERTS_PER_GROUP * g_sel
        el = jnp.where(jnp.logical_and(lane >= lo, lane < lo + EXPERTS_PER_GROUP), logits, neg_inf)
        v1 = jnp.max(el, axis=1, keepdims=True)
        i1 = jnp.min(jnp.where(el == v1, lane, LANES), axis=1, keepdims=True)
        el2 = jnp.where(lane == i1, neg_inf, el)
        v2 = jnp.max(el2, axis=1, keepdims=True)
        i2 = jnp.min(jnp.where(el2 == v2, lane, LANES), axis=1, keepdims=True)
        e2 = jnp.exp(v2 - v1)
        p1 = 1.0 / (1.0 + e2)
        p2 = e2 * p1
        gate_scr[...] = (jnp.where(lane == i1, p1 * p_group, 0.0)
                         + jnp.where(lane == i2, p2 * p_group, 0.0))

    h = _dot(xb_scr[...], wgu_ref[0])
    act = jax.nn.silu(h[:, :D_EXPERT]) * h[:, D_EXPERT:]
    y = _dot(act.astype(BF16), wd_ref[0])
    gate_e = jnp.sum(jnp.where(lane == e + N_GROUPS, gate_scr[...], 0.0), axis=1, keepdims=True)
    acc_scr[...] = acc_scr[...] + gate_e * y

    @pl.when(e == N_EXPERTS - 1)
    def _():
        o_ref[...] = _layer_norm(DN_ALPHA * x_ref[...] + acc_scr[...], g_ref[...], b_ref[...])


def _moe(x, wr, br, wgu, wd, g, b, tm):
    t = x.shape[0]
    return pl.pallas_call(
        _moe_kernel,
        grid=(t // tm, N_EXPERTS),
        in_specs=[
            pl.BlockSpec((tm, D_MODEL), lambda i, e: (i, 0)),
            pl.BlockSpec((D_MODEL, LANES), lambda i, e: (0, 0)),
            pl.BlockSpec((1, LANES), lambda i, e: (0, 0)),
            pl.BlockSpec((1, D_MODEL, 2 * D_EXPERT), lambda i, e: (e, 0, 0)),
            pl.BlockSpec((1, D_EXPERT, D_MODEL), lambda i, e: (e, 0, 0)),
            pl.BlockSpec((1, D_MODEL), lambda i, e: (0, 0)),
            pl.BlockSpec((1, D_MODEL), lambda i, e: (0, 0)),
        ],
        out_specs=pl.BlockSpec((tm, D_MODEL), lambda i, e: (i, 0)),
        out_shape=jax.ShapeDtypeStruct((t, D_MODEL), F32),
        scratch_shapes=[pltpu.VMEM((tm, LANES), F32), pltpu.VMEM((tm, D_MODEL), F32),
                        pltpu.VMEM((tm, D_MODEL), BF16)],
        compiler_params=_params(("parallel", "arbitrary")),
        name="moe",
    )(x, wr, br, wgu, wd, g, b)


def _rope_tables(pos):
    half = HEAD_DIM // 2
    freqs = ROPE_THETA ** (-jnp.arange(half, dtype=F32) / half)
    ang = pos.astype(F32)[:, None] * freqs[None, :]
    cos, sin = jnp.cos(ang), jnp.sin(ang)
    cosn = jnp.tile(cos, (1, LANES // half))
    sinn = jnp.tile(jnp.concatenate([-sin, sin], axis=1), (1, LANES // HEAD_DIM))
    return cosn, sinn, cos.T, sin.T


def _pad_axis(a, axis, size):
    pad = [(0, 0)] * a.ndim
    pad[axis] = (0, size - a.shape[axis])
    return jnp.pad(a, pad)


def kernel(x_prompt, x_sample, cache_k, cache_v, cache_k_idx, state_conv, state_h, w_in, conv_w, conv_b, w_rg_a, b_rg_a, w_rg_x, b_rg_x, lru_lambda, w_branch_attn, w_branch_rnn, w_out, ln1_g, ln1_b, w_router_group, b_router_group, w_router_expert, b_router_expert, w_gate_up, w_down, ln2_g, ln2_b):
    assert w_in.shape[0] == DEPTH == 1
    bp, sp, _ = x_prompt.shape
    bs, ss, _ = x_sample.shape
    past = cache_k.shape[2]
    ns_tok = bs * ss
    assert ns_tok == Q_BLOCK and sp % Q_BLOCK == 0 and sp % KEY_TILE == 0

    wq, wk, wv, wqi, wki, wwi, wxr, wgr, wga, wgb = jnp.split(w_in[0], SPLIT_POINTS, axis=1)
    wt = jnp.concatenate([wq.T, wqi.T, wv.T, _pad_axis(wwi.T, 0, 16)], axis=0).astype(BF16)
    wn = jnp.concatenate([wk, wv, _pad_axis(wki, 1, LANES)], axis=1).astype(BF16)
    row = lambda v: v.reshape(1, -1)
    rg = (wxr.astype(BF16), wgr.astype(BF16), conv_w[0], row(conv_b[0]),
          w_rg_a[0].astype(BF16), row(b_rg_a[0]), w_rg_x[0].astype(BF16), row(b_rg_x[0]),
          row(lru_lambda[0]))
    mg = (wga.astype(BF16), wgb.astype(BF16), w_branch_attn[0].astype(BF16),
          w_branch_rnn[0].astype(BF16), w_out[0].astype(BF16), row(ln1_g[0]), row(ln1_b[0]))
    wr = _pad_axis(jnp.concatenate([w_router_group[0], w_router_expert[0]], axis=1), 1, LANES).astype(BF16)
    br = _pad_axis(jnp.concatenate([b_router_group[0], b_router_expert[0]]).reshape(1, -1), 1, LANES)
    mo = (wr, br, w_gate_up[0].astype(BF16), w_down[0].astype(BF16), row(ln2_g[0]), row(ln2_b[0]))

    pos_p = jnp.arange(sp, dtype=I32)
    qT, qiT, vT, wiT, k_p, kb_p, v_p, ki_p, kib_p = _proj(x_prompt, wt, wn, _rope_tables(pos_p), 512, KEY_TILE)
    lim_p = jnp.minimum((pos_p // CHUNK + 1) * CHUNK, sp).reshape(sp // Q_BLOCK, 1, Q_BLOCK)
    nk_p = (jnp.max(lim_p, axis=(1, 2)) + KEY_TILE - 1) // KEY_TILE
    attn_p = _attn(nk_p.astype(I32), lim_p, qT, qiT, wiT, kb_p, vT, kib_p)
    y_p, co_p, hl_p = _rglru(x_prompt, jnp.zeros((bp, SUBLANES, D_RNN), F32), jnp.zeros((bp, 1, D_RNN), F32),
                             *rg, ts=256)
    x1_p = _merge(x_prompt.reshape(bp * sp, D_MODEL), attn_p.reshape(bp * sp, ATT_WIDTH),
                  y_p.reshape(bp * sp, D_RNN), *mg, tm=512)
    out_p = _moe(x1_p, *mo, tm=1024)

    pos_s = past + (jnp.arange(ns_tok, dtype=I32) % ss)
    xs_flat = x_sample.reshape(1, ns_tok, D_MODEL)
    qT_s, qiT_s, _, wiT_s, k_s, kb_s, v_s, ki_s, kib_s = _proj(xs_flat, wt, wn, _rope_tables(pos_s), ns_tok, ns_tok)

    def per_batch_lanes(a):
        r = a.shape[1]
        return _pad_axis(a[0].reshape(r, bs, ss).transpose(1, 0, 2), 2, Q_BLOCK)

    l_all = past + ss
    lk = -(-l_all // KEY_TILE) * KEY_TILE
    k_all = jnp.concatenate([cache_k[0].reshape(bs, past, ATT_WIDTH).astype(BF16),
                             kb_s.reshape(bs, ss, ATT_WIDTH)], axis=1)
    v_all = jnp.concatenate([cache_v[0].reshape(bs, past, ATT_WIDTH),
                             v_s.reshape(bs, ss, ATT_WIDTH)], axis=1).astype(BF16)
    ki_all = jnp.concatenate([cache_k_idx[0].astype(BF16), kib_s.reshape(bs, ss, IDX_DIM)], axis=1)
    k_all, v_all, ki_all = (_pad_axis(a, 1, lk) for a in (k_all, v_all, ki_all))
    vT_all = v_all.reshape(bs, lk // KEY_TILE, KEY_TILE, ATT_WIDTH).transpose(0, 1, 3, 2)
    limit_s = min((past // CHUNK + 1) * CHUNK, l_all)
    assert (past + ss - 1) // CHUNK == past // CHUNK
    lim_s = jnp.where(jnp.arange(Q_BLOCK) < ss, limit_s, CHUNK).astype(I32).reshape(1, 1, Q_BLOCK)
    nk_s = jnp.full((1,), -(-limit_s // KEY_TILE), I32)
    attn_s = _attn(nk_s, lim_s, per_batch_lanes(qT_s), per_batch_lanes(qiT_s), per_batch_lanes(wiT_s),
                   k_all, vT_all, ki_all)[:, :ss]
    cs8 = jnp.pad(state_conv[0], ((0, 0), (SUBLANES - (CONV_WIDTH - 1), 0), (0, 0)))
    y_s, co_s, hl_s = _rglru(x_sample, cs8, state_h[0][:, None, :], *rg, ts=ss)
    x1_s = _merge(x_sample.reshape(ns_tok, D_MODEL), attn_s.reshape(ns_tok, ATT_WIDTH),
                  y_s.reshape(ns_tok, D_RNN), *mg, tm=ns_tok)
    out_s = _moe(x1_s, *mo, tm=ns_tok)

    keep = CONV_WIDTH - 1
    return (out_p.reshape(bp, sp, D_MODEL), out_s.reshape(bs, ss, D_MODEL),
            k_p.reshape(1, bp, sp, N_HEADS, HEAD_DIM), v_p.reshape(1, bp, sp, N_HEADS, HEAD_DIM),
            ki_p[None], co_p[:, SUBLANES - keep:][None], hl_p[:, 0][None],
            k_s.reshape(1, bs, ss, N_HEADS, HEAD_DIM), v_s.reshape(1, bs, ss, N_HEADS, HEAD_DIM),
            ki_s.reshape(1, bs, ss, IDX_DIM), co_s[:, SUBLANES - keep:][None], hl_s[:, 0][None])
```

```python
import functools

import jax
import jax.numpy as jnp
import numpy as np
from jax import lax
from jax.experimental import pallas as pl
from jax.experimental.pallas import tpu as pltpu

F32 = jnp.float32
BF16 = jnp.bfloat16
I32 = jnp.int32

D_MODEL = 1024
N_HEADS = 8
HEAD_DIM = 64
ATT_WIDTH = N_HEADS * HEAD_DIM
N_IDX_HEADS = 8
IDX_DIM = 64
MAX_TOPK = 256
CHUNK = 64
D_RNN = D_MODEL
N_RNN_BLOCKS = 8
RNN_BLOCK = D_RNN // N_RNN_BLOCKS
CONV_WIDTH = 4
LRU_C = 8.0
N_GROUPS = 4
EXPERTS_PER_GROUP = 4
N_EXPERTS = N_GROUPS * EXPERTS_PER_GROUP
D_EXPERT = 256
ROPE_THETA = 10000.0
LN_EPS = 1e-5
DEPTH = 1
DN_ALPHA = (2.0 * DEPTH) ** 0.25
SPLITS = (ATT_WIDTH, ATT_WIDTH, ATT_WIDTH, N_IDX_HEADS * IDX_DIM, IDX_DIM, N_IDX_HEADS,
          D_RNN, D_RNN, D_MODEL, D_MODEL)
SPLIT_POINTS = tuple(int(v) for v in np.cumsum(SPLITS)[:-1])

LANES = 128
SUBLANES = 8
VMEM_LIMIT = 56 * 1024 * 1024
Q_BLOCK = LANES
KEY_TILE = 256
INT_MIN = np.int32(-2 ** 31)
NEG_BIG = -1e30
KEY_LOWEST_FINITE = np.int32(-2 ** 31 + 2 ** 23)
WT_ROWS = 3 * ATT_WIDTH + 16
WN_COLS = 2 * ATT_WIDTH + LANES


def _params(semantics):
    return pltpu.CompilerParams(dimension_semantics=semantics, vmem_limit_bytes=VMEM_LIMIT)


def _dot(a, b):
    return jnp.dot(a, b, preferred_element_type=F32)


def _layer_norm(x, g, b):
    mu = jnp.mean(x, axis=-1, keepdims=True)
    xc = x - mu
    var = jnp.mean(xc * xc, axis=-1, keepdims=True)
    return xc * lax.rsqrt(var + LN_EPS) * g + b


def _proj_kernel(x_ref, wt_ref, wn_ref, cosn_ref, sinn_ref, cost_ref, sint_ref,
                 qT_ref, qiT_ref, vT_ref, wiT_ref, k_ref, kb_ref, v_ref, ki_ref, kib_ref):
    tm = x_ref.shape[1]
    kt_out = vT_ref.shape[3]
    xb = x_ref[0].astype(BF16)
    zt = lax.dot_general(wt_ref[...], xb, (((1,), (1,)), ((), ())), preferred_element_type=F32)
    zn = _dot(xb, wn_ref[...])
    cost = cost_ref[...]
    sint = sint_ref[...]
    half = HEAD_DIM // 2
    scale = HEAD_DIM ** -0.5
    for base, ref in ((0, qT_ref), (ATT_WIDTH, qiT_ref)):
        for h in range(N_HEADS):
            r = h * HEAD_DIM
            x1 = zt[base + r:base + r + half]
            x2 = zt[base + r + half:base + r + HEAD_DIM]
            ref[0, r:r + half, :] = ((x1 * cost - x2 * sint) * scale).astype(BF16)
            ref[0, r + half:r + HEAD_DIM, :] = ((x1 * sint + x2 * cost) * scale).astype(BF16)
    for c in range(tm // kt_out):
        vT_ref[0, c] = zt[2 * ATT_WIDTH:3 * ATT_WIDTH, c * kt_out:(c + 1) * kt_out].astype(BF16)
    wiT_ref[0] = zt[3 * ATT_WIDTH:3 * ATT_WIDTH + N_IDX_HEADS] * (N_IDX_HEADS ** -0.5)

    cosn = cosn_ref[...]
    sinn = sinn_ref[...]
    lane = lax.broadcasted_iota(I32, (tm, LANES), 1)
    first_half = (lane & half) == 0

    def rope_n(z):
        partner = jnp.where(first_half, pltpu.roll(z, LANES - half, 1), pltpu.roll(z, half, 1))
        return z * cosn + partner * sinn

    for g in range(ATT_WIDTH // LANES):
        kg = rope_n(zn[:, g * LANES:(g + 1) * LANES])
        k_ref[0, :, g * LANES:(g + 1) * LANES] = kg
        kb_ref[0, :, g * LANES:(g + 1) * LANES] = kg.astype(BF16)
    v_ref[0] = zn[:, ATT_WIDTH:2 * ATT_WIDTH]
    kig = rope_n(zn[:, 2 * ATT_WIDTH:2 * ATT_WIDTH + LANES])
    ki_ref[0] = kig[:, :IDX_DIM]
    kib_ref[0] = kig[:, :IDX_DIM].astype(BF16)


def _proj(x, wt, wn, tables, tm, kt_out):
    b, s, _ = x.shape
    cosn, sinn, cost, sint = tables
    ns = s // tm
    out_shape = (
        jax.ShapeDtypeStruct((b, ATT_WIDTH, s), BF16),
        jax.ShapeDtypeStruct((b, ATT_WIDTH, s), BF16),
        jax.ShapeDtypeStruct((b, s // kt_out, ATT_WIDTH, kt_out), BF16),
        jax.ShapeDtypeStruct((b, N_IDX_HEADS, s), F32),
        jax.ShapeDtypeStruct((b, s, ATT_WIDTH), F32),
        jax.ShapeDtypeStruct((b, s, ATT_WIDTH), BF16),
        jax.ShapeDtypeStruct((b, s, ATT_WIDTH), F32),
        jax.ShapeDtypeStruct((b, s, IDX_DIM), F32),
        jax.ShapeDtypeStruct((b, s, IDX_DIM), BF16),
    )
    tok = lambda w: pl.BlockSpec((1, tm, w), lambda i, j: (i, j, 0))
    feat = lambda r: pl.BlockSpec((1, r, tm), lambda i, j: (i, 0, j))
    return pl.pallas_call(
        _proj_kernel,
        grid=(b, ns),
        in_specs=[
            tok(D_MODEL),
            pl.BlockSpec((WT_ROWS, D_MODEL), lambda i, j: (0, 0)),
            pl.BlockSpec((D_MODEL, WN_COLS), lambda i, j: (0, 0)),
            pl.BlockSpec((tm, LANES), lambda i, j: (j, 0)),
            pl.BlockSpec((tm, LANES), lambda i, j: (j, 0)),
            pl.BlockSpec((HEAD_DIM // 2, tm), lambda i, j: (0, j)),
            pl.BlockSpec((HEAD_DIM // 2, tm), lambda i, j: (0, j)),
        ],
        out_specs=(
            feat(ATT_WIDTH), feat(ATT_WIDTH),
            pl.BlockSpec((1, tm // kt_out, ATT_WIDTH, kt_out), lambda i, j: (i, j, 0, 0)),
            feat(N_IDX_HEADS),
            tok(ATT_WIDTH), tok(ATT_WIDTH), tok(ATT_WIDTH), tok(IDX_DIM), tok(IDX_DIM),
        ),
        out_shape=out_shape,
        compiler_params=_params(("parallel", "parallel")),
        name="proj",
    )(x, wt, wn, cosn, sinn, cost, sint)


def _order_key_to_f32(key):
    return lax.bitcast_convert_type(jnp.where(key < 0, key ^ 0x7FFFFFFF, key), F32)


def _attn_kernel(nk_ref, lim_ref, qT_ref, qiT_ref, wiT_ref, k_ref, vT_ref, ki_ref, o_ref,
                 score_scr, bias_scr, logit_scr, acc_scr):
    nk = nk_ref[pl.program_id(1)]
    lim = lim_ref[0]
    w = wiT_ref[0]
    qi = qiT_ref[0]
    q = qT_ref[0]
    groups = KEY_TILE // SUBLANES

    def rows(kt):
        return pl.ds(pl.multiple_of(kt * KEY_TILE, KEY_TILE), KEY_TILE)

    def fold(x, op):
        return op(x.reshape(groups, SUBLANES, Q_BLOCK), axis=0)

    qi_pairs = [jnp.concatenate([qi[(2 * p) * IDX_DIM:(2 * p + 1) * IDX_DIM],
                                 qi[(2 * p + 1) * IDX_DIM:(2 * p + 2) * IDX_DIM]], axis=1)
                for p in range(N_IDX_HEADS // 2)]
    row_iota = lax.broadcasted_iota(I32, (KEY_TILE, Q_BLOCK), 0)

    def score_tile(kt, carry):
        ki_t = ki_ref[0, rows(kt), :]
        acc = jnp.zeros((KEY_TILE, Q_BLOCK), F32)
        for p in range(N_IDX_HEADS // 2):
            d = _dot(ki_t, qi_pairs[p])
            acc = acc + jnp.maximum(d[:, :Q_BLOCK], 0.0) * w[2 * p:2 * p + 1]
            acc = acc + jnp.maximum(d[:, Q_BLOCK:], 0.0) * w[2 * p + 1:2 * p + 2]
        score_scr[rows(kt), :] = jnp.where(row_iota + kt * KEY_TILE < lim, acc, -jnp.inf)
        return carry

    lax.fori_loop(0, nk, score_tile, 0)

    def count(pred):
        def body(kt, c):
            return c + fold(jnp.where(pred(score_scr[rows(kt), :]), 1, 0), jnp.sum)
        c = lax.fori_loop(0, nk, body, jnp.zeros((SUBLANES, Q_BLOCK), I32))
        return jnp.sum(c, axis=0, keepdims=True)

    def bisect(i, carry):
        t, cnt_t = carry
        cand = t + lax.shift_left(jnp.int32(1), 31 - i)
        cand_f = _order_key_to_f32(cand)
        cnt = count(lambda s: s >= cand_f)
        ok = cnt >= MAX_TOPK
        return jnp.where(ok, cand, t), jnp.where(ok, cnt, cnt_t)

    t0 = jnp.full((1, Q_BLOCK), INT_MIN, I32)
    thr, cnt_thr = lax.fori_loop(0, 32, bisect, (t0, jnp.full((1, Q_BLOCK), nk * KEY_TILE, I32)))
    found = thr >= KEY_LOWEST_FINITE
    thr_f = jnp.where(found, _order_key_to_f32(thr), jnp.finfo(F32).min)

    def bias_tile(kt, carry):
        bias_scr[rows(kt), :] = jnp.where(score_scr[rows(kt), :] >= thr_f, 0.0, NEG_BIG)
        return carry

    lax.fori_loop(0, nk, bias_tile, 0)

    tie = jnp.logical_and(cnt_thr > MAX_TOPK, found)
    any_tie = jnp.max(jnp.where(tie, 1.0, 0.0)) > 0.0

    @pl.when(any_tie)
    def _():
        need = (MAX_TOPK - count(lambda s: s > thr_f)).astype(F32)
        tri = (lax.broadcasted_iota(I32, (KEY_TILE, KEY_TILE), 1)
               <= lax.broadcasted_iota(I32, (KEY_TILE, KEY_TILE), 0))
        tri = jnp.where(tri, 1.0, 0.0).astype(BF16)

        def tie_tile(kt, before):
            s = score_scr[rows(kt), :]
            eq = s == thr_f
            rank = _dot(tri, jnp.where(eq, 1.0, 0.0).astype(BF16)) + before
            sel = jnp.logical_or(s > thr_f, jnp.logical_and(eq, rank <= need))
            bias_scr[rows(kt), :] = jnp.where(sel, 0.0, NEG_BIG)
            return rank[KEY_TILE - 1:KEY_TILE, :]

        lax.fori_loop(0, nk, tie_tile, jnp.zeros((1, Q_BLOCK), F32))

    zeros_half = jnp.zeros((HEAD_DIM, Q_BLOCK), BF16)
    rhs = [jnp.concatenate([q[h * HEAD_DIM:(h + 1) * HEAD_DIM], zeros_half] if h % 2 == 0
                           else [zeros_half, q[h * HEAD_DIM:(h + 1) * HEAD_DIM]], axis=0)
           for h in range(N_HEADS)]

    def logits_tile(kt, m_parts):
        bias = bias_scr[rows(kt), :]
        out = []
        for h in range(N_HEADS):
            k_t = k_ref[0, rows(kt), (h // 2) * LANES:(h // 2 + 1) * LANES]
            s = _dot(k_t, rhs[h]) + bias
            logit_scr[h, rows(kt), :] = s
            out.append(jnp.maximum(m_parts[h], fold(s, jnp.max)))
        return tuple(out)

    m_parts = lax.fori_loop(0, nk, logits_tile,
                            tuple(jnp.full((SUBLANES, Q_BLOCK), NEG_BIG, F32) for _ in range(N_HEADS)))
    m = [jnp.max(mp, axis=0, keepdims=True) for mp in m_parts]
    acc_scr[...] = jnp.zeros_like(acc_scr)

    def pv_tile(kt, l_parts):
        out = []
        for h in range(N_HEADS):
            p = jnp.exp(logit_scr[h, rows(kt), :] - m[h])
            out.append(l_parts[h] + fold(p, jnp.sum))
            v_t = vT_ref[0, kt, h * HEAD_DIM:(h + 1) * HEAD_DIM, :]
            acc_scr[h * HEAD_DIM:(h + 1) * HEAD_DIM, :] += _dot(v_t, p.astype(BF16))
        return tuple(out)

    l_parts = lax.fori_loop(0, nk, pv_tile,
                            tuple(jnp.zeros((SUBLANES, Q_BLOCK), F32) for _ in range(N_HEADS)))
    outs = [acc_scr[h * HEAD_DIM:(h + 1) * HEAD_DIM, :] / jnp.sum(l_parts[h], axis=0, keepdims=True)
            for h in range(N_HEADS)]
    o_ref[0] = jnp.concatenate(outs, axis=0).T.astype(BF16)


def _attn(nk, limits, qT, qiT, wiT, kb, vT, kib):
    b, _, sq = qT.shape
    nq = sq // Q_BLOCK
    lk = kb.shape[1]
    grid_spec = pltpu.PrefetchScalarGridSpec(
        num_scalar_prefetch=1,
        grid=(b, nq),
        in_specs=[
            pl.BlockSpec((1, 1, Q_BLOCK), lambda i, j, nk: (j, 0, 0)),
            pl.BlockSpec((1, ATT_WIDTH, Q_BLOCK), lambda i, j, nk: (i, 0, j)),
            pl.BlockSpec((1, ATT_WIDTH, Q_BLOCK), lambda i, j, nk: (i, 0, j)),
            pl.BlockSpec((1, N_IDX_HEADS, Q_BLOCK), lambda i, j, nk: (i, 0, j)),
            pl.BlockSpec((1, lk, ATT_WIDTH), lambda i, j, nk: (i, 0, 0)),
            pl.BlockSpec((1, lk // KEY_TILE, ATT_WIDTH, KEY_TILE), lambda i, j, nk: (i, 0, 0, 0)),
            pl.BlockSpec((1, lk, IDX_DIM), lambda i, j, nk: (i, 0, 0)),
        ],
        out_specs=pl.BlockSpec((1, Q_BLOCK, ATT_WIDTH), lambda i, j, nk: (i, j, 0)),
        scratch_shapes=[pltpu.VMEM((lk, Q_BLOCK), F32), pltpu.VMEM((lk, Q_BLOCK), F32),
                        pltpu.VMEM((N_HEADS, lk, Q_BLOCK), F32), pltpu.VMEM((ATT_WIDTH, Q_BLOCK), F32)],
    )
    return pl.pallas_call(
        _attn_kernel,
        grid_spec=grid_spec,
        out_shape=jax.ShapeDtypeStruct((b, sq, ATT_WIDTH), BF16),
        compiler_params=_params(("parallel", "arbitrary")),
        name="attn",
    )(nk, limits, qT, qiT, wiT, kb, vT, kib)


def _rglru_kernel(x_ref, cs_ref, h0_ref, wxr_ref, wgr_ref, cw_ref, cb_ref, wa_ref, ba_ref,
                  wx_ref, bx_ref, lam_ref, y_ref, co_ref, hl_ref,
                  xbuf, a_scr, u_scr, h_scr, hc_scr):
    ts = x_ref.shape[1]

    @pl.when(pl.program_id(1) == 0)
    def _():
        xbuf[0:SUBLANES, :] = cs_ref[0]
        hc_scr[...] = h0_ref[0]

    xb = x_ref[0].astype(BF16)
    xr = _dot(xb, wxr_ref[...])
    gr = _dot(xb, wgr_ref[...])
    xbuf[SUBLANES:SUBLANES + ts, :] = xr
    cw = cw_ref[...]
    xc = cb_ref[...] + xbuf[5:5 + ts, :] * cw[0:1]
    xc = xc + xbuf[6:6 + ts, :] * cw[1:2]
    xc = xc + xbuf[7:7 + ts, :] * cw[2:3]
    xc = xc + xr * cw[3:4]
    tail = xbuf[ts:ts + SUBLANES, :]
    xbuf[0:SUBLANES, :] = tail
    co_ref[0] = tail

    xcb = xc.astype(BF16)

    def gate(w_ref, b_ref):
        parts = [_dot(xcb[:, n * RNN_BLOCK:(n + 1) * RNN_BLOCK], w_ref[n]) for n in range(N_RNN_BLOCKS)]
        return jax.nn.sigmoid(jnp.concatenate(parts, axis=1) + b_ref[...])

    r = gate(wa_ref, ba_ref)
    i = gate(wx_ref, bx_ref)
    nl = -lam_ref[...]
    softplus = jnp.maximum(nl, 0.0) + jnp.log1p(jnp.exp(-jnp.abs(nl)))
    log_a = (-LRU_C * r) * softplus
    a_scr[...] = jnp.exp(log_a)
    th = jnp.tanh(log_a)
    u_scr[...] = jnp.sqrt(-2.0 * th / (1.0 - th)) * (i * xc)

    row = lax.broadcasted_iota(I32, (SUBLANES, D_RNN), 0)

    def group(g, h_prev):
        rs = pl.ds(pl.multiple_of(g * SUBLANES, SUBLANES), SUBLANES)
        a = a_scr[rs, :]
        b = u_scr[rs, :]
        for d in (1, 2, 4):
            a_sh = jnp.where(row >= d, pltpu.roll(a, d, 0), 1.0)
            b_sh = jnp.where(row >= d, pltpu.roll(b, d, 0), 0.0)
            b = a * b_sh + b
            a = a * a_sh
        h = b + a * h_prev
        h_scr[rs, :] = h
        return h[SUBLANES - 1:SUBLANES, :]

    h_last = lax.fori_loop(0, ts // SUBLANES, group, hc_scr[...])
    hc_scr[...] = h_last
    hl_ref[0] = h_last
    y_ref[0] = (h_scr[...] * jax.nn.gelu(gr)).astype(BF16)


def _rglru(x, conv_state8, h0, wxr, wgr, conv_w, conv_b, wa, ba, wx, bx, lam, ts):
    b, s, _ = x.shape
    const2 = lambda shape: pl.BlockSpec(shape, lambda i, j: (0, 0))
    const3 = lambda shape: pl.BlockSpec(shape, lambda i, j: (0, 0, 0))
    per_b = lambda r: pl.BlockSpec((1, r, D_RNN), lambda i, j: (i, 0, 0))
    return pl.pallas_call(
        _rglru_kernel,
        grid=(b, s // ts),
        in_specs=[
            pl.BlockSpec((1, ts, D_MODEL), lambda i, j: (i, j, 0)),
            per_b(SUBLANES), per_b(1),
            const2((D_MODEL, D_RNN)), const2((D_MODEL, D_RNN)),
            const2((CONV_WIDTH, D_RNN)), const2((1, D_RNN)),
            const3((N_RNN_BLOCKS, RNN_BLOCK, RNN_BLOCK)), const2((1, D_RNN)),
            const3((N_RNN_BLOCKS, RNN_BLOCK, RNN_BLOCK)), const2((1, D_RNN)),
            const2((1, D_RNN)),
        ],
        out_specs=(
            pl.BlockSpec((1, ts, D_RNN), lambda i, j: (i, j, 0)),
            per_b(SUBLANES), per_b(1),
        ),
        out_shape=(
            jax.ShapeDtypeStruct((b, s, D_RNN), BF16),
            jax.ShapeDtypeStruct((b, SUBLANES, D_RNN), F32),
            jax.ShapeDtypeStruct((b, 1, D_RNN), F32),
        ),
        scratch_shapes=[
            pltpu.VMEM((ts + SUBLANES, D_RNN), F32),
            pltpu.VMEM((ts, D_RNN), F32), pltpu.VMEM((ts, D_RNN), F32), pltpu.VMEM((ts, D_RNN), F32),
            pltpu.VMEM((1, D_RNN), F32),
        ],
        compiler_params=_params(("parallel", "arbitrary")),
        name="rglru",
    )(x, conv_state8, h0, wxr, wgr, conv_w, conv_b, wa, ba, wx, bx, lam)


def _merge_kernel(x_ref, attn_ref, y_ref, wga_ref, wgb_ref, wba_ref, wbr_ref, wout_ref, g_ref, b_ref,
                  o_ref):
    x = x_ref[...]
    xb = x.astype(BF16)
    ga = _dot(xb, wga_ref[...])
    gb = _dot(xb, wgb_ref[...])
    merged = (jax.nn.sigmoid(ga) * _dot(attn_ref[...], wba_ref[...])
              + jax.nn.sigmoid(gb) * _dot(y_ref[...], wbr_ref[...]))
    mix = _dot(merged.astype(BF16), wout_ref[...])
    o_ref[...] = _layer_norm(DN_ALPHA * x + mix, g_ref[...], b_ref[...])


def _merge(x, attn, y, wga, wgb, wba, wbr, wout, g, b, tm):
    t = x.shape[0]
    tok = lambda w: pl.BlockSpec((tm, w), lambda i: (i, 0))
    const = lambda shape: pl.BlockSpec(shape, lambda i: (0, 0))
    return pl.pallas_call(
        _merge_kernel,
        grid=(t // tm,),
        in_specs=[tok(D_MODEL), tok(ATT_WIDTH), tok(D_RNN),
                  const((D_MODEL, D_MODEL)), const((D_MODEL, D_MODEL)),
                  const((ATT_WIDTH, D_MODEL)), const((D_RNN, D_MODEL)), const((D_MODEL, D_MODEL)),
                  const((1, D_MODEL)), const((1, D_MODEL))],
        out_specs=tok(D_MODEL),
        out_shape=jax.ShapeDtypeStruct((t, D_MODEL), F32),
        compiler_params=_params(("parallel",)),
        name="merge",
    )(x, attn, y, wga, wgb, wba, wbr, wout, g, b)


def _moe_kernel(x_ref, wr_ref, br_ref, wgu_ref, wd_ref, g_ref, b_ref, o_ref, gate_scr, acc_scr, xb_scr):
    e = pl.program_id(1)
    tm = x_ref.shape[0]
    lane = lax.broadcasted_iota(I32, (tm, LANES), 1)

    @pl.when(e == 0)
    def _():
        xb = x_ref[...].astype(BF16)
        xb_scr[...] = xb
        acc_scr[...] = jnp.zeros_like(acc_scr)
        logits = _dot(xb, wr_ref[...]) + br_ref[...]
        neg_inf = -jnp.inf
        gl = jnp.where(lane < N_GROUPS, logits, neg_inf)
        gmax = jnp.max(gl, axis=1, keepdims=True)
        p_group = 1.0 / jnp.sum(jnp.exp(gl - gmax), axis=1, keepdims=True)
        g_sel = jnp.min(jnp.where(gl == gmax, lane, LANES), axis=1, keepdims=True)
        lo = N_GROUPS + EXPERTS_PER_GROUP * g_sel
        el = jnp.where(jnp.logical_and(lane >= lo, lane < lo + EXPERTS_PER_GROUP), logits, neg_inf)
        v1 = jnp.max(el, axis=1, keepdims=True)
        i1 = jnp.min(jnp.where(el == v1, lane, LANES), axis=1, keepdims=True)
        el2 = jnp.where(lane == i1, neg_inf, el)
        v2 = jnp.max(el2, axis=1, keepdims=True)
        i2 = jnp.min(jnp.where(el2 == v2, lane, LANES), axis=1, keepdims=True)
        e2 = jnp.exp(v2 - v1)
        p1 = 1.0 / (1.0 + e2)
        p2 = e2 * p1
        gate_scr[...] = (jnp.where(lane == i1, p1 * p_group, 0.0)
                         + jnp.where(lane == i2, p2 * p_group, 0.0))

    h = _dot(xb_scr[...], wgu_ref[0])
    act = jax.nn.silu(h[:, :D_EXPERT]) * h[:, D_EXPERT:]
    y = _dot(act.astype(BF16), wd_ref[0])
    gate_e = jnp.sum(jnp.where(lane == e + N_GROUPS, gate_scr[...], 0.0), axis=1, keepdims=True)
    acc_scr[...] = acc_scr[...] + gate_e * y

    @pl.when(e == N_EXPERTS - 1)
    def _():
        o_ref[...] = _layer_norm(DN_ALPHA * x_ref[...] + acc_scr[...], g_ref[...], b_ref[...])


def _moe(x, wr, br, wgu, wd, g, b, tm):
    t = x.shape[0]
    return pl.pallas_call(
        _moe_kernel,
        grid=(t // tm, N_EXPERTS),
        in_specs=[
            pl.BlockSpec((tm, D_MODEL), lambda i, e: (i, 0)),
            pl.BlockSpec((D_MODEL, LANES), lambda i, e: (0, 0)),
            pl.BlockSpec((1, LANES), lambda i, e: (0, 0)),
            pl.BlockSpec((1, D_MODEL, 2 * D_EXPERT), lambda i, e: (e, 0, 0)),
            pl.BlockSpec((1, D_EXPERT, D_MODEL), lambda i, e: (e, 0, 0)),
            pl.BlockSpec((1, D_MODEL), lambda i, e: (0, 0)),
            pl.BlockSpec((1, D_MODEL), lambda i, e: (0, 0)),
        ],
        out_specs=pl.BlockSpec((tm, D_MODEL), lambda i, e: (i, 0)),
        out_shape=jax.ShapeDtypeStruct((t, D_MODEL), F32),
        scratch_shapes=[pltpu.VMEM((tm, LANES), F32), pltpu.VMEM((tm, D_MODEL), F32),
                        pltpu.VMEM((tm, D_MODEL), BF16)],
        compiler_params=_params(("parallel", "arbitrary")),
        name="moe",
    )(x, wr, br, wgu, wd, g, b)


def _rope_tables(pos):
    half = HEAD_DIM // 2
    freqs = ROPE_THETA ** (-jnp.arange(half, dtype=F32) / half)
    ang = pos.astype(F32)[:, None] * freqs[None, :]
    cos, sin = jnp.cos(ang), jnp.sin(ang)
    cosn = jnp.tile(cos, (1, LANES // half))
    sinn = jnp.tile(jnp.concatenate([-sin, sin], axis=1), (1, LANES // HEAD_DIM))
    return cosn, sinn, cos.T, sin.T


def _pad_axis(a, axis, size):
    pad = [(0, 0)] * a.ndim
    pad[axis] = (0, size - a.shape[axis])
    return jnp.pad(a, pad)


def kernel(x_prompt, x_sample, cache_k, cache_v, cache_k_idx, state_conv, state_h, w_in, conv_w, conv_b, w_rg_a, b_rg_a, w_rg_x, b_rg_x, lru_lambda, w_branch_attn, w_branch_rnn, w_out, ln1_g, ln1_b, w_router_group, b_router_group, w_router_expert, b_router_expert, w_gate_up, w_down, ln2_g, ln2_b):
    assert w_in.shape[0] == DEPTH == 1
    bp, sp, _ = x_prompt.shape
    bs, ss, _ = x_sample.shape
    past = cache_k.shape[2]
    ns_tok = bs * ss
    assert ns_tok == Q_BLOCK and sp % Q_BLOCK == 0 and sp % KEY_TILE == 0

    wq, wk, wv, wqi, wki, wwi, wxr, wgr, wga, wgb = jnp.split(w_in[0], SPLIT_POINTS, axis=1)
    wt = jnp.concatenate([wq.T, wqi.T, wv.T, _pad_axis(wwi.T, 0, 16)], axis=0).astype(BF16)
    wn = jnp.concatenate([wk, wv, _pad_axis(wki, 1, LANES)], axis=1).astype(BF16)
    row = lambda v: v.reshape(1, -1)
    rg = (wxr.astype(BF16), wgr.astype(BF16), conv_w[0], row(conv_b[0]),
          w_rg_a[0].astype(BF16), row(b_rg_a[0]), w_rg_x[0].astype(BF16), row(b_rg_x[0]),
          row(lru_lambda[0]))
    mg = (wga.astype(BF16), wgb.astype(BF16), w_branch_attn[0].astype(BF16),
          w_branch_rnn[0].astype(BF16), w_out[0].astype(BF16), row(ln1_g[0]), row(ln1_b[0]))
    wr = _pad_axis(jnp.concatenate([w_router_group[0], w_router_expert[0]], axis=1), 1, LANES).astype(BF16)
    br = _pad_axis(jnp.concatenate([b_router_group[0], b_router_expert[0]]).reshape(1, -1), 1, LANES)
    mo = (wr, br, w_gate_up[0].astype(BF16), w_down[0].astype(BF16), row(ln2_g[0]), row(ln2_b[0]))

    pos_p = jnp.arange(sp, dtype=I32)
    qT, qiT, vT, wiT, k_p, kb_p, v_p, ki_p, kib_p = _proj(x_prompt, wt, wn, _rope_tables(pos_p), 512, KEY_TILE)
    lim_p = jnp.minimum((pos_p // CHUNK + 1) * CHUNK, sp).reshape(sp // Q_BLOCK, 1, Q_BLOCK)
    nk_p = (jnp.max(lim_p, axis=(1, 2)) + KEY_TILE - 1) // KEY_TILE
    attn_p = _attn(nk_p.astype(I32), lim_p, qT, qiT, wiT, kb_p, vT, kib_p)
    y_p, co_p, hl_p = _rglru(x_prompt, jnp.zeros((bp, SUBLANES, D_RNN), F32), jnp.zeros((bp, 1, D_RNN), F32),
                             *rg, ts=256)
    x1_p = _merge(x_prompt.reshape(bp * sp, D_MODEL), attn_p.reshape(bp * sp, ATT_WIDTH),
                  y_p.reshape(bp * sp, D_RNN), *mg, tm=512)
    out_p = _moe(x1_p, *mo, tm=1024)

    pos_s = past + (jnp.arange(ns_tok, dtype=I32) % ss)
    xs_flat = x_sample.reshape(1, ns_tok, D_MODEL)
    qT_s, qiT_s, _, wiT_s, k_s, kb_s, v_s, ki_s, kib_s = _proj(xs_flat, wt, wn, _rope_tables(pos_s), ns_tok, ns_tok)

    def per_batch_lanes(a):
        r = a.shape[1]
        return _pad_axis(a[0].reshape(r, bs, ss).transpose(1, 0, 2), 2, Q_BLOCK)

    l_all = past + ss
    lk = -(-l_all // KEY_TILE) * KEY_TILE
    k_all = jnp.concatenate([cache_k[0].reshape(bs, past, ATT_WIDTH).astype(BF16),
                             kb_s.reshape(bs, ss, ATT_WIDTH)], axis=1)
    v_all = jnp.concatenate([cache_v[0].reshape(bs, past, ATT_WIDTH),
                             v_s.reshape(bs, ss, ATT_WIDTH)], axis=1).astype(BF16)
    ki_all = jnp.concatenate([cache_k_idx[0].astype(BF16), kib_s.reshape(bs, ss, IDX_DIM)], axis=1)
    k_all, v_all, ki_all = (_pad_axis(a, 1, lk) for a in (k_all, v_all, ki_all))
    vT_all = v_all.reshape(bs, lk // KEY_TILE, KEY_TILE, ATT_WIDTH).transpose(0, 1, 3, 2)
    limit_s = min((past // CHUNK + 1) * CHUNK, l_all)
    assert (past + ss - 1) // CHUNK == past // CHUNK
    lim_s = jnp.where(jnp.arange(Q_BLOCK) < ss, limit_s, CHUNK).astype(I32).reshape(1, 1, Q_BLOCK)
    nk_s = jnp.full((1,), -(-limit_s // KEY_TILE), I32)
    attn_s = _attn(nk_s, lim_s, per_batch_lanes(qT_s), per_batch_lanes(qiT_s), per_batch_lanes(wiT_s),
                   k_all, vT_all, ki_all)[:, :ss]
    cs8 = jnp.pad(state_conv[0], ((0, 0), (SUBLANES - (CONV_WIDTH - 1), 0), (0, 0)))
    y_s, co_s, hl_s = _rglru(x_sample, cs8, state_h[0][:, None, :], *rg, ts=ss)
    x1_s = _merge(x_sample.reshape(ns_tok, D_MODEL), attn_s.reshape(ns_tok, ATT_WIDTH),
                  y_s.reshape(ns_tok, D_RNN), *mg, tm=ns_tok)
    out_s = _moe(x1_s, *mo, tm=ns_tok)

    keep = CONV_WIDTH - 1
    return (out_p.reshape(bp, sp, D_MODEL), out_s.reshape(bs, ss, D_MODEL),
            k_p.reshape(1, bp, sp, N_HEADS, HEAD_DIM), v_p.reshape(1, bp, sp, N_HEADS, HEAD_DIM),
            ki_p[None], co_p[:, SUBLANES - keep:][None], hl_p[:, 0][None],
            k_s.reshape(1, bs, ss, N_HEADS, HEAD_DIM), v_s.reshape(1, bs, ss, N_HEADS, HEAD_DIM),
            ki_s.reshape(1, bs, ss, IDX_DIM), co_s[:, SUBLANES - keep:][None], hl_s[:, 0][None])
```

```python
import functools

import jax
import jax.numpy as jnp
import numpy as np
from jax import lax
from jax.experimental import pallas as pl
from jax.experimental.pallas import tpu as pltpu

F32 = jnp.float32
BF16 = jnp.bfloat16
I32 = jnp.int32

D_MODEL = 1024
N_HEADS = 8
HEAD_DIM = 64
ATT_WIDTH = N_HEADS * HEAD_DIM
N_IDX_HEADS = 8
IDX_DIM = 64
MAX_TOPK = 256
CHUNK = 64
D_RNN = D_MODEL
N_RNN_BLOCKS = 8
RNN_BLOCK = D_RNN // N_RNN_BLOCKS
CONV_WIDTH = 4
LRU_C = 8.0
N_GROUPS = 4
EXPERTS_PER_GROUP = 4
N_EXPERTS = N_GROUPS * EXPERTS_PER_GROUP
D_EXPERT = 256
ROPE_THETA = 10000.0
LN_EPS = 1e-5
DEPTH = 1
DN_ALPHA = (2.0 * DEPTH) ** 0.25
SPLITS = (ATT_WIDTH, ATT_WIDTH, ATT_WIDTH, N_IDX_HEADS * IDX_DIM, IDX_DIM, N_IDX_HEADS,
          D_RNN, D_RNN, D_MODEL, D_MODEL)
SPLIT_POINTS = tuple(int(v) for v in np.cumsum(SPLITS)[:-1])

LANES = 128
SUBLANES = 8
VMEM_LIMIT = 56 * 1024 * 1024
Q_BLOCK = LANES
KEY_TILE = 256
INT_MIN = np.int32(-2 ** 31)
NEG_BIG = -1e30
KEY_LOWEST_FINITE = np.int32(-2 ** 31 + 2 ** 23)
LOG2_E = 1.4426950408889634
WT_ROWS = 3 * ATT_WIDTH + 16
WN_COLS = 2 * ATT_WIDTH + LANES


def _params(semantics):
    return pltpu.CompilerParams(dimension_semantics=semantics, vmem_limit_bytes=VMEM_LIMIT)


def _dot(a, b):
    return jnp.dot(a, b, preferred_element_type=F32)


def _layer_norm(x, g, b):
    mu = jnp.mean(x, axis=-1, keepdims=True)
    xc = x - mu
    var = jnp.mean(xc * xc, axis=-1, keepdims=True)
    return xc * lax.rsqrt(var + LN_EPS) * g + b


def _proj_kernel(x_ref, wt_ref, wn_ref, cosn_ref, sinn_ref, cost_ref, sint_ref,
                 qT_ref, qiT_ref, vT_ref, wiT_ref, k_ref, kb_ref, v_ref, ki_ref, kib_ref):
    tm = x_ref.shape[1]
    kt_out = vT_ref.shape[3]
    xb = x_ref[0].astype(BF16)
    zt = lax.dot_general(wt_ref[...], xb, (((1,), (1,)), ((), ())), preferred_element_type=F32)
    zn = _dot(xb, wn_ref[...])
    cost = cost_ref[...]
    sint = sint_ref[...]
    half = HEAD_DIM // 2
    for base, ref, scale in ((0, qT_ref, HEAD_DIM ** -0.5 * LOG2_E), (ATT_WIDTH, qiT_ref, IDX_DIM ** -0.5)):
        for h in range(N_HEADS):
            r = h * HEAD_DIM
            x1 = zt[base + r:base + r + half]
            x2 = zt[base + r + half:base + r + HEAD_DIM]
            ref[0, r:r + half, :] = ((x1 * cost - x2 * sint) * scale).astype(BF16)
            ref[0, r + half:r + HEAD_DIM, :] = ((x1 * sint + x2 * cost) * scale).astype(BF16)
    for c in range(tm // kt_out):
        vT_ref[0, c] = zt[2 * ATT_WIDTH:3 * ATT_WIDTH, c * kt_out:(c + 1) * kt_out].astype(BF16)
    wiT_ref[0] = zt[3 * ATT_WIDTH:3 * ATT_WIDTH + N_IDX_HEADS] * (N_IDX_HEADS ** -0.5)

    cosn = cosn_ref[...]
    sinn = sinn_ref[...]
    lane = lax.broadcasted_iota(I32, (tm, LANES), 1)
    first_half = (lane & half) == 0

    def rope_n(z):
        partner = jnp.where(first_half, pltpu.roll(z, LANES - half, 1), pltpu.roll(z, half, 1))
        return z * cosn + partner * sinn

    for g in range(ATT_WIDTH // LANES):
        kg = rope_n(zn[:, g * LANES:(g + 1) * LANES])
        k_ref[0, :, g * LANES:(g + 1) * LANES] = kg
        kb_ref[0, :, g * LANES:(g + 1) * LANES] = kg.astype(BF16)
    v_ref[0] = zn[:, ATT_WIDTH:2 * ATT_WIDTH]
    kig = rope_n(zn[:, 2 * ATT_WIDTH:2 * ATT_WIDTH + LANES])
    ki_ref[0] = kig[:, :IDX_DIM]
    kib_ref[0] = kig[:, :IDX_DIM].astype(BF16)


def _proj(x, wt, wn, tables, tm, kt_out):
    b, s, _ = x.shape
    cosn, sinn, cost, sint = tables
    ns = s // tm
    out_shape = (
        jax.ShapeDtypeStruct((b, ATT_WIDTH, s), BF16),
        jax.ShapeDtypeStruct((b, ATT_WIDTH, s), BF16),
        jax.ShapeDtypeStruct((b, s // kt_out, ATT_WIDTH, kt_out), BF16),
        jax.ShapeDtypeStruct((b, N_IDX_HEADS, s), F32),
        jax.ShapeDtypeStruct((b, s, ATT_WIDTH), F32),
        jax.ShapeDtypeStruct((b, s, ATT_WIDTH), BF16),
        jax.ShapeDtypeStruct((b, s, ATT_WIDTH), F32),
        jax.ShapeDtypeStruct((b, s, IDX_DIM), F32),
        jax.ShapeDtypeStruct((b, s, IDX_DIM), BF16),
    )
    tok = lambda w: pl.BlockSpec((1, tm, w), lambda i, j: (i, j, 0))
    feat = lambda r: pl.BlockSpec((1, r, tm), lambda i, j: (i, 0, j))
    return pl.pallas_call(
        _proj_kernel,
        grid=(b, ns),
        in_specs=[
            tok(D_MODEL),
            pl.BlockSpec((WT_ROWS, D_MODEL), lambda i, j: (0, 0)),
            pl.BlockSpec((D_MODEL, WN_COLS), lambda i, j: (0, 0)),
            pl.BlockSpec((tm, LANES), lambda i, j: (j, 0)),
            pl.BlockSpec((tm, LANES), lambda i, j: (j, 0)),
            pl.BlockSpec((HEAD_DIM // 2, tm), lambda i, j: (0, j)),
            pl.BlockSpec((HEAD_DIM // 2, tm), lambda i, j: (0, j)),
        ],
        out_specs=(
            feat(ATT_WIDTH), feat(ATT_WIDTH),
            pl.BlockSpec((1, tm // kt_out, ATT_WIDTH, kt_out), lambda i, j: (i, j, 0, 0)),
            feat(N_IDX_HEADS),
            tok(ATT_WIDTH), tok(ATT_WIDTH), tok(ATT_WIDTH), tok(IDX_DIM), tok(IDX_DIM),
        ),
        out_shape=out_shape,
        compiler_params=_params(("parallel", "parallel")),
        name="proj",
    )(x, wt, wn, cosn, sinn, cost, sint)


def _order_key_to_f32(key):
    return lax.bitcast_convert_type(jnp.where(key < 0, key ^ 0x7FFFFFFF, key), F32)


def _attn_kernel(nk_ref, lim_ref, qT_ref, qiT_ref, wiT_ref, k_ref, vT_ref, ki_ref, o_ref,
                 score_scr, bias_scr, logit_scr, acc_scr):
    nk = nk_ref[pl.program_id(1)]
    lim = lim_ref[0]
    w = wiT_ref[0]
    qi = qiT_ref[0]
    q = qT_ref[0]
    groups = KEY_TILE // SUBLANES

    def rows(kt):
        return pl.ds(pl.multiple_of(kt * KEY_TILE, KEY_TILE), KEY_TILE)

    def fold(x, op):
        return op(x.reshape(groups, SUBLANES, Q_BLOCK), axis=0)

    def tile_loop(body, init):
        c = lax.fori_loop(0, nk // 2, lambda i, c: body(2 * i + 1, body(2 * i, c)), init)
        return lax.cond(nk % 2 == 1, lambda c: body(nk - 1, c), lambda c: c, c)

    qi_pairs = [jnp.concatenate([qi[(2 * p) * IDX_DIM:(2 * p + 1) * IDX_DIM],
                                 qi[(2 * p + 1) * IDX_DIM:(2 * p + 2) * IDX_DIM]], axis=1)
                for p in range(N_IDX_HEADS // 2)]
    row_iota = lax.broadcasted_iota(I32, (KEY_TILE, Q_BLOCK), 0)

    def score_tile(kt, carry):
        ki_t = ki_ref[0, rows(kt), :]
        acc = jnp.zeros((KEY_TILE, Q_BLOCK), F32)
        for p in range(N_IDX_HEADS // 2):
            d = _dot(ki_t, qi_pairs[p])
            acc = acc + jnp.maximum(d[:, :Q_BLOCK], 0.0) * w[2 * p:2 * p + 1]
            acc = acc + jnp.maximum(d[:, Q_BLOCK:], 0.0) * w[2 * p + 1:2 * p + 2]
        score_scr[rows(kt), :] = jnp.where(row_iota + kt * KEY_TILE < lim, acc, -jnp.inf)
        return carry

    tile_loop(score_tile, 0)

    def count(pred):
        def body(kt, c):
            return c + fold(jnp.where(pred(score_scr[rows(kt), :]), 1, 0), jnp.sum)

        return jnp.sum(tile_loop(body, jnp.zeros((SUBLANES, Q_BLOCK), I32)), axis=0, keepdims=True)

    def bisect(i, carry):
        t, cnt_t = carry
        cand = t + lax.shift_left(jnp.int32(1), 31 - i)
        cand_f = _order_key_to_f32(cand)
        cnt = count(lambda s: s >= cand_f)
        ok = cnt >= MAX_TOPK
        return jnp.where(ok, cand, t), jnp.where(ok, cnt, cnt_t)

    t0 = jnp.full((1, Q_BLOCK), INT_MIN, I32)
    thr, cnt_thr = lax.fori_loop(0, 32, bisect, (t0, jnp.full((1, Q_BLOCK), nk * KEY_TILE, I32)))
    found = thr >= KEY_LOWEST_FINITE
    thr_f = jnp.where(found, _order_key_to_f32(thr), jnp.finfo(F32).min)

    def bias_tile(kt, carry):
        bias_scr[rows(kt), :] = jnp.where(score_scr[rows(kt), :] >= thr_f, 0.0, NEG_BIG)
        return carry

    tile_loop(bias_tile, 0)

    tie = jnp.logical_and(cnt_thr > MAX_TOPK, found)
    any_tie = jnp.max(jnp.where(tie, 1.0, 0.0)) > 0.0

    @pl.when(any_tie)
    def _():
        need = (MAX_TOPK - count(lambda s: s > thr_f)).astype(F32)
        tri = (lax.broadcasted_iota(I32, (KEY_TILE, KEY_TILE), 1)
               <= lax.broadcasted_iota(I32, (KEY_TILE, KEY_TILE), 0))
        tri = jnp.where(tri, 1.0, 0.0).astype(BF16)

        def tie_tile(kt, before):
            s = score_scr[rows(kt), :]
            eq = s == thr_f
            rank = _dot(tri, jnp.where(eq, 1.0, 0.0).astype(BF16)) + before
            sel = jnp.logical_or(s > thr_f, jnp.logical_and(eq, rank <= need))
            bias_scr[rows(kt), :] = jnp.where(sel, 0.0, NEG_BIG)
            return rank[KEY_TILE - 1:KEY_TILE, :]

        lax.fori_loop(0, nk, tie_tile, jnp.zeros((1, Q_BLOCK), F32))

    zeros_half = jnp.zeros((HEAD_DIM, Q_BLOCK), BF16)
    rhs = [jnp.concatenate(
        [jnp.concatenate([q[(2 * p) * HEAD_DIM:(2 * p + 1) * HEAD_DIM], zeros_half], axis=0),
         jnp.concatenate([zeros_half, q[(2 * p + 1) * HEAD_DIM:(2 * p + 2) * HEAD_DIM]], axis=0)], axis=1)
        for p in range(N_HEADS // 2)]

    def logits_tile(kt, m_parts):
        bias = bias_scr[rows(kt), :]
        out = []
        for p in range(N_HEADS // 2):
            s2 = _dot(k_ref[0, rows(kt), p * LANES:(p + 1) * LANES], rhs[p])
            for e in range(2):
                h = 2 * p + e
                s = s2[:, e * Q_BLOCK:(e + 1) * Q_BLOCK] + bias
                logit_scr[h, rows(kt), :] = s
                out.append(jnp.maximum(m_parts[h], fold(s, jnp.max)))
        return tuple(out)

    m_parts = tile_loop(logits_tile, tuple(jnp.full((SUBLANES, Q_BLOCK), NEG_BIG, F32) for _ in range(N_HEADS)))
    m = [jnp.max(mp, axis=0, keepdims=True) for mp in m_parts]
    acc_scr[...] = jnp.zeros_like(acc_scr)

    def pv_tile(kt, l_parts):
        out = []
        for h in range(N_HEADS):
            p = jnp.exp2(logit_scr[h, rows(kt), :] - m[h])
            out.append(l_parts[h] + fold(p, jnp.sum))
            v_t = vT_ref[0, kt, h * HEAD_DIM:(h + 1) * HEAD_DIM, :]
            acc_scr[h * HEAD_DIM:(h + 1) * HEAD_DIM, :] += _dot(v_t, p.astype(BF16))
        return tuple(out)

    l_parts = tile_loop(pv_tile, tuple(jnp.zeros((SUBLANES, Q_BLOCK), F32) for _ in range(N_HEADS)))
    outs = [acc_scr[h * HEAD_DIM:(h + 1) * HEAD_DIM, :] / jnp.sum(l_parts[h], axis=0, keepdims=True)
            for h in range(N_HEADS)]
    o_ref[0] = jnp.concatenate(outs, axis=0).T.astype(BF16)


def _attn(nk, limits, qT, qiT, wiT, kb, vT, kib):
    b, _, sq = qT.shape
    nq = sq // Q_BLOCK
    lk = kb.shape[1]
    grid_spec = pltpu.PrefetchScalarGridSpec(
        num_scalar_prefetch=1,
        grid=(b, nq),
        in_specs=[
            pl.BlockSpec((1, 1, Q_BLOCK), lambda i, j, nk: (j, 0, 0)),
            pl.BlockSpec((1, ATT_WIDTH, Q_BLOCK), lambda i, j, nk: (i, 0, j)),
            pl.BlockSpec((1, ATT_WIDTH, Q_BLOCK), lambda i, j, nk: (i, 0, j)),
            pl.BlockSpec((1, N_IDX_HEADS, Q_BLOCK), lambda i, j, nk: (i, 0, j)),
            pl.BlockSpec((1, lk, ATT_WIDTH), lambda i, j, nk: (i, 0, 0)),
            pl.BlockSpec((1, lk // KEY_TILE, ATT_WIDTH, KEY_TILE), lambda i, j, nk: (i, 0, 0, 0)),
            pl.BlockSpec((1, lk, IDX_DIM), lambda i, j, nk: (i, 0, 0)),
        ],
        out_specs=pl.BlockSpec((1, Q_BLOCK, ATT_WIDTH), lambda i, j, nk: (i, j, 0)),
        scratch_shapes=[pltpu.VMEM((lk, Q_BLOCK), F32), pltpu.VMEM((lk, Q_BLOCK), F32),
                        pltpu.VMEM((N_HEADS, lk, Q_BLOCK), F32), pltpu.VMEM((ATT_WIDTH, Q_BLOCK), F32)],
    )
    return pl.pallas_call(
        _attn_kernel,
        grid_spec=grid_spec,
        out_shape=jax.ShapeDtypeStruct((b, sq, ATT_WIDTH), BF16),
        compiler_params=_params(("parallel", "arbitrary")),
        name="attn",
    )(nk, limits, qT, qiT, wiT, kb, vT, kib)


def _rglru_kernel(x_ref, cs_ref, h0_ref, wxr_ref, wgr_ref, cw_ref, cb_ref, wa_ref, ba_ref,
                  wx_ref, bx_ref, lam_ref, y_ref, co_ref, hl_ref,
                  xbuf, a_scr, u_scr, h_scr, hc_scr):
    ts = x_ref.shape[1]

    @pl.when(pl.program_id(1) == 0)
    def _():
        xbuf[0:SUBLANES, :] = cs_ref[0]
        hc_scr[...] = h0_ref[0]

    xb = x_ref[0].astype(BF16)
    xr = _dot(xb, wxr_ref[...])
    gr = _dot(xb, wgr_ref[...])
    xbuf[SUBLANES:SUBLANES + ts, :] = xr
    cw = cw_ref[...]
    xc = cb_ref[...] + xbuf[5:5 + ts, :] * cw[0:1]
    xc = xc + xbuf[6:6 + ts, :] * cw[1:2]
    xc = xc + xbuf[7:7 + ts, :] * cw[2:3]
    xc = xc + xr * cw[3:4]
    tail = xbuf[ts:ts + SUBLANES, :]
    xbuf[0:SUBLANES, :] = tail
    co_ref[0] = tail

    xcb = xc.astype(BF16)

    def gate(w_ref, b_ref):
        parts = [_dot(xcb[:, n * RNN_BLOCK:(n + 1) * RNN_BLOCK], w_ref[n]) for n in range(N_RNN_BLOCKS)]
        return jax.nn.sigmoid(jnp.concatenate(parts, axis=1) + b_ref[...])

    r = gate(wa_ref, ba_ref)
    i = gate(wx_ref, bx_ref)
    nl = -lam_ref[...]
    softplus = jnp.maximum(nl, 0.0) + jnp.log1p(jnp.exp(-jnp.abs(nl)))
    log_a = (-LRU_C * r) * softplus
    a_scr[...] = jnp.exp(log_a)
    th = jnp.tanh(log_a)
    u_scr[...] = jnp.sqrt(-2.0 * th / (1.0 - th)) * (i * xc)

    row = lax.broadcasted_iota(I32, (SUBLANES, D_RNN), 0)

    def group(g, h_prev):
        rs = pl.ds(pl.multiple_of(g * SUBLANES, SUBLANES), SUBLANES)
        a = a_scr[rs, :]
        b = u_scr[rs, :]
        for d in (1, 2, 4):
            a_sh = jnp.where(row >= d, pltpu.roll(a, d, 0), 1.0)
            b_sh = jnp.where(row >= d, pltpu.roll(b, d, 0), 0.0)
            b = a * b_sh + b
            a = a * a_sh
        h = b + a * h_prev
        h_scr[rs, :] = h
        return h[SUBLANES - 1:SUBLANES, :]

    h_last = lax.fori_loop(0, ts // SUBLANES, group, hc_scr[...])
    hc_scr[...] = h_last
    hl_ref[0] = h_last
    y_ref[0] = (h_scr[...] * jax.nn.gelu(gr)).astype(BF16)


def _rglru(x, conv_state8, h0, wxr, wgr, conv_w, conv_b, wa, ba, wx, bx, lam, ts):
    b, s, _ = x.shape
    const2 = lambda shape: pl.BlockSpec(shape, lambda i, j: (0, 0))
    const3 = lambda shape: pl.BlockSpec(shape, lambda i, j: (0, 0, 0))
    per_b = lambda r: pl.BlockSpec((1, r, D_RNN), lambda i, j: (i, 0, 0))
    return pl.pallas_call(
        _rglru_kernel,
        grid=(b, s // ts),
        in_specs=[
            pl.BlockSpec((1, ts, D_MODEL), lambda i, j: (i, j, 0)),
            per_b(SUBLANES), per_b(1),
            const2((D_MODEL, D_RNN)), const2((D_MODEL, D_RNN)),
            const2((CONV_WIDTH, D_RNN)), const2((1, D_RNN)),
            const3((N_RNN_BLOCKS, RNN_BLOCK, RNN_BLOCK)), const2((1, D_RNN)),
            const3((N_RNN_BLOCKS, RNN_BLOCK, RNN_BLOCK)), const2((1, D_RNN)),
            const2((1, D_RNN)),
        ],
        out_specs=(
            pl.BlockSpec((1, ts, D_RNN), lambda i, j: (i, j, 0)),
            per_b(SUBLANES), per_b(1),
        ),
        out_shape=(
            jax.ShapeDtypeStruct((b, s, D_RNN), BF16),
            jax.ShapeDtypeStruct((b, SUBLANES, D_RNN), F32),
            jax.ShapeDtypeStruct((b, 1, D_RNN), F32),
        ),
        scratch_shapes=[
            pltpu.VMEM((ts + SUBLANES, D_RNN), F32),
            pltpu.VMEM((ts, D_RNN), F32), pltpu.VMEM((ts, D_RNN), F32), pltpu.VMEM((ts, D_RNN), F32),
            pltpu.VMEM((1, D_RNN), F32),
        ],
        compiler_params=_params(("parallel", "arbitrary")),
        name="rglru",
    )(x, conv_state8, h0, wxr, wgr, conv_w, conv_b, wa, ba, wx, bx, lam)


def _merge_kernel(x_ref, attn_ref, y_ref, wga_ref, wgb_ref, wba_ref, wbr_ref, wout_ref, g_ref, b_ref,
                  wr_ref, br_ref, o_ref, grp_ref):
    x = x_ref[...]
    xb = x.astype(BF16)
    ga = _dot(xb, wga_ref[...])
    gb = _dot(xb, wgb_ref[...])
    merged = (jax.nn.sigmoid(ga) * _dot(attn_ref[...], wba_ref[...])
              + jax.nn.sigmoid(gb) * _dot(y_ref[...], wbr_ref[...]))
    mix = _dot(merged.astype(BF16), wout_ref[...])
    x1 = _layer_norm(DN_ALPHA * x + mix, g_ref[...], b_ref[...])
    o_ref[...] = x1
    logits = _dot(x1.astype(BF16), wr_ref[...]) + br_ref[...]
    lane = lax.broadcasted_iota(I32, logits.shape, 1)
    gl = jnp.where(lane < N_GROUPS, logits, -jnp.inf)
    g_sel = jnp.min(jnp.where(gl == jnp.max(gl, axis=1, keepdims=True), lane, LANES), axis=1, keepdims=True)
    grp_ref[...] = jnp.broadcast_to(g_sel, grp_ref.shape)


def _merge(x, attn, y, wga, wgb, wba, wbr, wout, g, b, wr, br, tm):
    t = x.shape[0]
    tok = lambda w: pl.BlockSpec((tm, w), lambda i: (i, 0))
    const = lambda shape: pl.BlockSpec(shape, lambda i: (0, 0))
    return pl.pallas_call(
        _merge_kernel,
        grid=(t // tm,),
        in_specs=[tok(D_MODEL), tok(ATT_WIDTH), tok(D_RNN),
                  const((D_MODEL, D_MODEL)), const((D_MODEL, D_MODEL)),
                  const((ATT_WIDTH, D_MODEL)), const((D_RNN, D_MODEL)), const((D_MODEL, D_MODEL)),
                  const((1, D_MODEL)), const((1, D_MODEL)), const((D_MODEL, LANES)), const((1, LANES))],
        out_specs=(tok(D_MODEL), tok(LANES)),
        out_shape=(jax.ShapeDtypeStruct((t, D_MODEL), F32), jax.ShapeDtypeStruct((t, LANES), I32)),
        compiler_params=_params(("parallel",)),
        name="merge",
    )(x, attn, y, wga, wgb, wba, wbr, wout, g, b, wr, br)


def _route_plan(grp, tile):
    g = grp[:, 0]
    t = g.shape[0]
    onehot = (g[:, None] == jnp.arange(N_GROUPS, dtype=I32)[None, :]).astype(I32)
    csum = jnp.cumsum(onehot, axis=0)
    padded = (csum[-1] + tile - 1) // tile * tile
    ends = jnp.cumsum(padded)
    pos = jnp.sum((ends - padded)[None, :] * onehot, axis=1) + jnp.sum(csum * onehot, axis=1) - 1
    n_tiles = t // tile + N_GROUPS
    tile_start = jnp.arange(n_tiles, dtype=I32) * tile
    tile_group = jnp.minimum(jnp.sum((tile_start[:, None] >= ends[None, :]).astype(I32), axis=1), N_GROUPS - 1)
    return pos.astype(I32), tile_group.astype(I32), n_tiles


def _row_dma_all(copy, n):
    def start(r, c):
        copy(r).start()
        return c

    def wait(r, c):
        copy(r).wait()
        return c

    lax.fori_loop(0, n, start, 0, unroll=8)
    lax.fori_loop(0, n, wait, 0, unroll=8)


def _moe_sort_kernel(pos_ref, x_ref, init_ref, xs_ref, sem):
    del init_ref
    ts = x_ref.shape[0]
    base = pl.program_id(0) * ts
    _row_dma_all(lambda r: pltpu.make_async_copy(
        x_ref.at[pl.ds(r, 1)], xs_ref.at[pl.ds(pos_ref[base + r], 1)], sem), ts)


def _moe_expert_kernel(tg_ref, xs_ref, wr_ref, br_ref, wgu_ref, wd_ref, ys_ref):
    g = tg_ref[pl.program_id(0)]
    xb = xs_ref[...].astype(BF16)
    tm = xb.shape[0]
    lane = lax.broadcasted_iota(I32, (tm, LANES), 1)
    logits = _dot(xb, wr_ref[...]) + br_ref[...]
    neg_inf = -jnp.inf
    eg = jnp.exp(jnp.where(lane < N_GROUPS, logits, neg_inf)
                 - jnp.max(jnp.where(lane < N_GROUPS, logits, neg_inf), axis=1, keepdims=True))
    p_group = (jnp.sum(jnp.where(lane == g, eg, 0.0), axis=1, keepdims=True)
               / jnp.sum(eg, axis=1, keepdims=True))
    lo = N_GROUPS + EXPERTS_PER_GROUP * g
    el = jnp.where(jnp.logical_and(lane >= lo, lane < lo + EXPERTS_PER_GROUP), logits, neg_inf)
    v1 = jnp.max(el, axis=1, keepdims=True)
    i1 = jnp.min(jnp.where(el == v1, lane, LANES), axis=1, keepdims=True)
    el2 = jnp.where(lane == i1, neg_inf, el)
    v2 = jnp.max(el2, axis=1, keepdims=True)
    i2 = jnp.min(jnp.where(el2 == v2, lane, LANES), axis=1, keepdims=True)
    e2 = jnp.exp(v2 - v1)
    p1 = p_group / (1.0 + e2)
    p2 = e2 * p1
    h = _dot(xb, wgu_ref[0])
    acts = []
    for j in range(EXPERTS_PER_GROUP):
        gate = jnp.where(i1 == lo + j, p1, 0.0) + jnp.where(i2 == lo + j, p2, 0.0)
        hj = h[:, 2 * D_EXPERT * j:2 * D_EXPERT * (j + 1)]
        acts.append((gate * (jax.nn.silu(hj[:, :D_EXPERT]) * hj[:, D_EXPERT:])).astype(BF16))
    ys_ref[...] = _dot(jnp.concatenate(acts, axis=1), wd_ref[0])


def _moe_unsort_kernel(pos_ref, x_ref, ys_ref, g_ref, b_ref, o_ref, ybuf, sem):
    ts = x_ref.shape[0]
    base = pl.program_id(0) * ts
    _row_dma_all(lambda r: pltpu.make_async_copy(
        ys_ref.at[pl.ds(pos_ref[base + r], 1)], ybuf.at[pl.ds(r, 1)], sem), ts)
    o_ref[...] = _layer_norm(DN_ALPHA * x_ref[...] + ybuf[...], g_ref[...], b_ref[...])


def _moe(x, grp, wr, br, wgu, wd, g, b, ts, tile):
    t = x.shape[0]
    pos, tile_group, n_tiles = _route_plan(grp, tile)
    any_spec = pl.BlockSpec(memory_space=pl.ANY)
    xs = pl.pallas_call(
        _moe_sort_kernel,
        grid_spec=pltpu.PrefetchScalarGridSpec(
            num_scalar_prefetch=1, grid=(t // ts,),
            in_specs=[pl.BlockSpec((ts, D_MODEL), lambda i, pos: (i, 0)), any_spec],
            out_specs=any_spec,
            scratch_shapes=[pltpu.SemaphoreType.DMA(())]),
        out_shape=jax.ShapeDtypeStruct((n_tiles * tile, D_MODEL), F32),
        input_output_aliases={2: 0},
        compiler_params=_params(("arbitrary",)),
        name="moe_sort",
    )(pos, x, jnp.zeros((n_tiles * tile, D_MODEL), F32))
    ys = pl.pallas_call(
        _moe_expert_kernel,
        grid_spec=pltpu.PrefetchScalarGridSpec(
            num_scalar_prefetch=1, grid=(n_tiles,),
            in_specs=[
                pl.BlockSpec((tile, D_MODEL), lambda i, tg: (i, 0)),
                pl.BlockSpec((D_MODEL, LANES), lambda i, tg: (0, 0)),
                pl.BlockSpec((1, LANES), lambda i, tg: (0, 0)),
                pl.BlockSpec((1, D_MODEL, EXPERTS_PER_GROUP * 2 * D_EXPERT), lambda i, tg: (tg[i], 0, 0)),
                pl.BlockSpec((1, EXPERTS_PER_GROUP * D_EXPERT, D_MODEL), lambda i, tg: (tg[i], 0, 0)),
            ],
            out_specs=pl.BlockSpec((tile, D_MODEL), lambda i, tg: (i, 0))),
        out_shape=jax.ShapeDtypeStruct((n_tiles * tile, D_MODEL), F32),
        compiler_params=_params(("arbitrary",)),
        name="moe_experts",
    )(tile_group, xs, wr, br, wgu, wd)
    return pl.pallas_call(
        _moe_unsort_kernel,
        grid_spec=pltpu.PrefetchScalarGridSpec(
            num_scalar_prefetch=1, grid=(t // ts,),
            in_specs=[pl.BlockSpec((ts, D_MODEL), lambda i, pos: (i, 0)), any_spec,
                      pl.BlockSpec((1, D_MODEL), lambda i, pos: (0, 0)),
                      pl.BlockSpec((1, D_MODEL), lambda i, pos: (0, 0))],
            out_specs=pl.BlockSpec((ts, D_MODEL), lambda i, pos: (i, 0)),
            scratch_shapes=[pltpu.VMEM((ts, D_MODEL), F32), pltpu.SemaphoreType.DMA(())]),
        out_shape=jax.ShapeDtypeStruct((t, D_MODEL), F32),
        compiler_params=_params(("arbitrary",)),
        name="moe_unsort",
    )(pos, x, ys, g, b)


def _rope_tables(pos):
    half = HEAD_DIM // 2
    freqs = ROPE_THETA ** (-jnp.arange(half, dtype=F32) / half)
    ang = pos.astype(F32)[:, None] * freqs[None, :]
    cos, sin = jnp.cos(ang), jnp.sin(ang)
    cosn = jnp.tile(cos, (1, LANES // half))
    sinn = jnp.tile(jnp.concatenate([-sin, sin], axis=1), (1, LANES // HEAD_DIM))
    return cosn, sinn, cos.T, sin.T


def _pad_axis(a, axis, size):
    pad = [(0, 0)] * a.ndim
    pad[axis] = (0, size - a.shape[axis])
    return jnp.pad(a, pad)


def kernel(x_prompt, x_sample, cache_k, cache_v, cache_k_idx, state_conv, state_h, w_in, conv_w, conv_b, w_rg_a, b_rg_a, w_rg_x, b_rg_x, lru_lambda, w_branch_attn, w_branch_rnn, w_out, ln1_g, ln1_b, w_router_group, b_router_group, w_router_expert, b_router_expert, w_gate_up, w_down, ln2_g, ln2_b):
    assert w_in.shape[0] == DEPTH == 1
    bp, sp, _ = x_prompt.shape
    bs, ss, _ = x_sample.shape
    past = cache_k.shape[2]
    ns_tok = bs * ss
    assert ns_tok == Q_BLOCK and sp % Q_BLOCK == 0 and sp % KEY_TILE == 0

    wq, wk, wv, wqi, wki, wwi, wxr, wgr, wga, wgb = jnp.split(w_in[0], SPLIT_POINTS, axis=1)
    wt = jnp.concatenate([wq.T, wqi.T, wv.T, _pad_axis(wwi.T, 0, 16)], axis=0).astype(BF16)
    wn = jnp.concatenate([wk, wv, _pad_axis(wki, 1, LANES)], axis=1).astype(BF16)
    row = lambda v: v.reshape(1, -1)
    rg = (wxr.astype(BF16), wgr.astype(BF16), conv_w[0], row(conv_b[0]),
          w_rg_a[0].astype(BF16), row(b_rg_a[0]), w_rg_x[0].astype(BF16), row(b_rg_x[0]),
          row(lru_lambda[0]))
    mg = (wga.astype(BF16), wgb.astype(BF16), w_branch_attn[0].astype(BF16),
          w_branch_rnn[0].astype(BF16), w_out[0].astype(BF16), row(ln1_g[0]), row(ln1_b[0]))
    wr = _pad_axis(jnp.concatenate([w_router_group[0], w_router_expert[0]], axis=1), 1, LANES).astype(BF16)
    br = _pad_axis(jnp.concatenate([b_router_group[0], b_router_expert[0]]).reshape(1, -1), 1, LANES)
    wgu = (w_gate_up[0].reshape(N_GROUPS, EXPERTS_PER_GROUP, D_MODEL, 2 * D_EXPERT).transpose(0, 2, 1, 3)
           .reshape(N_GROUPS, D_MODEL, EXPERTS_PER_GROUP * 2 * D_EXPERT).astype(BF16))
    wd = w_down[0].reshape(N_GROUPS, EXPERTS_PER_GROUP * D_EXPERT, D_MODEL).astype(BF16)
    mo = (wr, br, wgu, wd, row(ln2_g[0]), row(ln2_b[0]))

    pos_p = jnp.arange(sp, dtype=I32)
    qT, qiT, vT, wiT, k_p, kb_p, v_p, ki_p, kib_p = _proj(x_prompt, wt, wn, _rope_tables(pos_p), 512, KEY_TILE)
    lim_p = jnp.minimum((pos_p // CHUNK + 1) * CHUNK, sp).reshape(sp // Q_BLOCK, 1, Q_BLOCK)
    nk_p = (jnp.max(lim_p, axis=(1, 2)) + KEY_TILE - 1) // KEY_TILE
    attn_p = _attn(nk_p.astype(I32), lim_p, qT, qiT, wiT, kb_p, vT, kib_p)
    y_p, co_p, hl_p = _rglru(x_prompt, jnp.zeros((bp, SUBLANES, D_RNN), F32), jnp.zeros((bp, 1, D_RNN), F32),
                             *rg, ts=256)
    x1_p, grp_p = _merge(x_prompt.reshape(bp * sp, D_MODEL), attn_p.reshape(bp * sp, ATT_WIDTH),
                         y_p.reshape(bp * sp, D_RNN), *mg, wr, br, tm=512)
    out_p = _moe(x1_p, grp_p, *mo, ts=256, tile=512)

    pos_s = past + (jnp.arange(ns_tok, dtype=I32) % ss)
    xs_flat = x_sample.reshape(1, ns_tok, D_MODEL)
    qT_s, qiT_s, _, wiT_s, k_s, kb_s, v_s, ki_s, kib_s = _proj(xs_flat, wt, wn, _rope_tables(pos_s), ns_tok, ns_tok)

    def per_batch_lanes(a):
        r = a.shape[1]
        return _pad_axis(a[0].reshape(r, bs, ss).transpose(1, 0, 2), 2, Q_BLOCK)

    l_all = past + ss
    lk = -(-l_all // KEY_TILE) * KEY_TILE
    k_all = jnp.concatenate([cache_k[0].reshape(bs, past, ATT_WIDTH).astype(BF16),
                             kb_s.reshape(bs, ss, ATT_WIDTH)], axis=1)
    v_all = jnp.concatenate([cache_v[0].reshape(bs, past, ATT_WIDTH),
                             v_s.reshape(bs, ss, ATT_WIDTH)], axis=1).astype(BF16)
    ki_all = jnp.concatenate([cache_k_idx[0].astype(BF16), kib_s.reshape(bs, ss, IDX_DIM)], axis=1)
    k_all, v_all, ki_all = (_pad_axis(a, 1, lk) for a in (k_all, v_all, ki_all))
    vT_all = v_all.reshape(bs, lk // KEY_TILE, KEY_TILE, ATT_WIDTH).transpose(0, 1, 3, 2)
    limit_s = min((past // CHUNK + 1) * CHUNK, l_all)
    assert (past + ss - 1) // CHUNK == past // CHUNK
    lim_s = jnp.where(jnp.arange(Q_BLOCK) < ss, limit_s, CHUNK).astype(I32).reshape(1, 1, Q_BLOCK)
    nk_s = jnp.full((1,), -(-limit_s // KEY_TILE), I32)
    attn_s = _attn(nk_s, lim_s, per_batch_lanes(qT_s), per_batch_lanes(qiT_s), per_batch_lanes(wiT_s),
                   k_all, vT_all, ki_all)[:, :ss]
    cs8 = jnp.pad(state_conv[0], ((0, 0), (SUBLANES - (CONV_WIDTH - 1), 0), (0, 0)))
    y_s, co_s, hl_s = _rglru(x_sample, cs8, state_h[0][:, None, :], *rg, ts=ss)
    x1_s, grp_s = _merge(x_sample.reshape(ns_tok, D_MODEL), attn_s.reshape(ns_tok, ATT_WIDTH),
                         y_s.reshape(ns_tok, D_RNN), *mg, wr, br, tm=ns_tok)
    out_s = _moe(x1_s, grp_s, *mo, ts=ns_tok, tile=ns_tok)

    keep = CONV_WIDTH - 1
    return (out_p.reshape(bp, sp, D_MODEL), out_s.reshape(bs, ss, D_MODEL),
            k_p.reshape(1, bp, sp, N_HEADS, HEAD_DIM), v_p.reshape(1, bp, sp, N_HEADS, HEAD_DIM),
            ki_p[None], co_p[:, SUBLANES - keep:][None], hl_p[:, 0][None],
            k_s.reshape(1, bs, ss, N_HEADS, HEAD_DIM), v_s.reshape(1, bs, ss, N_HEADS, HEAD_DIM),
            ki_s.reshape(1, bs, ss, IDX_DIM), co_s[:, SUBLANES - keep:][None], hl_s[:, 0][None])
```

```python
import functools

import jax
import jax.numpy as jnp
import numpy as np
from jax import lax
from jax.experimental import pallas as pl
from jax.experimental.pallas import tpu as pltpu

F32 = jnp.float32
BF16 = jnp.bfloat16
I32 = jnp.int32

D_MODEL = 1024
N_HEADS = 8
HEAD_DIM = 64
ATT_WIDTH = N_HEADS * HEAD_DIM
N_IDX_HEADS = 8
IDX_DIM = 64
MAX_TOPK = 256
CHUNK = 64
D_RNN = D_MODEL
N_RNN_BLOCKS = 8
RNN_BLOCK = D_RNN // N_RNN_BLOCKS
CONV_WIDTH = 4
LRU_C = 8.0
N_GROUPS = 4
EXPERTS_PER_GROUP = 4
N_EXPERTS = N_GROUPS * EXPERTS_PER_GROUP
D_EXPERT = 256
ROPE_THETA = 10000.0
LN_EPS = 1e-5
DEPTH = 1
DN_ALPHA = (2.0 * DEPTH) ** 0.25
SPLITS = (ATT_WIDTH, ATT_WIDTH, ATT_WIDTH, N_IDX_HEADS * IDX_DIM, IDX_DIM, N_IDX_HEADS,
          D_RNN, D_RNN, D_MODEL, D_MODEL)
SPLIT_POINTS = tuple(int(v) for v in np.cumsum(SPLITS)[:-1])

LANES = 128
SUBLANES = 8
VMEM_LIMIT = 56 * 1024 * 1024
Q_BLOCK = LANES
KEY_TILE = 256
INT_MIN = np.int32(-2 ** 31)
NEG_BIG = -1e30
KEY_LOWEST_FINITE = np.int32(-2 ** 31 + 2 ** 23)
LOG2_E = 1.4426950408889634
LATE_BITS = 4
WT_ROWS = 3 * ATT_WIDTH + 16
WN_COLS = 2 * ATT_WIDTH + LANES


def _params(semantics):
    return pltpu.CompilerParams(dimension_semantics=semantics, vmem_limit_bytes=VMEM_LIMIT)


def _dot(a, b):
    return jnp.dot(a, b, preferred_element_type=F32)


def _layer_norm(x, g, b):
    mu = jnp.mean(x, axis=-1, keepdims=True)
    xc = x - mu
    var = jnp.mean(xc * xc, axis=-1, keepdims=True)
    return xc * lax.rsqrt(var + LN_EPS) * g + b


def _proj_kernel(x_ref, wt_ref, wn_ref, cosn_ref, sinn_ref, cost_ref, sint_ref,
                 qT_ref, qiT_ref, vT_ref, wiT_ref, k_ref, kb_ref, v_ref, ki_ref, kib_ref):
    tm = x_ref.shape[1]
    kt_out = vT_ref.shape[3]
    xb = x_ref[0].astype(BF16)
    zt = lax.dot_general(wt_ref[...], xb, (((1,), (1,)), ((), ())), preferred_element_type=F32)
    zn = _dot(xb, wn_ref[...])
    cost = cost_ref[...]
    sint = sint_ref[...]
    half = HEAD_DIM // 2
    for base, ref, scale in ((0, qT_ref, HEAD_DIM ** -0.5 * LOG2_E), (ATT_WIDTH, qiT_ref, IDX_DIM ** -0.5)):
        for h in range(N_HEADS):
            r = h * HEAD_DIM
            x1 = zt[base + r:base + r + half]
            x2 = zt[base + r + half:base + r + HEAD_DIM]
            ref[0, r:r + half, :] = ((x1 * cost - x2 * sint) * scale).astype(BF16)
            ref[0, r + half:r + HEAD_DIM, :] = ((x1 * sint + x2 * cost) * scale).astype(BF16)
    for c in range(tm // kt_out):
        vT_ref[0, c] = zt[2 * ATT_WIDTH:3 * ATT_WIDTH, c * kt_out:(c + 1) * kt_out].astype(BF16)
    wiT_ref[0] = zt[3 * ATT_WIDTH:3 * ATT_WIDTH + N_IDX_HEADS] * (N_IDX_HEADS ** -0.5)

    cosn = cosn_ref[...]
    sinn = sinn_ref[...]
    lane = lax.broadcasted_iota(I32, (tm, LANES), 1)
    first_half = (lane & half) == 0

    def rope_n(z):
        partner = jnp.where(first_half, pltpu.roll(z, LANES - half, 1), pltpu.roll(z, half, 1))
        return z * cosn + partner * sinn

    for g in range(ATT_WIDTH // LANES):
        kg = rope_n(zn[:, g * LANES:(g + 1) * LANES])
        k_ref[0, :, g * LANES:(g + 1) * LANES] = kg
        kb_ref[0, :, g * LANES:(g + 1) * LANES] = kg.astype(BF16)
    v_ref[0] = zn[:, ATT_WIDTH:2 * ATT_WIDTH]
    kig = rope_n(zn[:, 2 * ATT_WIDTH:2 * ATT_WIDTH + LANES])
    ki_ref[0] = kig[:, :IDX_DIM]
    kib_ref[0] = kig[:, :IDX_DIM].astype(BF16)


def _proj(x, wt, wn, tables, tm, kt_out):
    b, s, _ = x.shape
    cosn, sinn, cost, sint = tables
    ns = s // tm
    out_shape = (
        jax.ShapeDtypeStruct((b, ATT_WIDTH, s), BF16),
        jax.ShapeDtypeStruct((b, ATT_WIDTH, s), BF16),
        jax.ShapeDtypeStruct((b, s // kt_out, ATT_WIDTH, kt_out), BF16),
        jax.ShapeDtypeStruct((b, N_IDX_HEADS, s), F32),
        jax.ShapeDtypeStruct((b, s, ATT_WIDTH), F32),
        jax.ShapeDtypeStruct((b, s, ATT_WIDTH), BF16),
        jax.ShapeDtypeStruct((b, s, ATT_WIDTH), F32),
        jax.ShapeDtypeStruct((b, s, IDX_DIM), F32),
        jax.ShapeDtypeStruct((b, s, IDX_DIM), BF16),
    )
    tok = lambda w: pl.BlockSpec((1, tm, w), lambda i, j: (i, j, 0))
    feat = lambda r: pl.BlockSpec((1, r, tm), lambda i, j: (i, 0, j))
    return pl.pallas_call(
        _proj_kernel,
        grid=(b, ns),
        in_specs=[
            tok(D_MODEL),
            pl.BlockSpec((WT_ROWS, D_MODEL), lambda i, j: (0, 0)),
            pl.BlockSpec((D_MODEL, WN_COLS), lambda i, j: (0, 0)),
            pl.BlockSpec((tm, LANES), lambda i, j: (j, 0)),
            pl.BlockSpec((tm, LANES), lambda i, j: (j, 0)),
            pl.BlockSpec((HEAD_DIM // 2, tm), lambda i, j: (0, j)),
            pl.BlockSpec((HEAD_DIM // 2, tm), lambda i, j: (0, j)),
        ],
        out_specs=(
            feat(ATT_WIDTH), feat(ATT_WIDTH),
            pl.BlockSpec((1, tm // kt_out, ATT_WIDTH, kt_out), lambda i, j: (i, j, 0, 0)),
            feat(N_IDX_HEADS),
            tok(ATT_WIDTH), tok(ATT_WIDTH), tok(ATT_WIDTH), tok(IDX_DIM), tok(IDX_DIM),
        ),
        out_shape=out_shape,
        compiler_params=_params(("parallel", "parallel")),
        name="proj",
    )(x, wt, wn, cosn, sinn, cost, sint)


def _order_key_to_f32(key):
    return lax.bitcast_convert_type(jnp.where(key < 0, key ^ 0x7FFFFFFF, key), F32)


def _attn_kernel(nk_ref, lim_ref, qT_ref, qiT_ref, wiT_ref, k_ref, vT_ref, ki_ref, o_ref,
                 score_scr, bias_scr, logit_scr, acc_scr):
    nk = nk_ref[pl.program_id(1)]
    lim = lim_ref[0]
    w = wiT_ref[0]
    qi = qiT_ref[0]
    q = qT_ref[0]
    groups = KEY_TILE // SUBLANES

    def rows(kt):
        return pl.ds(pl.multiple_of(kt * KEY_TILE, KEY_TILE), KEY_TILE)

    def fold(x, op):
        return op(x.reshape(groups, SUBLANES, Q_BLOCK), axis=0)

    def tile_loop(body, init):
        c = lax.fori_loop(0, nk // 2, lambda i, c: body(2 * i + 1, body(2 * i, c)), init)
        return lax.cond(nk % 2 == 1, lambda c: body(nk - 1, c), lambda c: c, c)

    qi_pairs = [jnp.concatenate([qi[(2 * p) * IDX_DIM:(2 * p + 1) * IDX_DIM],
                                 qi[(2 * p + 1) * IDX_DIM:(2 * p + 2) * IDX_DIM]], axis=1)
                for p in range(N_IDX_HEADS // 2)]
    row_iota = lax.broadcasted_iota(I32, (KEY_TILE, Q_BLOCK), 0)

    def score_tile(kt, carry):
        ki_t = ki_ref[0, rows(kt), :]
        acc = jnp.zeros((KEY_TILE, Q_BLOCK), F32)
        for p in range(N_IDX_HEADS // 2):
            d = _dot(ki_t, qi_pairs[p])
            acc = acc + jnp.maximum(d[:, :Q_BLOCK], 0.0) * w[2 * p:2 * p + 1]
            acc = acc + jnp.maximum(d[:, Q_BLOCK:], 0.0) * w[2 * p + 1:2 * p + 2]
        score_scr[rows(kt), :] = jnp.where(row_iota + kt * KEY_TILE < lim, acc, -jnp.inf)
        return carry

    tile_loop(score_tile, 0)

    def count(pred):
        def body(kt, c):
            return c + fold(jnp.where(pred(score_scr[rows(kt), :]), 1, 0), jnp.sum)

        return jnp.sum(tile_loop(body, jnp.zeros((SUBLANES, Q_BLOCK), I32)), axis=0, keepdims=True)

    def bisect_for(n_tiles):
        def bisect(i, carry):
            t, cnt_t = carry
            cand = t + lax.shift_left(jnp.int32(1), 31 - i)
            cand_f = _order_key_to_f32(cand)
            c = jnp.zeros((SUBLANES, Q_BLOCK), I32)
            for kt in range(n_tiles):
                c = c + fold(jnp.where(score_scr[kt * KEY_TILE:(kt + 1) * KEY_TILE, :] >= cand_f, 1, 0), jnp.sum)
            cnt = jnp.sum(c, axis=0, keepdims=True)
            ok = cnt >= MAX_TOPK
            return jnp.where(ok, cand, t), jnp.where(ok, cnt, cnt_t)

        def run():
            t0 = jnp.full((1, Q_BLOCK), INT_MIN, I32)
            c0 = jnp.full((1, Q_BLOCK), n_tiles * KEY_TILE, I32)
            t, c = lax.fori_loop(0, 32 - LATE_BITS, bisect, (t0, c0))
            settled = jnp.logical_or(c == MAX_TOPK, lim <= MAX_TOPK)
            all_settled = jnp.min(jnp.where(settled, 1.0, 0.0)) > 0.0
            return lax.cond(all_settled, lambda tc: tc,
                            lambda tc: lax.fori_loop(32 - LATE_BITS, 32, bisect, tc), (t, c))
        return run

    max_tiles = score_scr.shape[0] // KEY_TILE
    thr, cnt_thr = lax.switch(nk - 1, [bisect_for(n) for n in range(1, max_tiles + 1)])
    found = thr >= KEY_LOWEST_FINITE
    thr_f = jnp.where(found, _order_key_to_f32(thr), jnp.finfo(F32).min)

    def bias_tile(kt, carry):
        bias_scr[rows(kt), :] = jnp.where(score_scr[rows(kt), :] >= thr_f, 0.0, NEG_BIG)
        return carry

    tile_loop(bias_tile, 0)

    tie = jnp.logical_and(cnt_thr > MAX_TOPK, found)
    any_tie = jnp.max(jnp.where(tie, 1.0, 0.0)) > 0.0

    @pl.when(any_tie)
    def _():
        need = (MAX_TOPK - count(lambda s: s > thr_f)).astype(F32)
        tri = (lax.broadcasted_iota(I32, (KEY_TILE, KEY_TILE), 1)
               <= lax.broadcasted_iota(I32, (KEY_TILE, KEY_TILE), 0))
        tri = jnp.where(tri, 1.0, 0.0).astype(BF16)

        def tie_tile(kt, before):
            s = score_scr[rows(kt), :]
            eq = s == thr_f
            rank = _dot(tri, jnp.where(eq, 1.0, 0.0).astype(BF16)) + before
            sel = jnp.logical_or(s > thr_f, jnp.logical_and(eq, rank <= need))
            bias_scr[rows(kt), :] = jnp.where(sel, 0.0, NEG_BIG)
            return rank[KEY_TILE - 1:KEY_TILE, :]

        lax.fori_loop(0, nk, tie_tile, jnp.zeros((1, Q_BLOCK), F32))

    zeros_half = jnp.zeros((HEAD_DIM, Q_BLOCK), BF16)
    rhs = [jnp.concatenate(
        [jnp.concatenate([q[(2 * p) * HEAD_DIM:(2 * p + 1) * HEAD_DIM], zeros_half], axis=0),
         jnp.concatenate([zeros_half, q[(2 * p + 1) * HEAD_DIM:(2 * p + 2) * HEAD_DIM]], axis=0)], axis=1)
        for p in range(N_HEADS // 2)]

    def logits_tile(kt, m_parts):
        bias = bias_scr[rows(kt), :]
        out = []
        for p in range(N_HEADS // 2):
            s2 = _dot(k_ref[0, rows(kt), p * LANES:(p + 1) * LANES], rhs[p])
            for e in range(2):
                h = 2 * p + e
                s = s2[:, e * Q_BLOCK:(e + 1) * Q_BLOCK] + bias
                logit_scr[h, rows(kt), :] = s
                out.append(jnp.maximum(m_parts[h], fold(s, jnp.max)))
        return tuple(out)

    m_parts = tile_loop(logits_tile, tuple(jnp.full((SUBLANES, Q_BLOCK), NEG_BIG, F32) for _ in range(N_HEADS)))
    m = [jnp.max(mp, axis=0, keepdims=True) for mp in m_parts]
    acc_scr[...] = jnp.zeros_like(acc_scr)

    def pv_tile(kt, l_parts):
        out = []
        for h in range(N_HEADS):
            p = jnp.exp2(logit_scr[h, rows(kt), :] - m[h])
            out.append(l_parts[h] + fold(p, jnp.sum))
            v_t = vT_ref[0, kt, h * HEAD_DIM:(h + 1) * HEAD_DIM, :]
            acc_scr[h * HEAD_DIM:(h + 1) * HEAD_DIM, :] += _dot(v_t, p.astype(BF16))
        return tuple(out)

    l_parts = tile_loop(pv_tile, tuple(jnp.zeros((SUBLANES, Q_BLOCK), F32) for _ in range(N_HEADS)))
    outs = [acc_scr[h * HEAD_DIM:(h + 1) * HEAD_DIM, :] / jnp.sum(l_parts[h], axis=0, keepdims=True)
            for h in range(N_HEADS)]
    o_ref[0] = jnp.concatenate(outs, axis=0).T.astype(BF16)


def _attn(nk, limits, qT, qiT, wiT, kb, vT, kib):
    b, _, sq = qT.shape
    nq = sq // Q_BLOCK
    lk = kb.shape[1]
    grid_spec = pltpu.PrefetchScalarGridSpec(
        num_scalar_prefetch=1,
        grid=(b, nq),
        in_specs=[
            pl.BlockSpec((1, 1, Q_BLOCK), lambda i, j, nk: (j, 0, 0)),
            pl.BlockSpec((1, ATT_WIDTH, Q_BLOCK), lambda i, j, nk: (i, 0, j)),
            pl.BlockSpec((1, ATT_WIDTH, Q_BLOCK), lambda i, j, nk: (i, 0, j)),
            pl.BlockSpec((1, N_IDX_HEADS, Q_BLOCK), lambda i, j, nk: (i, 0, j)),
            pl.BlockSpec((1, lk, ATT_WIDTH), lambda i, j, nk: (i, 0, 0)),
            pl.BlockSpec((1, lk // KEY_TILE, ATT_WIDTH, KEY_TILE), lambda i, j, nk: (i, 0, 0, 0)),
            pl.BlockSpec((1, lk, IDX_DIM), lambda i, j, nk: (i, 0, 0)),
        ],
        out_specs=pl.BlockSpec((1, Q_BLOCK, ATT_WIDTH), lambda i, j, nk: (i, j, 0)),
        scratch_shapes=[pltpu.VMEM((lk, Q_BLOCK), F32), pltpu.VMEM((lk, Q_BLOCK), F32),
                        pltpu.VMEM((N_HEADS, lk, Q_BLOCK), F32), pltpu.VMEM((ATT_WIDTH, Q_BLOCK), F32)],
    )
    return pl.pallas_call(
        _attn_kernel,
        grid_spec=grid_spec,
        out_shape=jax.ShapeDtypeStruct((b, sq, ATT_WIDTH), BF16),
        compiler_params=_params(("parallel", "arbitrary")),
        name="attn",
    )(nk, limits, qT, qiT, wiT, kb, vT, kib)


def _rglru_kernel(x_ref, cs_ref, h0_ref, wxr_ref, wgr_ref, cw_ref, cb_ref, wa_ref, ba_ref,
                  wx_ref, bx_ref, lam_ref, y_ref, co_ref, hl_ref,
                  xbuf, a_scr, u_scr, h_scr, hc_scr):
    ts = x_ref.shape[1]

    @pl.when(pl.program_id(1) == 0)
    def _():
        xbuf[0:SUBLANES, :] = cs_ref[0]
        hc_scr[...] = h0_ref[0]

    xb = x_ref[0].astype(BF16)
    xr = _dot(xb, wxr_ref[...])
    gr = _dot(xb, wgr_ref[...])
    xbuf[SUBLANES:SUBLANES + ts, :] = xr
    cw = cw_ref[...]
    xc = cb_ref[...] + xbuf[5:5 + ts, :] * cw[0:1]
    xc = xc + xbuf[6:6 + ts, :] * cw[1:2]
    xc = xc + xbuf[7:7 + ts, :] * cw[2:3]
    xc = xc + xr * cw[3:4]
    tail = xbuf[ts:ts + SUBLANES, :]
    xbuf[0:SUBLANES, :] = tail
    co_ref[0] = tail

    xcb = xc.astype(BF16)

    def gate(w_ref, b_ref):
        parts = [_dot(xcb[:, n * RNN_BLOCK:(n + 1) * RNN_BLOCK], w_ref[n]) for n in range(N_RNN_BLOCKS)]
        return jax.nn.sigmoid(jnp.concatenate(parts, axis=1) + b_ref[...])

    r = gate(wa_ref, ba_ref)
    i = gate(wx_ref, bx_ref)
    nl = -lam_ref[...]
    softplus = jnp.maximum(nl, 0.0) + jnp.log1p(jnp.exp(-jnp.abs(nl)))
    log_a = (-LRU_C * r) * softplus
    a_scr[...] = jnp.exp(log_a)
    th = jnp.tanh(log_a)
    u_scr[...] = jnp.sqrt(-2.0 * th / (1.0 - th)) * (i * xc)

    row = lax.broadcasted_iota(I32, (SUBLANES, D_RNN), 0)

    def group(g, h_prev):
        rs = pl.ds(pl.multiple_of(g * SUBLANES, SUBLANES), SUBLANES)
        a = a_scr[rs, :]
        b = u_scr[rs, :]
        for d in (1, 2, 4):
            a_sh = jnp.where(row >= d, pltpu.roll(a, d, 0), 1.0)
            b_sh = jnp.where(row >= d, pltpu.roll(b, d, 0), 0.0)
            b = a * b_sh + b
            a = a * a_sh
        h = b + a * h_prev
        h_scr[rs, :] = h
        return h[SUBLANES - 1:SUBLANES, :]

    h_last = lax.fori_loop(0, ts // SUBLANES, group, hc_scr[...])
    hc_scr[...] = h_last
    hl_ref[0] = h_last
    y_ref[0] = (h_scr[...] * jax.nn.gelu(gr)).astype(BF16)


def _rglru(x, conv_state8, h0, wxr, wgr, conv_w, conv_b, wa, ba, wx, bx, lam, ts):
    b, s, _ = x.shape
    const2 = lambda shape: pl.BlockSpec(shape, lambda i, j: (0, 0))
    const3 = lambda shape: pl.BlockSpec(shape, lambda i, j: (0, 0, 0))
    per_b = lambda r: pl.BlockSpec((1, r, D_RNN), lambda i, j: (i, 0, 0))
    return pl.pallas_call(
        _rglru_kernel,
        grid=(b, s // ts),
        in_specs=[
            pl.BlockSpec((1, ts, D_MODEL), lambda i, j: (i, j, 0)),
            per_b(SUBLANES), per_b(1),
            const2((D_MODEL, D_RNN)), const2((D_MODEL, D_RNN)),
            const2((CONV_WIDTH, D_RNN)), const2((1, D_RNN)),
            const3((N_RNN_BLOCKS, RNN_BLOCK, RNN_BLOCK)), const2((1, D_RNN)),
            const3((N_RNN_BLOCKS, RNN_BLOCK, RNN_BLOCK)), const2((1, D_RNN)),
            const2((1, D_RNN)),
        ],
        out_specs=(
            pl.BlockSpec((1, ts, D_RNN), lambda i, j: (i, j, 0)),
            per_b(SUBLANES), per_b(1),
        ),
        out_shape=(
            jax.ShapeDtypeStruct((b, s, D_RNN), BF16),
            jax.ShapeDtypeStruct((b, SUBLANES, D_RNN), F32),
            jax.ShapeDtypeStruct((b, 1, D_RNN), F32),
        ),
        scratch_shapes=[
            pltpu.VMEM((ts + SUBLANES, D_RNN), F32),
            pltpu.VMEM((ts, D_RNN), F32), pltpu.VMEM((ts, D_RNN), F32), pltpu.VMEM((ts, D_RNN), F32),
            pltpu.VMEM((1, D_RNN), F32),
        ],
        compiler_params=_params(("parallel", "arbitrary")),
        name="rglru",
    )(x, conv_state8, h0, wxr, wgr, conv_w, conv_b, wa, ba, wx, bx, lam)


def _merge_kernel(x_ref, attn_ref, y_ref, wga_ref, wgb_ref, wba_ref, wbr_ref, wout_ref, g_ref, b_ref,
                  wr_ref, br_ref, o_ref, grp_ref):
    x = x_ref[...]
    xb = x.astype(BF16)
    ga = _dot(xb, wga_ref[...])
    gb = _dot(xb, wgb_ref[...])
    merged = (jax.nn.sigmoid(ga) * _dot(attn_ref[...], wba_ref[...])
              + jax.nn.sigmoid(gb) * _dot(y_ref[...], wbr_ref[...]))
    mix = _dot(merged.astype(BF16), wout_ref[...])
    x1 = _layer_norm(DN_ALPHA * x + mix, g_ref[...], b_ref[...])
    o_ref[...] = x1
    logits = _dot(x1.astype(BF16), wr_ref[...]) + br_ref[...]
    lane = lax.broadcasted_iota(I32, logits.shape, 1)
    gl = jnp.where(lane < N_GROUPS, logits, -jnp.inf)
    g_sel = jnp.min(jnp.where(gl == jnp.max(gl, axis=1, keepdims=True), lane, LANES), axis=1, keepdims=True)
    grp_ref[...] = jnp.broadcast_to(g_sel, grp_ref.shape)


def _merge(x, attn, y, wga, wgb, wba, wbr, wout, g, b, wr, br, tm):
    t = x.shape[0]
    tok = lambda w: pl.BlockSpec((tm, w), lambda i: (i, 0))
    const = lambda shape: pl.BlockSpec(shape, lambda i: (0, 0))
    return pl.pallas_call(
        _merge_kernel,
        grid=(t // tm,),
        in_specs=[tok(D_MODEL), tok(ATT_WIDTH), tok(D_RNN),
                  const((D_MODEL, D_MODEL)), const((D_MODEL, D_MODEL)),
                  const((ATT_WIDTH, D_MODEL)), const((D_RNN, D_MODEL)), const((D_MODEL, D_MODEL)),
                  const((1, D_MODEL)), const((1, D_MODEL)), const((D_MODEL, LANES)), const((1, LANES))],
        out_specs=(tok(D_MODEL), tok(LANES)),
        out_shape=(jax.ShapeDtypeStruct((t, D_MODEL), F32), jax.ShapeDtypeStruct((t, LANES), I32)),
        compiler_params=_params(("parallel",)),
        name="merge",
    )(x, attn, y, wga, wgb, wba, wbr, wout, g, b, wr, br)


def _route_plan(grp, tile):
    g = grp[:, 0]
    t = g.shape[0]
    onehot = (g[:, None] == jnp.arange(N_GROUPS, dtype=I32)[None, :]).astype(I32)
    csum = jnp.cumsum(onehot, axis=0)
    padded = (csum[-1] + tile - 1) // tile * tile
    ends = jnp.cumsum(padded)
    pos = jnp.sum((ends - padded)[None, :] * onehot, axis=1) + jnp.sum(csum * onehot, axis=1) - 1
    n_tiles = t // tile + N_GROUPS
    tile_start = jnp.arange(n_tiles, dtype=I32) * tile
    tile_group = jnp.minimum(jnp.sum((tile_start[:, None] >= ends[None, :]).astype(I32), axis=1), N_GROUPS - 1)
    return pos.astype(I32), tile_group.astype(I32), n_tiles


ROW_DMA_UNROLL = 8


def _start_rows(copy, n):
    def batch(i, c):
        for u in range(ROW_DMA_UNROLL):
            copy(i * ROW_DMA_UNROLL + u).start(priority=u % 2)
        return c

    lax.fori_loop(0, n // ROW_DMA_UNROLL, batch, 0)


def _wait_rows(copy, n):
    def batch(i, c):
        for _ in range(ROW_DMA_UNROLL):
            copy.wait()
        return c

    lax.fori_loop(0, n // ROW_DMA_UNROLL, batch, 0)


def _moe_sort_kernel(pos_ref, x_ref, init_ref, xs_ref, sems, *, rows):
    del init_ref
    i = pl.program_id(0)
    base = i * rows
    slot = i % 2

    def copy(r, s):
        return pltpu.make_async_copy(x_ref.at[pl.ds(base + r, 1)],
                                     xs_ref.at[pl.ds(pos_ref[base + r], 1)], sems.at[s])

    _start_rows(lambda r: copy(r, slot), rows)

    @pl.when(i > 0)
    def _():
        _wait_rows(copy(0, 1 - slot), rows)

    @pl.when(i == pl.num_programs(0) - 1)
    def _():
        _wait_rows(copy(0, slot), rows)


def _moe_expert_kernel(tg_ref, xs_ref, wr_ref, br_ref, wgu_ref, wd_ref, ys_ref):
    g = tg_ref[pl.program_id(0)]
    xb = xs_ref[...].astype(BF16)
    tm = xb.shape[0]
    lane = lax.broadcasted_iota(I32, (tm, LANES), 1)
    logits = _dot(xb, wr_ref[...]) + br_ref[...]
    neg_inf = -jnp.inf
    eg = jnp.exp(jnp.where(lane < N_GROUPS, logits, neg_inf)
                 - jnp.max(jnp.where(lane < N_GROUPS, logits, neg_inf), axis=1, keepdims=True))
    p_group = (jnp.sum(jnp.where(lane == g, eg, 0.0), axis=1, keepdims=True)
               / jnp.sum(eg, axis=1, keepdims=True))
    lo = N_GROUPS + EXPERTS_PER_GROUP * g
    el = jnp.where(jnp.logical_and(lane >= lo, lane < lo + EXPERTS_PER_GROUP), logits, neg_inf)
    v1 = jnp.max(el, axis=1, keepdims=True)
    i1 = jnp.min(jnp.where(el == v1, lane, LANES), axis=1, keepdims=True)
    el2 = jnp.where(lane == i1, neg_inf, el)
    v2 = jnp.max(el2, axis=1, keepdims=True)
    i2 = jnp.min(jnp.where(el2 == v2, lane, LANES), axis=1, keepdims=True)
    e2 = jnp.exp(v2 - v1)
    p1 = p_group / (1.0 + e2)
    p2 = e2 * p1
    acts = []
    for j in range(EXPERTS_PER_GROUP):
        gate = jnp.where(i1 == lo + j, p1, 0.0) + jnp.where(i2 == lo + j, p2, 0.0)
        hj = _dot(xb, wgu_ref[j])
        acts.append((gate * (jax.nn.silu(hj[:, :D_EXPERT]) * hj[:, D_EXPERT:])).astype(BF16))
    ys_ref[...] = _dot(jnp.concatenate(acts, axis=1), wd_ref[0])


def _moe_unsort_kernel(pos_ref, x_ref, ys_ref, g_ref, b_ref, o_ref, ybuf, sems):
    ts = x_ref.shape[0]
    i = pl.program_id(0)
    slot = i % 2

    def copy(step, r, s):
        return pltpu.make_async_copy(ys_ref.at[pl.ds(pos_ref[step * ts + r], 1)],
                                     ybuf.at[s, pl.ds(r, 1)], sems.at[s])

    @pl.when(i == 0)
    def _():
        _start_rows(lambda r: copy(0, r, 0), ts)

    @pl.when(i < pl.num_programs(0) - 1)
    def _():
        _start_rows(lambda r: copy(i + 1, r, 1 - slot), ts)

    _wait_rows(copy(i, 0, slot), ts)
    o_ref[...] = _layer_norm(DN_ALPHA * x_ref[...] + ybuf[slot], g_ref[...], b_ref[...])


def _moe(x, grp, wr, br, wgu, wd, g, b, ts, tile):
    t = x.shape[0]
    pos, tile_group, n_tiles = _route_plan(grp, tile)
    any_spec = pl.BlockSpec(memory_space=pl.ANY)
    sort_rows = min(t, 1024)
    xs = pl.pallas_call(
        functools.partial(_moe_sort_kernel, rows=sort_rows),
        grid_spec=pltpu.PrefetchScalarGridSpec(
            num_scalar_prefetch=1, grid=(t // sort_rows,),
            in_specs=[any_spec, any_spec],
            out_specs=any_spec,
            scratch_shapes=[pltpu.SemaphoreType.DMA((2,))]),
        out_shape=jax.ShapeDtypeStruct((n_tiles * tile, D_MODEL), F32),
        input_output_aliases={2: 0},
        compiler_params=_params(("arbitrary",)),
        name="moe_sort",
    )(pos, x, jnp.zeros((n_tiles * tile, D_MODEL), F32))
    ys = pl.pallas_call(
        _moe_expert_kernel,
        grid_spec=pltpu.PrefetchScalarGridSpec(
            num_scalar_prefetch=1, grid=(n_tiles,),
            in_specs=[
                pl.BlockSpec((tile, D_MODEL), lambda i, tg: (i, 0)),
                pl.BlockSpec((D_MODEL, LANES), lambda i, tg: (0, 0)),
                pl.BlockSpec((1, LANES), lambda i, tg: (0, 0)),
                pl.BlockSpec((EXPERTS_PER_GROUP, D_MODEL, 2 * D_EXPERT), lambda i, tg: (tg[i], 0, 0)),
                pl.BlockSpec((1, EXPERTS_PER_GROUP * D_EXPERT, D_MODEL), lambda i, tg: (tg[i], 0, 0)),
            ],
            out_specs=pl.BlockSpec((tile, D_MODEL), lambda i, tg: (i, 0))),
        out_shape=jax.ShapeDtypeStruct((n_tiles * tile, D_MODEL), F32),
        compiler_params=_params(("arbitrary",)),
        name="moe_experts",
    )(tile_group, xs, wr, br, wgu, wd)
    return pl.pallas_call(
        _moe_unsort_kernel,
        grid_spec=pltpu.PrefetchScalarGridSpec(
            num_scalar_prefetch=1, grid=(t // ts,),
            in_specs=[pl.BlockSpec((ts, D_MODEL), lambda i, pos: (i, 0)), any_spec,
                      pl.BlockSpec((1, D_MODEL), lambda i, pos: (0, 0)),
                      pl.BlockSpec((1, D_MODEL), lambda i, pos: (0, 0))],
            out_specs=pl.BlockSpec((ts, D_MODEL), lambda i, pos: (i, 0)),
            scratch_shapes=[pltpu.VMEM((2, ts, D_MODEL), F32), pltpu.SemaphoreType.DMA((2,))]),
        out_shape=jax.ShapeDtypeStruct((t, D_MODEL), F32),
        compiler_params=_params(("arbitrary",)),
        name="moe_unsort",
    )(pos, x, ys, g, b)


def _rope_tables(pos):
    half = HEAD_DIM // 2
    freqs = ROPE_THETA ** (-jnp.arange(half, dtype=F32) / half)
    ang = pos.astype(F32)[:, None] * freqs[None, :]
    cos, sin = jnp.cos(ang), jnp.sin(ang)
    cosn = jnp.tile(cos, (1, LANES // half))
    sinn = jnp.tile(jnp.concatenate([-sin, sin], axis=1), (1, LANES // HEAD_DIM))
    return cosn, sinn, cos.T, sin.T


def _pad_axis(a, axis, size):
    pad = [(0, 0)] * a.ndim
    pad[axis] = (0, size - a.shape[axis])
    return jnp.pad(a, pad)


def kernel(x_prompt, x_sample, cache_k, cache_v, cache_k_idx, state_conv, state_h, w_in, conv_w, conv_b, w_rg_a, b_rg_a, w_rg_x, b_rg_x, lru_lambda, w_branch_attn, w_branch_rnn, w_out, ln1_g, ln1_b, w_router_group, b_router_group, w_router_expert, b_router_expert, w_gate_up, w_down, ln2_g, ln2_b):
    assert w_in.shape[0] == DEPTH == 1
    bp, sp, _ = x_prompt.shape
    bs, ss, _ = x_sample.shape
    past = cache_k.shape[2]
    ns_tok = bs * ss
    assert ns_tok == Q_BLOCK and sp % Q_BLOCK == 0 and sp % KEY_TILE == 0

    wq, wk, wv, wqi, wki, wwi, wxr, wgr, wga, wgb = jnp.split(w_in[0], SPLIT_POINTS, axis=1)
    wt = jnp.concatenate([wq.T, wqi.T, wv.T, _pad_axis(wwi.T, 0, 16)], axis=0).astype(BF16)
    wn = jnp.concatenate([wk, wv, _pad_axis(wki, 1, LANES)], axis=1).astype(BF16)
    row = lambda v: v.reshape(1, -1)
    rg = (wxr.astype(BF16), wgr.astype(BF16), conv_w[0], row(conv_b[0]),
          w_rg_a[0].astype(BF16), row(b_rg_a[0]), w_rg_x[0].astype(BF16), row(b_rg_x[0]),
          row(lru_lambda[0]))
    mg = (wga.astype(BF16), wgb.astype(BF16), w_branch_attn[0].astype(BF16),
          w_branch_rnn[0].astype(BF16), w_out[0].astype(BF16), row(ln1_g[0]), row(ln1_b[0]))
    wr = _pad_axis(jnp.concatenate([w_router_group[0], w_router_expert[0]], axis=1), 1, LANES).astype(BF16)
    br = _pad_axis(jnp.concatenate([b_router_group[0], b_router_expert[0]]).reshape(1, -1), 1, LANES)
    wd = w_down[0].reshape(N_GROUPS, EXPERTS_PER_GROUP * D_EXPERT, D_MODEL).astype(BF16)
    mo = (wr, br, w_gate_up[0].astype(BF16), wd, row(ln2_g[0]), row(ln2_b[0]))

    pos_p = jnp.arange(sp, dtype=I32)
    qT, qiT, vT, wiT, k_p, kb_p, v_p, ki_p, kib_p = _proj(x_prompt, wt, wn, _rope_tables(pos_p), 512, KEY_TILE)
    lim_p = jnp.minimum((pos_p // CHUNK + 1) * CHUNK, sp).reshape(sp // Q_BLOCK, 1, Q_BLOCK)
    nk_p = (jnp.max(lim_p, axis=(1, 2)) + KEY_TILE - 1) // KEY_TILE
    attn_p = _attn(nk_p.astype(I32), lim_p, qT, qiT, wiT, kb_p, vT, kib_p)
    y_p, co_p, hl_p = _rglru(x_prompt, jnp.zeros((bp, SUBLANES, D_RNN), F32), jnp.zeros((bp, 1, D_RNN), F32),
                             *rg, ts=256)
    x1_p, grp_p = _merge(x_prompt.reshape(bp * sp, D_MODEL), attn_p.reshape(bp * sp, ATT_WIDTH),
                         y_p.reshape(bp * sp, D_RNN), *mg, wr, br, tm=512)
    out_p = _moe(x1_p, grp_p, *mo, ts=256, tile=512)

    pos_s = past + (jnp.arange(ns_tok, dtype=I32) % ss)
    xs_flat = x_sample.reshape(1, ns_tok, D_MODEL)
    qT_s, qiT_s, _, wiT_s, k_s, kb_s, v_s, ki_s, kib_s = _proj(xs_flat, wt, wn, _rope_tables(pos_s), ns_tok, ns_tok)

    def per_batch_lanes(a):
        r = a.shape[1]
        return _pad_axis(a[0].reshape(r, bs, ss).transpose(1, 0, 2), 2, Q_BLOCK)

    l_all = past + ss
    lk = -(-l_all // KEY_TILE) * KEY_TILE
    k_all = jnp.concatenate([cache_k[0].reshape(bs, past, ATT_WIDTH).astype(BF16),
                             kb_s.reshape(bs, ss, ATT_WIDTH)], axis=1)
    v_all = jnp.concatenate([cache_v[0].reshape(bs, past, ATT_WIDTH),
                             v_s.reshape(bs, ss, ATT_WIDTH)], axis=1).astype(BF16)
    ki_all = jnp.concatenate([cache_k_idx[0].astype(BF16), kib_s.reshape(bs, ss, IDX_DIM)], axis=1)
    k_all, v_all, ki_all = (_pad_axis(a, 1, lk) for a in (k_all, v_all, ki_all))
    vT_all = v_all.reshape(bs, lk // KEY_TILE, KEY_TILE, ATT_WIDTH).transpose(0, 1, 3, 2)
    limit_s = min((past // CHUNK + 1) * CHUNK, l_all)
    assert (past + ss - 1) // CHUNK == past // CHUNK
    lim_s = jnp.where(jnp.arange(Q_BLOCK) < ss, limit_s, CHUNK).astype(I32).reshape(1, 1, Q_BLOCK)
    nk_s = jnp.full((1,), -(-limit_s // KEY_TILE), I32)
    attn_s = _attn(nk_s, lim_s, per_batch_lanes(qT_s), per_batch_lanes(qiT_s), per_batch_lanes(wiT_s),
                   k_all, vT_all, ki_all)[:, :ss]
    cs8 = jnp.pad(state_conv[0], ((0, 0), (SUBLANES - (CONV_WIDTH - 1), 0), (0, 0)))
    y_s, co_s, hl_s = _rglru(x_sample, cs8, state_h[0][:, None, :], *rg, ts=ss)
    x1_s, grp_s = _merge(x_sample.reshape(ns_tok, D_MODEL), attn_s.reshape(ns_tok, ATT_WIDTH),
                         y_s.reshape(ns_tok, D_RNN), *mg, wr, br, tm=ns_tok)
    out_s = _moe(x1_s, grp_s, *mo, ts=ns_tok, tile=ns_tok)

    keep = CONV_WIDTH - 1
    return (out_p.reshape(bp, sp, D_MODEL), out_s.reshape(bs, ss, D_MODEL),
            k_p.reshape(1, bp, sp, N_HEADS, HEAD_DIM), v_p.reshape(1, bp, sp, N_HEADS, HEAD_DIM),
            ki_p[None], co_p[:, SUBLANES - keep:][None], hl_p[:, 0][None],
            k_s.reshape(1, bs, ss, N_HEADS, HEAD_DIM), v_s.reshape(1, bs, ss, N_HEADS, HEAD_DIM),
            ki_s.reshape(1, bs, ss, IDX_DIM), co_s[:, SUBLANES - keep:][None], hl_s[:, 0][None])
```

```python
import functools

import jax
import jax.numpy as jnp
import numpy as np
from jax import lax
from jax.experimental import pallas as pl
from jax.experimental.pallas import tpu as pltpu

F32 = jnp.float32
BF16 = jnp.bfloat16
I32 = jnp.int32

D_MODEL = 1024
N_HEADS = 8
HEAD_DIM = 64
ATT_WIDTH = N_HEADS * HEAD_DIM
N_IDX_HEADS = 8
IDX_DIM = 64
MAX_TOPK = 256
CHUNK = 64
D_RNN = D_MODEL
N_RNN_BLOCKS = 8
RNN_BLOCK = D_RNN // N_RNN_BLOCKS
CONV_WIDTH = 4
LRU_C = 8.0
N_GROUPS = 4
EXPERTS_PER_GROUP = 4
N_EXPERTS = N_GROUPS * EXPERTS_PER_GROUP
D_EXPERT = 256
ROPE_THETA = 10000.0
LN_EPS = 1e-5
DEPTH = 1
DN_ALPHA = (2.0 * DEPTH) ** 0.25
SPLITS = (ATT_WIDTH, ATT_WIDTH, ATT_WIDTH, N_IDX_HEADS * IDX_DIM, IDX_DIM, N_IDX_HEADS,
          D_RNN, D_RNN, D_MODEL, D_MODEL)
SPLIT_POINTS = tuple(int(v) for v in np.cumsum(SPLITS)[:-1])

LANES = 128
SUBLANES = 8
VMEM_LIMIT = 56 * 1024 * 1024
Q_BLOCK = LANES
KEY_TILE = 256
INT_MIN = np.int32(-2 ** 31)
NEG_BIG = -1e30
KEY_LOWEST_FINITE = np.int32(-2 ** 31 + 2 ** 23)
LOG2_E = 1.4426950408889634
LATE_BITS = 4
WT_ROWS = 3 * ATT_WIDTH + 16
WN_COLS = 2 * ATT_WIDTH + LANES


def _params(semantics):
    return pltpu.CompilerParams(dimension_semantics=semantics, vmem_limit_bytes=VMEM_LIMIT)


def _dot(a, b):
    return jnp.dot(a, b, preferred_element_type=F32)


def _layer_norm(x, g, b):
    mu = jnp.mean(x, axis=-1, keepdims=True)
    xc = x - mu
    var = jnp.mean(xc * xc, axis=-1, keepdims=True)
    return xc * lax.rsqrt(var + LN_EPS) * g + b


def _proj_kernel(x_ref, wt_ref, wn_ref, cosn_ref, sinn_ref, cost_ref, sint_ref,
                 qT_ref, qiT_ref, vT_ref, wiT_ref, k_ref, kb_ref, v_ref, ki_ref, kib_ref):
    tm = x_ref.shape[1]
    kt_out = vT_ref.shape[3]
    xb = x_ref[0].astype(BF16)
    zt = lax.dot_general(wt_ref[...], xb, (((1,), (1,)), ((), ())), preferred_element_type=F32)
    zn = _dot(xb, wn_ref[...])
    cost = cost_ref[...]
    sint = sint_ref[...]
    half = HEAD_DIM // 2
    for base, ref, scale in ((0, qT_ref, HEAD_DIM ** -0.5 * LOG2_E), (ATT_WIDTH, qiT_ref, IDX_DIM ** -0.5)):
        for h in range(N_HEADS):
            r = h * HEAD_DIM
            x1 = zt[base + r:base + r + half]
            x2 = zt[base + r + half:base + r + HEAD_DIM]
            ref[0, r:r + half, :] = ((x1 * cost - x2 * sint) * scale).astype(BF16)
            ref[0, r + half:r + HEAD_DIM, :] = ((x1 * sint + x2 * cost) * scale).astype(BF16)
    for c in range(tm // kt_out):
        vT_ref[0, c] = zt[2 * ATT_WIDTH:3 * ATT_WIDTH, c * kt_out:(c + 1) * kt_out].astype(BF16)
    wiT_ref[0] = zt[3 * ATT_WIDTH:3 * ATT_WIDTH + N_IDX_HEADS] * (N_IDX_HEADS ** -0.5)

    cosn = cosn_ref[...]
    sinn = sinn_ref[...]
    lane = lax.broadcasted_iota(I32, (tm, LANES), 1)
    first_half = (lane & half) == 0

    def rope_n(z):
        partner = jnp.where(first_half, pltpu.roll(z, LANES - half, 1), pltpu.roll(z, half, 1))
        return z * cosn + partner * sinn

    for g in range(ATT_WIDTH // LANES):
        kg = rope_n(zn[:, g * LANES:(g + 1) * LANES])
        k_ref[0, :, g * LANES:(g + 1) * LANES] = kg
        kb_ref[0, :, g * LANES:(g + 1) * LANES] = kg.astype(BF16)
    v_ref[0] = zn[:, ATT_WIDTH:2 * ATT_WIDTH]
    kig = rope_n(zn[:, 2 * ATT_WIDTH:2 * ATT_WIDTH + LANES])
    ki_ref[0] = kig[:, :IDX_DIM]
    kib_ref[0] = kig[:, :IDX_DIM].astype(BF16)


def _proj(x, wt, wn, tables, tm, kt_out):
    b, s, _ = x.shape
    cosn, sinn, cost, sint = tables
    ns = s // tm
    out_shape = (
        jax.ShapeDtypeStruct((b, ATT_WIDTH, s), BF16),
        jax.ShapeDtypeStruct((b, ATT_WIDTH, s), BF16),
        jax.ShapeDtypeStruct((b, s // kt_out, ATT_WIDTH, kt_out), BF16),
        jax.ShapeDtypeStruct((b, N_IDX_HEADS, s), F32),
        jax.ShapeDtypeStruct((b, s, ATT_WIDTH), F32),
        jax.ShapeDtypeStruct((b, s, ATT_WIDTH), BF16),
        jax.ShapeDtypeStruct((b, s, ATT_WIDTH), F32),
        jax.ShapeDtypeStruct((b, s, IDX_DIM), F32),
        jax.ShapeDtypeStruct((b, s, IDX_DIM), BF16),
    )
    tok = lambda w: pl.BlockSpec((1, tm, w), lambda i, j: (i, j, 0))
    feat = lambda r: pl.BlockSpec((1, r, tm), lambda i, j: (i, 0, j))
    return pl.pallas_call(
        _proj_kernel,
        grid=(b, ns),
        in_specs=[
            tok(D_MODEL),
            pl.BlockSpec((WT_ROWS, D_MODEL), lambda i, j: (0, 0)),
            pl.BlockSpec((D_MODEL, WN_COLS), lambda i, j: (0, 0)),
            pl.BlockSpec((tm, LANES), lambda i, j: (j, 0)),
            pl.BlockSpec((tm, LANES), lambda i, j: (j, 0)),
            pl.BlockSpec((HEAD_DIM // 2, tm), lambda i, j: (0, j)),
            pl.BlockSpec((HEAD_DIM // 2, tm), lambda i, j: (0, j)),
        ],
        out_specs=(
            feat(ATT_WIDTH), feat(ATT_WIDTH),
            pl.BlockSpec((1, tm // kt_out, ATT_WIDTH, kt_out), lambda i, j: (i, j, 0, 0)),
            feat(N_IDX_HEADS),
            tok(ATT_WIDTH), tok(ATT_WIDTH), tok(ATT_WIDTH), tok(IDX_DIM), tok(IDX_DIM),
        ),
        out_shape=out_shape,
        compiler_params=_params(("parallel", "parallel")),
        name="proj",
    )(x, wt, wn, cosn, sinn, cost, sint)


def _order_key_to_f32(key):
    return lax.bitcast_convert_type(jnp.where(key < 0, key ^ 0x7FFFFFFF, key), F32)


def _attn_kernel(nk_ref, lim_ref, qT_ref, qiT_ref, wiT_ref, k_ref, vT_ref, ki_ref, o_ref,
                 score_scr, bias_scr, logit_scr, acc_scr):
    nk = nk_ref[pl.program_id(1)]
    lim = lim_ref[0]
    w = wiT_ref[0]
    qi = qiT_ref[0]
    q = qT_ref[0]
    groups = KEY_TILE // SUBLANES

    def rows(kt):
        return pl.ds(pl.multiple_of(kt * KEY_TILE, KEY_TILE), KEY_TILE)

    def fold(x, op):
        return op(x.reshape(groups, SUBLANES, Q_BLOCK), axis=0)

    def tile_loop(body, init):
        c = lax.fori_loop(0, nk // 2, lambda i, c: body(2 * i + 1, body(2 * i, c)), init)
        return lax.cond(nk % 2 == 1, lambda c: body(nk - 1, c), lambda c: c, c)

    qi_pairs = [jnp.concatenate([qi[(2 * p) * IDX_DIM:(2 * p + 1) * IDX_DIM],
                                 qi[(2 * p + 1) * IDX_DIM:(2 * p + 2) * IDX_DIM]], axis=1)
                for p in range(N_IDX_HEADS // 2)]
    row_iota = lax.broadcasted_iota(I32, (KEY_TILE, Q_BLOCK), 0)

    def score_tile(kt, carry):
        ki_t = ki_ref[0, rows(kt), :]
        acc = jnp.zeros((KEY_TILE, Q_BLOCK), F32)
        for p in range(N_IDX_HEADS // 2):
            d = _dot(ki_t, qi_pairs[p])
            acc = acc + jnp.maximum(d[:, :Q_BLOCK], 0.0) * w[2 * p:2 * p + 1]
            acc = acc + jnp.maximum(d[:, Q_BLOCK:], 0.0) * w[2 * p + 1:2 * p + 2]
        score_scr[rows(kt), :] = jnp.where(row_iota + kt * KEY_TILE < lim, acc, -jnp.inf)
        return carry

    tile_loop(score_tile, 0)

    def count(pred):
        def body(kt, c):
            return c + fold(jnp.where(pred(score_scr[rows(kt), :]), 1, 0), jnp.sum)

        return jnp.sum(tile_loop(body, jnp.zeros((SUBLANES, Q_BLOCK), I32)), axis=0, keepdims=True)

    def bisect_for(n_tiles):
        def bisect(i, carry):
            t, cnt_t = carry
            cand = t + lax.shift_left(jnp.int32(1), 31 - i)
            cand_f = _order_key_to_f32(cand)
            c = jnp.zeros((SUBLANES, Q_BLOCK), I32)
            for kt in range(n_tiles):
                c = c + fold(jnp.where(score_scr[kt * KEY_TILE:(kt + 1) * KEY_TILE, :] >= cand_f, 1, 0), jnp.sum)
            cnt = jnp.sum(c, axis=0, keepdims=True)
            ok = cnt >= MAX_TOPK
            return jnp.where(ok, cand, t), jnp.where(ok, cnt, cnt_t)

        def run():
            t0 = jnp.full((1, Q_BLOCK), INT_MIN, I32)
            c0 = jnp.full((1, Q_BLOCK), n_tiles * KEY_TILE, I32)
            t, c = lax.fori_loop(0, 32 - LATE_BITS, bisect, (t0, c0))
            settled = jnp.logical_or(c == MAX_TOPK, lim <= MAX_TOPK)
            all_settled = jnp.min(jnp.where(settled, 1.0, 0.0)) > 0.0
            return lax.cond(all_settled, lambda tc: tc,
                            lambda tc: lax.fori_loop(32 - LATE_BITS, 32, bisect, tc), (t, c))
        return run

    max_tiles = score_scr.shape[0] // KEY_TILE
    thr, cnt_thr = lax.switch(nk - 1, [bisect_for(n) for n in range(1, max_tiles + 1)])
    found = thr >= KEY_LOWEST_FINITE
    thr_f = jnp.where(found, _order_key_to_f32(thr), jnp.finfo(F32).min)

    def bias_tile(kt, carry):
        bias_scr[rows(kt), :] = jnp.where(score_scr[rows(kt), :] >= thr_f, 0.0, NEG_BIG)
        return carry

    tile_loop(bias_tile, 0)

    tie = jnp.logical_and(cnt_thr > MAX_TOPK, found)
    any_tie = jnp.max(jnp.where(tie, 1.0, 0.0)) > 0.0

    @pl.when(any_tie)
    def _():
        need = (MAX_TOPK - count(lambda s: s > thr_f)).astype(F32)
        tri = (lax.broadcasted_iota(I32, (KEY_TILE, KEY_TILE), 1)
               <= lax.broadcasted_iota(I32, (KEY_TILE, KEY_TILE), 0))
        tri = jnp.where(tri, 1.0, 0.0).astype(BF16)

        def tie_tile(kt, before):
            s = score_scr[rows(kt), :]
            eq = s == thr_f
            rank = _dot(tri, jnp.where(eq, 1.0, 0.0).astype(BF16)) + before
            sel = jnp.logical_or(s > thr_f, jnp.logical_and(eq, rank <= need))
            bias_scr[rows(kt), :] = jnp.where(sel, 0.0, NEG_BIG)
            return rank[KEY_TILE - 1:KEY_TILE, :]

        lax.fori_loop(0, nk, tie_tile, jnp.zeros((1, Q_BLOCK), F32))

    zeros_half = jnp.zeros((HEAD_DIM, Q_BLOCK), BF16)
    rhs = [jnp.concatenate(
        [jnp.concatenate([q[(2 * p) * HEAD_DIM:(2 * p + 1) * HEAD_DIM], zeros_half], axis=0),
         jnp.concatenate([zeros_half, q[(2 * p + 1) * HEAD_DIM:(2 * p + 2) * HEAD_DIM]], axis=0)], axis=1)
        for p in range(N_HEADS // 2)]

    def logits_tile(kt, m_parts):
        bias = bias_scr[rows(kt), :]
        out = []
        for p in range(N_HEADS // 2):
            s2 = _dot(k_ref[0, rows(kt), p * LANES:(p + 1) * LANES], rhs[p])
            for e in range(2):
                h = 2 * p + e
                s = s2[:, e * Q_BLOCK:(e + 1) * Q_BLOCK] + bias
                logit_scr[h, rows(kt), :] = s
                out.append(jnp.maximum(m_parts[h], fold(s, jnp.max)))
        return tuple(out)

    m_parts = tile_loop(logits_tile, tuple(jnp.full((SUBLANES, Q_BLOCK), NEG_BIG, F32) for _ in range(N_HEADS)))
    m = [jnp.max(mp, axis=0, keepdims=True) for mp in m_parts]
    acc_scr[...] = jnp.zeros_like(acc_scr)

    def pv_tile(kt, l_parts):
        out = []
        for h in range(N_HEADS):
            p = jnp.exp2(logit_scr[h, rows(kt), :] - m[h])
            out.append(l_parts[h] + fold(p, jnp.sum))
            v_t = vT_ref[0, kt, h * HEAD_DIM:(h + 1) * HEAD_DIM, :]
            acc_scr[h * HEAD_DIM:(h + 1) * HEAD_DIM, :] += _dot(v_t, p.astype(BF16))
        return tuple(out)

    l_parts = tile_loop(pv_tile, tuple(jnp.zeros((SUBLANES, Q_BLOCK), F32) for _ in range(N_HEADS)))
    outs = [acc_scr[h * HEAD_DIM:(h + 1) * HEAD_DIM, :] / jnp.sum(l_parts[h], axis=0, keepdims=True)
            for h in range(N_HEADS)]
    o_ref[0] = jnp.concatenate(outs, axis=0).T.astype(BF16)


def _attn(nk, limits, qT, qiT, wiT, kb, vT, kib):
    b, _, sq = qT.shape
    nq = sq // Q_BLOCK
    lk = kb.shape[1]
    grid_spec = pltpu.PrefetchScalarGridSpec(
        num_scalar_prefetch=1,
        grid=(b, nq),
        in_specs=[
            pl.BlockSpec((1, 1, Q_BLOCK), lambda i, j, nk: (j, 0, 0)),
            pl.BlockSpec((1, ATT_WIDTH, Q_BLOCK), lambda i, j, nk: (i, 0, j)),
            pl.BlockSpec((1, ATT_WIDTH, Q_BLOCK), lambda i, j, nk: (i, 0, j)),
            pl.BlockSpec((1, N_IDX_HEADS, Q_BLOCK), lambda i, j, nk: (i, 0, j)),
            pl.BlockSpec((1, lk, ATT_WIDTH), lambda i, j, nk: (i, 0, 0)),
            pl.BlockSpec((1, lk // KEY_TILE, ATT_WIDTH, KEY_TILE), lambda i, j, nk: (i, 0, 0, 0)),
            pl.BlockSpec((1, lk, IDX_DIM), lambda i, j, nk: (i, 0, 0)),
        ],
        out_specs=pl.BlockSpec((1, Q_BLOCK, ATT_WIDTH), lambda i, j, nk: (i, j, 0)),
        scratch_shapes=[pltpu.VMEM((lk, Q_BLOCK), F32), pltpu.VMEM((lk, Q_BLOCK), F32),
                        pltpu.VMEM((N_HEADS, lk, Q_BLOCK), F32), pltpu.VMEM((ATT_WIDTH, Q_BLOCK), F32)],
    )
    return pl.pallas_call(
        _attn_kernel,
        grid_spec=grid_spec,
        out_shape=jax.ShapeDtypeStruct((b, sq, ATT_WIDTH), BF16),
        compiler_params=_params(("parallel", "arbitrary")),
        name="attn",
    )(nk, limits, qT, qiT, wiT, kb, vT, kib)


def _rglru_kernel(x_ref, cs_ref, h0_ref, wxr_ref, wgr_ref, cw_ref, cb_ref, wa_ref, ba_ref,
                  wx_ref, bx_ref, lam_ref, y_ref, co_ref, hl_ref,
                  xbuf, a_scr, u_scr, h_scr, hc_scr):
    ts = x_ref.shape[1]

    @pl.when(pl.program_id(1) == 0)
    def _():
        xbuf[0:SUBLANES, :] = cs_ref[0]
        hc_scr[...] = h0_ref[0]

    xb = x_ref[0].astype(BF16)
    xr = _dot(xb, wxr_ref[...])
    gr = _dot(xb, wgr_ref[...])
    xbuf[SUBLANES:SUBLANES + ts, :] = xr
    cw = cw_ref[...]
    xc = cb_ref[...] + xbuf[5:5 + ts, :] * cw[0:1]
    xc = xc + xbuf[6:6 + ts, :] * cw[1:2]
    xc = xc + xbuf[7:7 + ts, :] * cw[2:3]
    xc = xc + xr * cw[3:4]
    tail = xbuf[ts:ts + SUBLANES, :]
    xbuf[0:SUBLANES, :] = tail
    co_ref[0] = tail

    xcb = xc.astype(BF16)

    def gate(w_ref, b_ref):
        parts = [_dot(xcb[:, n * RNN_BLOCK:(n + 1) * RNN_BLOCK], w_ref[n]) for n in range(N_RNN_BLOCKS)]
        return jax.nn.sigmoid(jnp.concatenate(parts, axis=1) + b_ref[...])

    r = gate(wa_ref, ba_ref)
    i = gate(wx_ref, bx_ref)
    nl = -lam_ref[...]
    softplus = jnp.maximum(nl, 0.0) + jnp.log1p(jnp.exp(-jnp.abs(nl)))
    log_a = (-LRU_C * r) * softplus
    a_scr[...] = jnp.exp(log_a)
    th = jnp.tanh(log_a)
    u_scr[...] = jnp.exp2(0.5 * jnp.log2(-2.0 * th / (1.0 - th))) * (i * xc)

    row = lax.broadcasted_iota(I32, (SUBLANES, D_RNN), 0)

    def group(g, h_prev):
        rs = pl.ds(pl.multiple_of(g * SUBLANES, SUBLANES), SUBLANES)
        a = a_scr[rs, :]
        b = u_scr[rs, :]
        for d in (1, 2, 4):
            a_sh = jnp.where(row >= d, pltpu.roll(a, d, 0), 1.0)
            b_sh = jnp.where(row >= d, pltpu.roll(b, d, 0), 0.0)
            b = a * b_sh + b
            a = a * a_sh
        h = b + a * h_prev
        h_scr[rs, :] = h
        return h[SUBLANES - 1:SUBLANES, :]

    h_last = lax.fori_loop(0, ts // SUBLANES, group, hc_scr[...])
    hc_scr[...] = h_last
    hl_ref[0] = h_last
    y_ref[0] = (h_scr[...] * jax.nn.gelu(gr)).astype(BF16)


def _rglru(x, conv_state8, h0, wxr, wgr, conv_w, conv_b, wa, ba, wx, bx, lam, ts):
    b, s, _ = x.shape
    const2 = lambda shape: pl.BlockSpec(shape, lambda i, j: (0, 0))
    const3 = lambda shape: pl.BlockSpec(shape, lambda i, j: (0, 0, 0))
    per_b = lambda r: pl.BlockSpec((1, r, D_RNN), lambda i, j: (i, 0, 0))
    return pl.pallas_call(
        _rglru_kernel,
        grid=(b, s // ts),
        in_specs=[
            pl.BlockSpec((1, ts, D_MODEL), lambda i, j: (i, j, 0)),
            per_b(SUBLANES), per_b(1),
            const2((D_MODEL, D_RNN)), const2((D_MODEL, D_RNN)),
            const2((CONV_WIDTH, D_RNN)), const2((1, D_RNN)),
            const3((N_RNN_BLOCKS, RNN_BLOCK, RNN_BLOCK)), const2((1, D_RNN)),
            const3((N_RNN_BLOCKS, RNN_BLOCK, RNN_BLOCK)), const2((1, D_RNN)),
            const2((1, D_RNN)),
        ],
        out_specs=(
            pl.BlockSpec((1, ts, D_RNN), lambda i, j: (i, j, 0)),
            per_b(SUBLANES), per_b(1),
        ),
        out_shape=(
            jax.ShapeDtypeStruct((b, s, D_RNN), BF16),
            jax.ShapeDtypeStruct((b, SUBLANES, D_RNN), F32),
            jax.ShapeDtypeStruct((b, 1, D_RNN), F32),
        ),
        scratch_shapes=[
            pltpu.VMEM((ts + SUBLANES, D_RNN), F32),
            pltpu.VMEM((ts, D_RNN), F32), pltpu.VMEM((ts, D_RNN), F32), pltpu.VMEM((ts, D_RNN), F32),
            pltpu.VMEM((1, D_RNN), F32),
        ],
        compiler_params=_params(("parallel", "arbitrary")),
        name="rglru",
    )(x, conv_state8, h0, wxr, wgr, conv_w, conv_b, wa, ba, wx, bx, lam)


def _merge_kernel(x_ref, attn_ref, y_ref, wga_ref, wgb_ref, wba_ref, wbr_ref, wout_ref, g_ref, b_ref,
                  wgt_ref, bgt_ref, o_ref, grp_ref):
    x = x_ref[...]
    xb = x.astype(BF16)
    ga = _dot(xb, wga_ref[...])
    gb = _dot(xb, wgb_ref[...])
    merged = (jax.nn.sigmoid(ga) * _dot(attn_ref[...], wba_ref[...])
              + jax.nn.sigmoid(gb) * _dot(y_ref[...], wbr_ref[...]))
    mix = _dot(merged.astype(BF16), wout_ref[...])
    x1 = _layer_norm(DN_ALPHA * x + mix, g_ref[...], b_ref[...])
    o_ref[...] = x1
    logits = lax.dot_general(wgt_ref[...], x1.astype(BF16), (((1,), (1,)), ((), ())),
                             preferred_element_type=F32) + bgt_ref[...]
    best = logits[0:1]
    g_sel = jnp.zeros(best.shape, I32)
    for k in range(1, N_GROUPS):
        g_sel = jnp.where(logits[k:k + 1] > best, k, g_sel)
        best = jnp.maximum(best, logits[k:k + 1])
    grp_ref[0] = g_sel


def _merge(x, attn, y, wga, wgb, wba, wbr, wout, g, b, wgt, bgt, tm):
    t = x.shape[0]
    tok = lambda w: pl.BlockSpec((tm, w), lambda i: (i, 0))
    const = lambda shape: pl.BlockSpec(shape, lambda i: (0, 0))
    x1, grp = pl.pallas_call(
        _merge_kernel,
        grid=(t // tm,),
        in_specs=[tok(D_MODEL), tok(ATT_WIDTH), tok(D_RNN),
                  const((D_MODEL, D_MODEL)), const((D_MODEL, D_MODEL)),
                  const((ATT_WIDTH, D_MODEL)), const((D_RNN, D_MODEL)), const((D_MODEL, D_MODEL)),
                  const((1, D_MODEL)), const((1, D_MODEL)), const(wgt.shape), const(bgt.shape)],
        out_specs=(tok(D_MODEL), pl.BlockSpec((1, 1, tm), lambda i: (i, 0, 0))),
        out_shape=(jax.ShapeDtypeStruct((t, D_MODEL), F32), jax.ShapeDtypeStruct((t // tm, 1, tm), I32)),
        compiler_params=_params(("parallel",)),
        name="merge",
    )(x, attn, y, wga, wgb, wba, wbr, wout, g, b, wgt, bgt)
    return x1, grp.reshape(t)


def _route_plan(g, tile):
    t = g.shape[0]
    onehot = (g[:, None] == jnp.arange(N_GROUPS, dtype=I32)[None, :]).astype(I32)
    csum = jnp.cumsum(onehot, axis=0)
    padded = (csum[-1] + tile - 1) // tile * tile
    ends = jnp.cumsum(padded)
    pos = jnp.sum((ends - padded)[None, :] * onehot, axis=1) + jnp.sum(csum * onehot, axis=1) - 1
    n_tiles = t // tile + N_GROUPS
    tile_start = jnp.arange(n_tiles, dtype=I32) * tile
    tile_group = jnp.minimum(jnp.sum((tile_start[:, None] >= ends[None, :]).astype(I32), axis=1), N_GROUPS - 1)
    return pos.astype(I32), tile_group.astype(I32), n_tiles


ROW_DMA_UNROLL = 8


def _start_rows(copy, n):
    for r in range(n):
        copy(r).start(priority=r % 2)


def _wait_rows(copy, n):
    def batch(i, c):
        for _ in range(ROW_DMA_UNROLL):
            copy.wait()
        return c

    lax.fori_loop(0, n // ROW_DMA_UNROLL, batch, 0)


def _moe_sort_kernel(pos_ref, x_ref, init_ref, xs_ref, sbuf, sems):
    del init_ref
    ts = x_ref.shape[0]
    i = pl.program_id(0)
    base = i * ts
    slot = i % 2
    sbuf[slot] = x_ref[...]

    def copy(r, s):
        return pltpu.make_async_copy(sbuf.at[s, pl.ds(r, 1)],
                                     xs_ref.at[pl.ds(pos_ref[base + r], 1)], sems.at[s])

    _start_rows(lambda r: copy(r, slot), ts)

    @pl.when(i > 0)
    def _():
        _wait_rows(copy(0, 1 - slot), ts)

    @pl.when(i == pl.num_programs(0) - 1)
    def _():
        _wait_rows(copy(0, slot), ts)


def _moe_expert_kernel(tg_ref, xs_ref, wr_ref, br_ref, wgu_ref, wd_ref, ys_ref):
    g = tg_ref[pl.program_id(0)]
    xb = xs_ref[...].astype(BF16)
    tm = xb.shape[0]
    lane = lax.broadcasted_iota(I32, (tm, LANES), 1)
    logits = _dot(xb, wr_ref[...]) + br_ref[...]
    neg_inf = -jnp.inf
    eg = jnp.exp(jnp.where(lane < N_GROUPS, logits, neg_inf)
                 - jnp.max(jnp.where(lane < N_GROUPS, logits, neg_inf), axis=1, keepdims=True))
    p_group = (jnp.sum(jnp.where(lane == g, eg, 0.0), axis=1, keepdims=True)
               / jnp.sum(eg, axis=1, keepdims=True))
    lo = N_GROUPS + EXPERTS_PER_GROUP * g
    el = jnp.where(jnp.logical_and(lane >= lo, lane < lo + EXPERTS_PER_GROUP), logits, neg_inf)
    v1 = jnp.max(el, axis=1, keepdims=True)
    i1 = jnp.min(jnp.where(el == v1, lane, LANES), axis=1, keepdims=True)
    el2 = jnp.where(lane == i1, neg_inf, el)
    v2 = jnp.max(el2, axis=1, keepdims=True)
    i2 = jnp.min(jnp.where(el2 == v2, lane, LANES), axis=1, keepdims=True)
    e2 = jnp.exp(v2 - v1)
    p1 = p_group / (1.0 + e2)
    p2 = e2 * p1
    acts = []
    for j in range(EXPERTS_PER_GROUP):
        gate = jnp.where(i1 == lo + j, p1, 0.0) + jnp.where(i2 == lo + j, p2, 0.0)
        hj = _dot(xb, wgu_ref[j])
        acts.append((gate * (jax.nn.silu(hj[:, :D_EXPERT]) * hj[:, D_EXPERT:])).astype(BF16))
    ys_ref[...] = _dot(jnp.concatenate(acts, axis=1), wd_ref[0])


def _moe_unsort_kernel(pos_ref, x_ref, ys_ref, g_ref, b_ref, o_ref, ybuf, sems):
    ts = x_ref.shape[0]
    i = pl.program_id(0)
    slot = i % 2

    def copy(step, r, s):
        return pltpu.make_async_copy(ys_ref.at[pl.ds(pos_ref[step * ts + r], 1)],
                                     ybuf.at[s, pl.ds(r, 1)], sems.at[s])

    @pl.when(i == 0)
    def _():
        _start_rows(lambda r: copy(0, r, 0), ts)

    @pl.when(i < pl.num_programs(0) - 1)
    def _():
        _start_rows(lambda r: copy(i + 1, r, 1 - slot), ts)

    _wait_rows(copy(i, 0, slot), ts)
    o_ref[...] = _layer_norm(DN_ALPHA * x_ref[...] + ybuf[slot], g_ref[...], b_ref[...])


def _moe(x, grp, wr, br, wgu, wd, g, b, ts, tile):
    t = x.shape[0]
    pos, tile_group, n_tiles = _route_plan(grp, tile)
    any_spec = pl.BlockSpec(memory_space=pl.ANY)
    xs = pl.pallas_call(
        _moe_sort_kernel,
        grid_spec=pltpu.PrefetchScalarGridSpec(
            num_scalar_prefetch=1, grid=(t // ts,),
            in_specs=[pl.BlockSpec((ts, D_MODEL), lambda i, pos: (i, 0)), any_spec],
            out_specs=any_spec,
            scratch_shapes=[pltpu.VMEM((2, ts, D_MODEL), F32), pltpu.SemaphoreType.DMA((2,))]),
        out_shape=jax.ShapeDtypeStruct((n_tiles * tile, D_MODEL), F32),
        input_output_aliases={2: 0},
        compiler_params=_params(("arbitrary",)),
        name="moe_sort",
    )(pos, x, jnp.zeros((n_tiles * tile, D_MODEL), F32))
    ys = pl.pallas_call(
        _moe_expert_kernel,
        grid_spec=pltpu.PrefetchScalarGridSpec(
            num_scalar_prefetch=1, grid=(n_tiles,),
            in_specs=[
                pl.BlockSpec((tile, D_MODEL), lambda i, tg: (i, 0)),
                pl.BlockSpec((D_MODEL, LANES), lambda i, tg: (0, 0)),
                pl.BlockSpec((1, LANES), lambda i, tg: (0, 0)),
                pl.BlockSpec((EXPERTS_PER_GROUP, D_MODEL, 2 * D_EXPERT), lambda i, tg: (tg[i], 0, 0)),
                pl.BlockSpec((1, EXPERTS_PER_GROUP * D_EXPERT, D_MODEL), lambda i, tg: (tg[i], 0, 0)),
            ],
            out_specs=pl.BlockSpec((tile, D_MODEL), lambda i, tg: (i, 0))),
        out_shape=jax.ShapeDtypeStruct((n_tiles * tile, D_MODEL), F32),
        compiler_params=_params(("arbitrary",)),
        name="moe_experts",
    )(tile_group, xs, wr, br, wgu, wd)
    return pl.pallas_call(
        _moe_unsort_kernel,
        grid_spec=pltpu.PrefetchScalarGridSpec(
            num_scalar_prefetch=1, grid=(t // ts,),
            in_specs=[pl.BlockSpec((ts, D_MODEL), lambda i, pos: (i, 0)), any_spec,
                      pl.BlockSpec((1, D_MODEL), lambda i, pos: (0, 0)),
                      pl.BlockSpec((1, D_MODEL), lambda i, pos: (0, 0))],
            out_specs=pl.BlockSpec((ts, D_MODEL), lambda i, pos: (i, 0)),
            scratch_shapes=[pltpu.VMEM((2, ts, D_MODEL), F32), pltpu.SemaphoreType.DMA((2,))]),
        out_shape=jax.ShapeDtypeStruct((t, D_MODEL), F32),
        compiler_params=_params(("arbitrary",)),
        name="moe_unsort",
    )(pos, x, ys, g, b)


def _rope_tables(pos):
    half = HEAD_DIM // 2
    freqs = ROPE_THETA ** (-jnp.arange(half, dtype=F32) / half)
    ang = pos.astype(F32)[:, None] * freqs[None, :]
    cos, sin = jnp.cos(ang), jnp.sin(ang)
    cosn = jnp.tile(cos, (1, LANES // half))
    sinn = jnp.tile(jnp.concatenate([-sin, sin], axis=1), (1, LANES // HEAD_DIM))
    return cosn, sinn, cos.T, sin.T


def _pad_axis(a, axis, size):
    pad = [(0, 0)] * a.ndim
    pad[axis] = (0, size - a.shape[axis])
    return jnp.pad(a, pad)


def kernel(x_prompt, x_sample, cache_k, cache_v, cache_k_idx, state_conv, state_h, w_in, conv_w, conv_b, w_rg_a, b_rg_a, w_rg_x, b_rg_x, lru_lambda, w_branch_attn, w_branch_rnn, w_out, ln1_g, ln1_b, w_router_group, b_router_group, w_router_expert, b_router_expert, w_gate_up, w_down, ln2_g, ln2_b):
    assert w_in.shape[0] == DEPTH == 1
    bp, sp, _ = x_prompt.shape
    bs, ss, _ = x_sample.shape
    past = cache_k.shape[2]
    ns_tok = bs * ss
    assert ns_tok == Q_BLOCK and sp % Q_BLOCK == 0 and sp % KEY_TILE == 0

    wq, wk, wv, wqi, wki, wwi, wxr, wgr, wga, wgb = jnp.split(w_in[0], SPLIT_POINTS, axis=1)
    wt = jnp.concatenate([wq.T, wqi.T, wv.T, _pad_axis(wwi.T, 0, 16)], axis=0).astype(BF16)
    wn = jnp.concatenate([wk, wv, _pad_axis(wki, 1, LANES)], axis=1).astype(BF16)
    row = lambda v: v.reshape(1, -1)
    rg = (wxr.astype(BF16), wgr.astype(BF16), conv_w[0], row(conv_b[0]),
          w_rg_a[0].astype(BF16), row(b_rg_a[0]), w_rg_x[0].astype(BF16), row(b_rg_x[0]),
          row(lru_lambda[0]))
    mg = (wga.astype(BF16), wgb.astype(BF16), w_branch_attn[0].astype(BF16),
          w_branch_rnn[0].astype(BF16), w_out[0].astype(BF16), row(ln1_g[0]), row(ln1_b[0]))
    wr = _pad_axis(jnp.concatenate([w_router_group[0], w_router_expert[0]], axis=1), 1, LANES).astype(BF16)
    br = _pad_axis(jnp.concatenate([b_router_group[0], b_router_expert[0]]).reshape(1, -1), 1, LANES)
    wgt = _pad_axis(w_router_group[0].T, 0, 16).astype(BF16)
    bgt = _pad_axis(b_router_group[0].reshape(-1, 1), 0, 16)
    wd = w_down[0].reshape(N_GROUPS, EXPERTS_PER_GROUP * D_EXPERT, D_MODEL).astype(BF16)
    mo = (wr, br, w_gate_up[0].astype(BF16), wd, row(ln2_g[0]), row(ln2_b[0]))

    pos_p = jnp.arange(sp, dtype=I32)
    qT, qiT, vT, wiT, k_p, kb_p, v_p, ki_p, kib_p = _proj(x_prompt, wt, wn, _rope_tables(pos_p), 512, KEY_TILE)
    lim_p = jnp.minimum((pos_p // CHUNK + 1) * CHUNK, sp).reshape(sp // Q_BLOCK, 1, Q_BLOCK)
    nk_p = (jnp.max(lim_p, axis=(1, 2)) + KEY_TILE - 1) // KEY_TILE
    attn_p = _attn(nk_p.astype(I32), lim_p, qT, qiT, wiT, kb_p, vT, kib_p)
    y_p, co_p, hl_p = _rglru(x_prompt, jnp.zeros((bp, SUBLANES, D_RNN), F32), jnp.zeros((bp, 1, D_RNN), F32),
                             *rg, ts=256)
    x1_p, grp_p = _merge(x_prompt.reshape(bp * sp, D_MODEL), attn_p.reshape(bp * sp, ATT_WIDTH),
                         y_p.reshape(bp * sp, D_RNN), *mg, wgt, bgt, tm=512)
    out_p = _moe(x1_p, grp_p, *mo, ts=256, tile=512)

    pos_s = past + (jnp.arange(ns_tok, dtype=I32) % ss)
    xs_flat = x_sample.reshape(1, ns_tok, D_MODEL)
    qT_s, qiT_s, _, wiT_s, k_s, kb_s, v_s, ki_s, kib_s = _proj(xs_flat, wt, wn, _rope_tables(pos_s), ns_tok, ns_tok)

    def per_batch_lanes(a):
        r = a.shape[1]
        return _pad_axis(a[0].reshape(r, bs, ss).transpose(1, 0, 2), 2, Q_BLOCK)

    l_all = past + ss
    lk = -(-l_all // KEY_TILE) * KEY_TILE
    k_all = jnp.concatenate([cache_k[0].reshape(bs, past, ATT_WIDTH).astype(BF16),
                             kb_s.reshape(bs, ss, ATT_WIDTH)], axis=1)
    v_all = jnp.concatenate([cache_v[0].reshape(bs, past, ATT_WIDTH),
                             v_s.reshape(bs, ss, ATT_WIDTH)], axis=1).astype(BF16)
    ki_all = jnp.concatenate([cache_k_idx[0].astype(BF16), kib_s.reshape(bs, ss, IDX_DIM)], axis=1)
    k_all, v_all, ki_all = (_pad_axis(a, 1, lk) for a in (k_all, v_all, ki_all))
    vT_all = v_all.reshape(bs, lk // KEY_TILE, KEY_TILE, ATT_WIDTH).transpose(0, 1, 3, 2)
    limit_s = min((past // CHUNK + 1) * CHUNK, l_all)
    assert (past + ss - 1) // CHUNK == past // CHUNK
    lim_s = jnp.where(jnp.arange(Q_BLOCK) < ss, limit_s, CHUNK).astype(I32).reshape(1, 1, Q_BLOCK)
    nk_s = jnp.full((1,), -(-limit_s // KEY_TILE), I32)
    attn_s = _attn(nk_s, lim_s, per_batch_lanes(qT_s), per_batch_lanes(qiT_s), per_batch_lanes(wiT_s),
                   k_all, vT_all, ki_all)[:, :ss]
    cs8 = jnp.pad(state_conv[0], ((0, 0), (SUBLANES - (CONV_WIDTH - 1), 0), (0, 0)))
    y_s, co_s, hl_s = _rglru(x_sample, cs8, state_h[0][:, None, :], *rg, ts=ss)
    x1_s, grp_s = _merge(x_sample.reshape(ns_tok, D_MODEL), attn_s.reshape(ns_tok, ATT_WIDTH),
                         y_s.reshape(ns_tok, D_RNN), *mg, wgt, bgt, tm=ns_tok)
    out_s = _moe(x1_s, grp_s, *mo, ts=ns_tok, tile=ns_tok)

    keep = CONV_WIDTH - 1
    return (out_p.reshape(bp, sp, D_MODEL), out_s.reshape(bs, ss, D_MODEL),
            k_p.reshape(1, bp, sp, N_HEADS, HEAD_DIM), v_p.reshape(1, bp, sp, N_HEADS, HEAD_DIM),
            ki_p[None], co_p[:, SUBLANES - keep:][None], hl_p[:, 0][None],
            k_s.reshape(1, bs, ss, N_HEADS, HEAD_DIM), v_s.reshape(1, bs, ss, N_HEADS, HEAD_DIM),
            ki_s.reshape(1, bs, ss, IDX_DIM), co_s[:, SUBLANES - keep:][None], hl_s[:, 0][None])
```

```python
import functools

import jax
import jax.numpy as jnp
import numpy as np
from jax import lax
from jax.experimental import pallas as pl
from jax.experimental.pallas import tpu as pltpu

F32 = jnp.float32
BF16 = jnp.bfloat16
I32 = jnp.int32

D_MODEL = 1024
N_HEADS = 8
HEAD_DIM = 64
ATT_WIDTH = N_HEADS * HEAD_DIM
N_IDX_HEADS = 8
IDX_DIM = 64
MAX_TOPK = 256
CHUNK = 64
D_RNN = D_MODEL
N_RNN_BLOCKS = 8
RNN_BLOCK = D_RNN // N_RNN_BLOCKS
CONV_WIDTH = 4
LRU_C = 8.0
N_GROUPS = 4
EXPERTS_PER_GROUP = 4
N_EXPERTS = N_GROUPS * EXPERTS_PER_GROUP
D_EXPERT = 256
ROPE_THETA = 10000.0
LN_EPS = 1e-5
DEPTH = 1
DN_ALPHA = (2.0 * DEPTH) ** 0.25
SPLITS = (ATT_WIDTH, ATT_WIDTH, ATT_WIDTH, N_IDX_HEADS * IDX_DIM, IDX_DIM, N_IDX_HEADS,
          D_RNN, D_RNN, D_MODEL, D_MODEL)
SPLIT_POINTS = tuple(int(v) for v in np.cumsum(SPLITS)[:-1])

LANES = 128
SUBLANES = 8
VMEM_LIMIT = 56 * 1024 * 1024
Q_BLOCK = LANES
KEY_TILE = 256
INT_MIN = np.int32(-2 ** 31)
NEG_BIG = -1e30
KEY_LOWEST_FINITE = np.int32(-2 ** 31 + 2 ** 23)
LOG2_E = 1.4426950408889634
LATE_BITS = 4
WT_ROWS = 3 * ATT_WIDTH + 16
WN_COLS = 2 * ATT_WIDTH + LANES


def _params(semantics):
    return pltpu.CompilerParams(dimension_semantics=semantics, vmem_limit_bytes=VMEM_LIMIT)


def _dot(a, b):
    return jnp.dot(a, b, preferred_element_type=F32)


def _layer_norm(x, g, b):
    mu = jnp.mean(x, axis=-1, keepdims=True)
    xc = x - mu
    var = jnp.mean(xc * xc, axis=-1, keepdims=True)
    return xc * lax.rsqrt(var + LN_EPS) * g + b


def _proj_kernel(x_ref, wt_ref, wn_ref, cosn_ref, sinn_ref, cost_ref, sint_ref,
                 qT_ref, qiT_ref, vT_ref, wiT_ref, k_ref, kb_ref, v_ref, ki_ref, kib_ref):
    tm = x_ref.shape[1]
    kt_out = vT_ref.shape[3]
    xb = x_ref[0].astype(BF16)
    zt = lax.dot_general(wt_ref[...], xb, (((1,), (1,)), ((), ())), preferred_element_type=F32)
    zn = _dot(xb, wn_ref[...])
    cost = cost_ref[...]
    sint = sint_ref[...]
    half = HEAD_DIM // 2
    for base, ref, scale in ((0, qT_ref, HEAD_DIM ** -0.5 * LOG2_E), (ATT_WIDTH, qiT_ref, IDX_DIM ** -0.5)):
        for h in range(N_HEADS):
            r = h * HEAD_DIM
            x1 = zt[base + r:base + r + half]
            x2 = zt[base + r + half:base + r + HEAD_DIM]
            ref[0, r:r + half, :] = ((x1 * cost - x2 * sint) * scale).astype(BF16)
            ref[0, r + half:r + HEAD_DIM, :] = ((x1 * sint + x2 * cost) * scale).astype(BF16)
    for c in range(tm // kt_out):
        vT_ref[0, c] = zt[2 * ATT_WIDTH:3 * ATT_WIDTH, c * kt_out:(c + 1) * kt_out].astype(BF16)
    wiT_ref[0] = zt[3 * ATT_WIDTH:3 * ATT_WIDTH + N_IDX_HEADS] * (N_IDX_HEADS ** -0.5)

    cosn = cosn_ref[...]
    sinn = sinn_ref[...]
    lane = lax.broadcasted_iota(I32, (tm, LANES), 1)
    first_half = (lane & half) == 0

    def rope_n(z):
        partner = jnp.where(first_half, pltpu.roll(z, LANES - half, 1), pltpu.roll(z, half, 1))
        return z * cosn + partner * sinn

    for g in range(ATT_WIDTH // LANES):
        kg = rope_n(zn[:, g * LANES:(g + 1) * LANES])
        k_ref[0, :, g * LANES:(g + 1) * LANES] = kg
        kb_ref[0, :, g * LANES:(g + 1) * LANES] = kg.astype(BF16)
    v_ref[0] = zn[:, ATT_WIDTH:2 * ATT_WIDTH]
    kig = rope_n(zn[:, 2 * ATT_WIDTH:2 * ATT_WIDTH + LANES])
    ki_ref[0] = kig[:, :IDX_DIM]
    kib_ref[0] = kig[:, :IDX_DIM].astype(BF16)


def _proj(x, wt, wn, tables, tm, kt_out):
    b, s, _ = x.shape
    cosn, sinn, cost, sint = tables
    ns = s // tm
    out_shape = (
        jax.ShapeDtypeStruct((b, ATT_WIDTH, s), BF16),
        jax.ShapeDtypeStruct((b, ATT_WIDTH, s), BF16),
        jax.ShapeDtypeStruct((b, s // kt_out, ATT_WIDTH, kt_out), BF16),
        jax.ShapeDtypeStruct((b, N_IDX_HEADS, s), F32),
        jax.ShapeDtypeStruct((b, s, ATT_WIDTH), F32),
        jax.ShapeDtypeStruct((b, s, ATT_WIDTH), BF16),
        jax.ShapeDtypeStruct((b, s, ATT_WIDTH), F32),
        jax.ShapeDtypeStruct((b, s, IDX_DIM), F32),
        jax.ShapeDtypeStruct((b, s, IDX_DIM), BF16),
    )
    tok = lambda w: pl.BlockSpec((1, tm, w), lambda i, j: (i, j, 0))
    feat = lambda r: pl.BlockSpec((1, r, tm), lambda i, j: (i, 0, j))
    return pl.pallas_call(
        _proj_kernel,
        grid=(b, ns),
        in_specs=[
            tok(D_MODEL),
            pl.BlockSpec((WT_ROWS, D_MODEL), lambda i, j: (0, 0)),
            pl.BlockSpec((D_MODEL, WN_COLS), lambda i, j: (0, 0)),
            pl.BlockSpec((tm, LANES), lambda i, j: (j, 0)),
            pl.BlockSpec((tm, LANES), lambda i, j: (j, 0)),
            pl.BlockSpec((HEAD_DIM // 2, tm), lambda i, j: (0, j)),
            pl.BlockSpec((HEAD_DIM // 2, tm), lambda i, j: (0, j)),
        ],
        out_specs=(
            feat(ATT_WIDTH), feat(ATT_WIDTH),
            pl.BlockSpec((1, tm // kt_out, ATT_WIDTH, kt_out), lambda i, j: (i, j, 0, 0)),
            feat(N_IDX_HEADS),
            tok(ATT_WIDTH), tok(ATT_WIDTH), tok(ATT_WIDTH), tok(IDX_DIM), tok(IDX_DIM),
        ),
        out_shape=out_shape,
        compiler_params=_params(("parallel", "parallel")),
        name="proj",
    )(x, wt, wn, cosn, sinn, cost, sint)


def _order_key_to_f32(key):
    return lax.bitcast_convert_type(jnp.where(key < 0, key ^ 0x7FFFFFFF, key), F32)


def _attn_kernel(nk_ref, lim_ref, qT_ref, qiT_ref, wiT_ref, k_ref, vT_ref, ki_ref, o_ref,
                 score_scr, bias_scr, logit_scr, acc_scr):
    nk = nk_ref[pl.program_id(1)]
    lim = lim_ref[0]
    w = wiT_ref[0]
    qi = qiT_ref[0]
    q = qT_ref[0]
    groups = KEY_TILE // SUBLANES

    def rows(kt):
        return pl.ds(pl.multiple_of(kt * KEY_TILE, KEY_TILE), KEY_TILE)

    def fold(x, op):
        return op(x.reshape(groups, SUBLANES, Q_BLOCK), axis=0)

    def tile_loop(body, init):
        c = lax.fori_loop(0, nk // 2, lambda i, c: body(2 * i + 1, body(2 * i, c)), init)
        return lax.cond(nk % 2 == 1, lambda c: body(nk - 1, c), lambda c: c, c)

    qi_pairs = [jnp.concatenate([qi[(2 * p) * IDX_DIM:(2 * p + 1) * IDX_DIM],
                                 qi[(2 * p + 1) * IDX_DIM:(2 * p + 2) * IDX_DIM]], axis=1)
                for p in range(N_IDX_HEADS // 2)]
    row_iota = lax.broadcasted_iota(I32, (KEY_TILE, Q_BLOCK), 0)

    def score_tile(kt, carry):
        ki_t = ki_ref[0, rows(kt), :]
        acc = jnp.zeros((KEY_TILE, Q_BLOCK), F32)
        for p in range(N_IDX_HEADS // 2):
            d = _dot(ki_t, qi_pairs[p])
            acc = acc + jnp.maximum(d[:, :Q_BLOCK], 0.0) * w[2 * p:2 * p + 1]
            acc = acc + jnp.maximum(d[:, Q_BLOCK:], 0.0) * w[2 * p + 1:2 * p + 2]
        score_scr[rows(kt), :] = jnp.where(row_iota + kt * KEY_TILE < lim, acc, -jnp.inf)
        return carry

    tile_loop(score_tile, 0)

    def count(pred):
        def body(kt, c):
            return c + fold(jnp.where(pred(score_scr[rows(kt), :]), 1, 0), jnp.sum)

        return jnp.sum(tile_loop(body, jnp.zeros((SUBLANES, Q_BLOCK), I32)), axis=0, keepdims=True)

    def bisect_for(n_tiles):
        def bisect(i, carry):
            t, cnt_t = carry
            cand = t + lax.shift_left(jnp.int32(1), 31 - i)
            cand_f = _order_key_to_f32(cand)
            c = jnp.zeros((SUBLANES, Q_BLOCK), I32)
            for kt in range(n_tiles):
                c = c + fold(jnp.where(score_scr[kt * KEY_TILE:(kt + 1) * KEY_TILE, :] >= cand_f, 1, 0), jnp.sum)
            cnt = jnp.sum(c, axis=0, keepdims=True)
            ok = cnt >= MAX_TOPK
            return jnp.where(ok, cand, t), jnp.where(ok, cnt, cnt_t)

        def run():
            t0 = jnp.full((1, Q_BLOCK), INT_MIN, I32)
            c0 = jnp.full((1, Q_BLOCK), n_tiles * KEY_TILE, I32)
            t, c = lax.fori_loop(0, 32 - LATE_BITS, bisect, (t0, c0))
            settled = jnp.logical_or(c == MAX_TOPK, lim <= MAX_TOPK)
            all_settled = jnp.min(jnp.where(settled, 1.0, 0.0)) > 0.0
            return lax.cond(all_settled, lambda tc: tc,
                            lambda tc: lax.fori_loop(32 - LATE_BITS, 32, bisect, tc), (t, c))
        return run

    max_tiles = score_scr.shape[0] // KEY_TILE
    thr, cnt_thr = lax.switch(nk - 1, [bisect_for(n) for n in range(1, max_tiles + 1)])
    found = thr >= KEY_LOWEST_FINITE
    thr_f = jnp.where(found, _order_key_to_f32(thr), jnp.finfo(F32).min)

    def bias_tile(kt, carry):
        bias_scr[rows(kt), :] = jnp.where(score_scr[rows(kt), :] >= thr_f, 0.0, NEG_BIG)
        return carry

    tile_loop(bias_tile, 0)

    tie = jnp.logical_and(cnt_thr > MAX_TOPK, found)
    any_tie = jnp.max(jnp.where(tie, 1.0, 0.0)) > 0.0

    @pl.when(any_tie)
    def _():
        need = (MAX_TOPK - count(lambda s: s > thr_f)).astype(F32)
        tri = (lax.broadcasted_iota(I32, (KEY_TILE, KEY_TILE), 1)
               <= lax.broadcasted_iota(I32, (KEY_TILE, KEY_TILE), 0))
        tri = jnp.where(tri, 1.0, 0.0).astype(BF16)

        def tie_tile(kt, before):
            s = score_scr[rows(kt), :]
            eq = s == thr_f
            rank = _dot(tri, jnp.where(eq, 1.0, 0.0).astype(BF16)) + before
            sel = jnp.logical_or(s > thr_f, jnp.logical_and(eq, rank <= need))
            bias_scr[rows(kt), :] = jnp.where(sel, 0.0, NEG_BIG)
            return rank[KEY_TILE - 1:KEY_TILE, :]

        lax.fori_loop(0, nk, tie_tile, jnp.zeros((1, Q_BLOCK), F32))

    zeros_half = jnp.zeros((HEAD_DIM, Q_BLOCK), BF16)
    rhs = [jnp.concatenate(
        [jnp.concatenate([q[(2 * p) * HEAD_DIM:(2 * p + 1) * HEAD_DIM], zeros_half], axis=0),
         jnp.concatenate([zeros_half, q[(2 * p + 1) * HEAD_DIM:(2 * p + 2) * HEAD_DIM]], axis=0)], axis=1)
        for p in range(N_HEADS // 2)]

    def logits_tile(kt, m_parts):
        bias = bias_scr[rows(kt), :]
        out = []
        for p in range(N_HEADS // 2):
            s2 = _dot(k_ref[0, rows(kt), p * LANES:(p + 1) * LANES], rhs[p])
            for e in range(2):
                h = 2 * p + e
                s = s2[:, e * Q_BLOCK:(e + 1) * Q_BLOCK] + bias
                logit_scr[h, rows(kt), :] = s
                out.append(jnp.maximum(m_parts[h], fold(s, jnp.max)))
        return tuple(out)

    m_parts = tile_loop(logits_tile, tuple(jnp.full((SUBLANES, Q_BLOCK), NEG_BIG, F32) for _ in range(N_HEADS)))
    m = [jnp.max(mp, axis=0, keepdims=True) for mp in m_parts]
    acc_scr[...] = jnp.zeros_like(acc_scr)

    def pv_tile(kt, l_parts):
        out = []
        for h in range(N_HEADS):
            p = jnp.exp2(logit_scr[h, rows(kt), :] - m[h])
            out.append(l_parts[h] + fold(p, jnp.sum))
            v_t = vT_ref[0, kt, h * HEAD_DIM:(h + 1) * HEAD_DIM, :]
            acc_scr[h * HEAD_DIM:(h + 1) * HEAD_DIM, :] += _dot(v_t, p.astype(BF16))
        return tuple(out)

    l_parts = tile_loop(pv_tile, tuple(jnp.zeros((SUBLANES, Q_BLOCK), F32) for _ in range(N_HEADS)))
    outs = [acc_scr[h * HEAD_DIM:(h + 1) * HEAD_DIM, :] / jnp.sum(l_parts[h], axis=0, keepdims=True)
            for h in range(N_HEADS)]
    o_ref[0] = jnp.concatenate(outs, axis=0).T.astype(BF16)


def _attn(nk, limits, qT, qiT, wiT, kb, vT, kib):
    b, _, sq = qT.shape
    nq = sq // Q_BLOCK
    lk = kb.shape[1]
    grid_spec = pltpu.PrefetchScalarGridSpec(
        num_scalar_prefetch=1,
        grid=(b, nq),
        in_specs=[
            pl.BlockSpec((1, 1, Q_BLOCK), lambda i, j, nk: (j, 0, 0)),
            pl.BlockSpec((1, ATT_WIDTH, Q_BLOCK), lambda i, j, nk: (i, 0, j)),
            pl.BlockSpec((1, ATT_WIDTH, Q_BLOCK), lambda i, j, nk: (i, 0, j)),
            pl.BlockSpec((1, N_IDX_HEADS, Q_BLOCK), lambda i, j, nk: (i, 0, j)),
            pl.BlockSpec((1, lk, ATT_WIDTH), lambda i, j, nk: (i, 0, 0)),
            pl.BlockSpec((1, lk // KEY_TILE, ATT_WIDTH, KEY_TILE), lambda i, j, nk: (i, 0, 0, 0)),
            pl.BlockSpec((1, lk, IDX_DIM), lambda i, j, nk: (i, 0, 0)),
        ],
        out_specs=pl.BlockSpec((1, Q_BLOCK, ATT_WIDTH), lambda i, j, nk: (i, j, 0)),
        scratch_shapes=[pltpu.VMEM((lk, Q_BLOCK), F32), pltpu.VMEM((lk, Q_BLOCK), F32),
                        pltpu.VMEM((N_HEADS, lk, Q_BLOCK), F32), pltpu.VMEM((ATT_WIDTH, Q_BLOCK), F32)],
    )
    return pl.pallas_call(
        _attn_kernel,
        grid_spec=grid_spec,
        out_shape=jax.ShapeDtypeStruct((b, sq, ATT_WIDTH), BF16),
        compiler_params=_params(("parallel", "arbitrary")),
        name="attn",
    )(nk, limits, qT, qiT, wiT, kb, vT, kib)


def _mixer_kernel(x_ref, attn_ref, cs_ref, h0_ref, wxr_ref, wgr_ref, cw_ref, cb_ref, wa_ref, ba_ref,
                  wx_ref, bx_ref, lam_ref, wga_ref, wgb_ref, wba_ref, wbr_ref, wout_ref, g_ref, b_ref,
                  wgt_ref, bgt_ref, o_ref, grp_ref, co_ref, hl_ref,
                  xbuf, a_scr, u_scr, h_scr, hc_scr):
    ts = x_ref.shape[1]

    @pl.when(pl.program_id(1) == 0)
    def _():
        xbuf[0:SUBLANES, :] = cs_ref[0]
        hc_scr[...] = h0_ref[0]

    x = x_ref[0]
    xb = x.astype(BF16)
    xr = _dot(xb, wxr_ref[...])
    gr = _dot(xb, wgr_ref[...])
    xbuf[SUBLANES:SUBLANES + ts, :] = xr
    cw = cw_ref[...]
    xc = cb_ref[...] + xbuf[5:5 + ts, :] * cw[0:1]
    xc = xc + xbuf[6:6 + ts, :] * cw[1:2]
    xc = xc + xbuf[7:7 + ts, :] * cw[2:3]
    xc = xc + xr * cw[3:4]
    tail = xbuf[ts:ts + SUBLANES, :]
    xbuf[0:SUBLANES, :] = tail
    co_ref[0] = tail

    xcb = xc.astype(BF16)

    def gate(w_ref, b_ref):
        parts = [_dot(xcb[:, n * RNN_BLOCK:(n + 1) * RNN_BLOCK], w_ref[n]) for n in range(N_RNN_BLOCKS)]
        return jax.nn.sigmoid(jnp.concatenate(parts, axis=1) + b_ref[...])

    r = gate(wa_ref, ba_ref)
    i = gate(wx_ref, bx_ref)
    nl = -lam_ref[...]
    softplus = jnp.maximum(nl, 0.0) + jnp.log1p(jnp.exp(-jnp.abs(nl)))
    log_a = (-LRU_C * r) * softplus
    a_scr[...] = jnp.exp(log_a)
    th = jnp.tanh(log_a)
    u_scr[...] = jnp.exp2(0.5 * jnp.log2(-2.0 * th / (1.0 - th))) * (i * xc)

    row = lax.broadcasted_iota(I32, (SUBLANES, D_RNN), 0)
    h_prev = hc_scr[...]
    for g in range(ts // SUBLANES):
        rs = slice(g * SUBLANES, (g + 1) * SUBLANES)
        a = a_scr[rs, :]
        b = u_scr[rs, :]
        for d in (1, 2, 4):
            a_sh = jnp.where(row >= d, pltpu.roll(a, d, 0), 1.0)
            b_sh = jnp.where(row >= d, pltpu.roll(b, d, 0), 0.0)
            b = a * b_sh + b
            a = a * a_sh
        h = b + a * h_prev
        h_scr[rs, :] = h
        h_prev = h[SUBLANES - 1:SUBLANES, :]
    hc_scr[...] = h_prev
    hl_ref[0] = h_prev
    y = (h_scr[...] * jax.nn.gelu(gr)).astype(BF16)

    ga = _dot(xb, wga_ref[...])
    gb = _dot(xb, wgb_ref[...])
    merged = (jax.nn.sigmoid(ga) * _dot(attn_ref[0], wba_ref[...])
              + jax.nn.sigmoid(gb) * _dot(y, wbr_ref[...]))
    mix = _dot(merged.astype(BF16), wout_ref[...])
    x1 = _layer_norm(DN_ALPHA * x + mix, g_ref[...], b_ref[...])
    o_ref[0] = x1
    logits = lax.dot_general(wgt_ref[...], x1.astype(BF16), (((1,), (1,)), ((), ())),
                             preferred_element_type=F32) + bgt_ref[...]
    best = logits[0:1]
    g_sel = jnp.zeros(best.shape, I32)
    for k in range(1, N_GROUPS):
        g_sel = jnp.where(logits[k:k + 1] > best, k, g_sel)
        best = jnp.maximum(best, logits[k:k + 1])
    grp_ref[0] = g_sel


def _mixer(x, attn, conv_state8, h0, wxr, wgr, conv_w, conv_b, wa, ba, wx, bx, lam,
           wga, wgb, wba, wbr, wout, g, b, wgt, bgt, ts):
    bsz, s, _ = x.shape
    ns = s // ts
    once = pl.Buffered(1)
    const = lambda a: pl.BlockSpec(a.shape, lambda i, j, nd=a.ndim: (0,) * nd, pipeline_mode=once)
    per_b = lambda r: pl.BlockSpec((1, r, D_RNN), lambda i, j: (i, 0, 0))
    tok = lambda w: pl.BlockSpec((1, ts, w), lambda i, j: (i, j, 0))
    weights = (wxr, wgr, conv_w, conv_b, wa, ba, wx, bx, lam, wga, wgb, wba, wbr, wout, g, b, wgt, bgt)
    x1, grp, co, hl = pl.pallas_call(
        _mixer_kernel,
        grid=(bsz, ns),
        in_specs=[tok(D_MODEL), tok(ATT_WIDTH), per_b(SUBLANES), per_b(1)] + [const(a) for a in weights],
        out_specs=(tok(D_MODEL), pl.BlockSpec((1, 1, ts), lambda i, j: (i * ns + j, 0, 0)),
                   per_b(SUBLANES), per_b(1)),
        out_shape=(
            jax.ShapeDtypeStruct((bsz, s, D_MODEL), F32),
            jax.ShapeDtypeStruct((bsz * ns, 1, ts), I32),
            jax.ShapeDtypeStruct((bsz, SUBLANES, D_RNN), F32),
            jax.ShapeDtypeStruct((bsz, 1, D_RNN), F32),
        ),
        scratch_shapes=[
            pltpu.VMEM((ts + SUBLANES, D_RNN), F32),
            pltpu.VMEM((ts, D_RNN), F32), pltpu.VMEM((ts, D_RNN), F32), pltpu.VMEM((ts, D_RNN), F32),
            pltpu.VMEM((1, D_RNN), F32),
        ],
        compiler_params=_params(("parallel", "arbitrary")),
        name="mixer",
    )(x, attn, conv_state8, h0, *weights)
    return x1.reshape(bsz * s, D_MODEL), grp.reshape(bsz * s), co, hl


def _rglru_kernel(x_ref, cs_ref, h0_ref, wxr_ref, wgr_ref, cw_ref, cb_ref, wa_ref, ba_ref,
                  wx_ref, bx_ref, lam_ref, y_ref, co_ref, hl_ref,
                  xbuf, a_scr, u_scr, h_scr, hc_scr):
    ts = x_ref.shape[1]

    @pl.when(pl.program_id(1) == 0)
    def _():
        xbuf[0:SUBLANES, :] = cs_ref[0]
        hc_scr[...] = h0_ref[0]

    xb = x_ref[0].astype(BF16)
    xr = _dot(xb, wxr_ref[...])
    gr = _dot(xb, wgr_ref[...])
    xbuf[SUBLANES:SUBLANES + ts, :] = xr
    cw = cw_ref[...]
    xc = cb_ref[...] + xbuf[5:5 + ts, :] * cw[0:1]
    xc = xc + xbuf[6:6 + ts, :] * cw[1:2]
    xc = xc + xbuf[7:7 + ts, :] * cw[2:3]
    xc = xc + xr * cw[3:4]
    tail = xbuf[ts:ts + SUBLANES, :]
    xbuf[0:SUBLANES, :] = tail
    co_ref[0] = tail

    xcb = xc.astype(BF16)

    def gate(w_ref, b_ref):
        parts = [_dot(xcb[:, n * RNN_BLOCK:(n + 1) * RNN_BLOCK], w_ref[n]) for n in range(N_RNN_BLOCKS)]
        return jax.nn.sigmoid(jnp.concatenate(parts, axis=1) + b_ref[...])

    r = gate(wa_ref, ba_ref)
    i = gate(wx_ref, bx_ref)
    nl = -lam_ref[...]
    softplus = jnp.maximum(nl, 0.0) + jnp.log1p(jnp.exp(-jnp.abs(nl)))
    log_a = (-LRU_C * r) * softplus
    a_scr[...] = jnp.exp(log_a)
    th = jnp.tanh(log_a)
    u_scr[...] = jnp.exp2(0.5 * jnp.log2(-2.0 * th / (1.0 - th))) * (i * xc)

    row = lax.broadcasted_iota(I32, (SUBLANES, D_RNN), 0)

    def group(g, h_prev):
        rs = pl.ds(pl.multiple_of(g * SUBLANES, SUBLANES), SUBLANES)
        a = a_scr[rs, :]
        b = u_scr[rs, :]
        for d in (1, 2, 4):
            a_sh = jnp.where(row >= d, pltpu.roll(a, d, 0), 1.0)
            b_sh = jnp.where(row >= d, pltpu.roll(b, d, 0), 0.0)
            b = a * b_sh + b
            a = a * a_sh
        h = b + a * h_prev
        h_scr[rs, :] = h
        return h[SUBLANES - 1:SUBLANES, :]

    h_last = lax.fori_loop(0, ts // SUBLANES, group, hc_scr[...])
    hc_scr[...] = h_last
    hl_ref[0] = h_last
    y_ref[0] = (h_scr[...] * jax.nn.gelu(gr)).astype(BF16)


def _rglru(x, conv_state8, h0, wxr, wgr, conv_w, conv_b, wa, ba, wx, bx, lam, ts):
    b, s, _ = x.shape
    const2 = lambda shape: pl.BlockSpec(shape, lambda i, j: (0, 0))
    const3 = lambda shape: pl.BlockSpec(shape, lambda i, j: (0, 0, 0))
    per_b = lambda r: pl.BlockSpec((1, r, D_RNN), lambda i, j: (i, 0, 0))
    return pl.pallas_call(
        _rglru_kernel,
        grid=(b, s // ts),
        in_specs=[
            pl.BlockSpec((1, ts, D_MODEL), lambda i, j: (i, j, 0)),
            per_b(SUBLANES), per_b(1),
            const2((D_MODEL, D_RNN)), const2((D_MODEL, D_RNN)),
            const2((CONV_WIDTH, D_RNN)), const2((1, D_RNN)),
            const3((N_RNN_BLOCKS, RNN_BLOCK, RNN_BLOCK)), const2((1, D_RNN)),
            const3((N_RNN_BLOCKS, RNN_BLOCK, RNN_BLOCK)), const2((1, D_RNN)),
            const2((1, D_RNN)),
        ],
        out_specs=(
            pl.BlockSpec((1, ts, D_RNN), lambda i, j: (i, j, 0)),
            per_b(SUBLANES), per_b(1),
        ),
        out_shape=(
            jax.ShapeDtypeStruct((b, s, D_RNN), BF16),
            jax.ShapeDtypeStruct((b, SUBLANES, D_RNN), F32),
            jax.ShapeDtypeStruct((b, 1, D_RNN), F32),
        ),
        scratch_shapes=[
            pltpu.VMEM((ts + SUBLANES, D_RNN), F32),
            pltpu.VMEM((ts, D_RNN), F32), pltpu.VMEM((ts, D_RNN), F32), pltpu.VMEM((ts, D_RNN), F32),
            pltpu.VMEM((1, D_RNN), F32),
        ],
        compiler_params=_params(("parallel", "arbitrary")),
        name="rglru",
    )(x, conv_state8, h0, wxr, wgr, conv_w, conv_b, wa, ba, wx, bx, lam)


def _merge_kernel(x_ref, attn_ref, y_ref, wga_ref, wgb_ref, wba_ref, wbr_ref, wout_ref, g_ref, b_ref,
                  wgt_ref, bgt_ref, o_ref, grp_ref):
    x = x_ref[...]
    xb = x.astype(BF16)
    ga = _dot(xb, wga_ref[...])
    gb = _dot(xb, wgb_ref[...])
    merged = (jax.nn.sigmoid(ga) * _dot(attn_ref[...], wba_ref[...])
              + jax.nn.sigmoid(gb) * _dot(y_ref[...], wbr_ref[...]))
    mix = _dot(merged.astype(BF16), wout_ref[...])
    x1 = _layer_norm(DN_ALPHA * x + mix, g_ref[...], b_ref[...])
    o_ref[...] = x1
    logits = lax.dot_general(wgt_ref[...], x1.astype(BF16), (((1,), (1,)), ((), ())),
                             preferred_element_type=F32) + bgt_ref[...]
    best = logits[0:1]
    g_sel = jnp.zeros(best.shape, I32)
    for k in range(1, N_GROUPS):
        g_sel = jnp.where(logits[k:k + 1] > best, k, g_sel)
        best = jnp.maximum(best, logits[k:k + 1])
    grp_ref[0] = g_sel


def _merge(x, attn, y, wga, wgb, wba, wbr, wout, g, b, wgt, bgt, tm):
    t = x.shape[0]
    tok = lambda w: pl.BlockSpec((tm, w), lambda i: (i, 0))
    const = lambda shape: pl.BlockSpec(shape, lambda i: (0, 0))
    x1, grp = pl.pallas_call(
        _merge_kernel,
        grid=(t // tm,),
        in_specs=[tok(D_MODEL), tok(ATT_WIDTH), tok(D_RNN),
                  const((D_MODEL, D_MODEL)), const((D_MODEL, D_MODEL)),
                  const((ATT_WIDTH, D_MODEL)), const((D_RNN, D_MODEL)), const((D_MODEL, D_MODEL)),
                  const((1, D_MODEL)), const((1, D_MODEL)), const(wgt.shape), const(bgt.shape)],
        out_specs=(tok(D_MODEL), pl.BlockSpec((1, 1, tm), lambda i: (i, 0, 0))),
        out_shape=(jax.ShapeDtypeStruct((t, D_MODEL), F32), jax.ShapeDtypeStruct((t // tm, 1, tm), I32)),
        compiler_params=_params(("parallel",)),
        name="merge",
    )(x, attn, y, wga, wgb, wba, wbr, wout, g, b, wgt, bgt)
    return x1, grp.reshape(t)


def _route_plan(g, tile):
    t = g.shape[0]
    onehot = (g[:, None] == jnp.arange(N_GROUPS, dtype=I32)[None, :]).astype(I32)
    csum = jnp.cumsum(onehot, axis=0)
    padded = (csum[-1] + tile - 1) // tile * tile
    ends = jnp.cumsum(padded)
    pos = jnp.sum((ends - padded)[None, :] * onehot, axis=1) + jnp.sum(csum * onehot, axis=1) - 1
    n_tiles = t // tile + N_GROUPS
    tile_start = jnp.arange(n_tiles, dtype=I32) * tile
    tile_group = jnp.minimum(jnp.sum((tile_start[:, None] >= ends[None, :]).astype(I32), axis=1), N_GROUPS - 1)
    return pos.astype(I32), tile_group.astype(I32), n_tiles


ROW_DMA_UNROLL = 8


def _start_rows(copy, n):
    for r in range(n):
        copy(r).start(priority=r % 2)


def _wait_rows(copy, n):
    def batch(i, c):
        for _ in range(ROW_DMA_UNROLL):
            copy.wait()
        return c

    lax.fori_loop(0, n // ROW_DMA_UNROLL, batch, 0)


def _moe_sort_kernel(pos_ref, x_ref, init_ref, xs_ref, sbuf, sems):
    del init_ref
    ts = x_ref.shape[0]
    i = pl.program_id(0)
    base = i * ts
    slot = i % 2
    sbuf[slot] = x_ref[...]

    def copy(r, s):
        return pltpu.make_async_copy(sbuf.at[s, pl.ds(r, 1)],
                                     xs_ref.at[pl.ds(pos_ref[base + r], 1)], sems.at[s])

    _start_rows(lambda r: copy(r, slot), ts)

    @pl.when(i > 0)
    def _():
        _wait_rows(copy(0, 1 - slot), ts)

    @pl.when(i == pl.num_programs(0) - 1)
    def _():
        _wait_rows(copy(0, slot), ts)


def _moe_expert_kernel(tg_ref, xs_ref, wr_ref, br_ref, wgu_ref, wd_ref, ys_ref):
    g = tg_ref[pl.program_id(0)]
    xb = xs_ref[...].astype(BF16)
    tm = xb.shape[0]
    lane = lax.broadcasted_iota(I32, (tm, LANES), 1)
    logits = _dot(xb, wr_ref[...]) + br_ref[...]
    neg_inf = -jnp.inf
    eg = jnp.exp(jnp.where(lane < N_GROUPS, logits, neg_inf)
                 - jnp.max(jnp.where(lane < N_GROUPS, logits, neg_inf), axis=1, keepdims=True))
    p_group = (jnp.sum(jnp.where(lane == g, eg, 0.0), axis=1, keepdims=True)
               / jnp.sum(eg, axis=1, keepdims=True))
    lo = N_GROUPS + EXPERTS_PER_GROUP * g
    el = jnp.where(jnp.logical_and(lane >= lo, lane < lo + EXPERTS_PER_GROUP), logits, neg_inf)
    v1 = jnp.max(el, axis=1, keepdims=True)
    i1 = jnp.min(jnp.where(el == v1, lane, LANES), axis=1, keepdims=True)
    el2 = jnp.where(lane == i1, neg_inf, el)
    v2 = jnp.max(el2, axis=1, keepdims=True)
    i2 = jnp.min(jnp.where(el2 == v2, lane, LANES), axis=1, keepdims=True)
    e2 = jnp.exp(v2 - v1)
    p1 = p_group / (1.0 + e2)
    p2 = e2 * p1
    acts = []
    for j in range(EXPERTS_PER_GROUP):
        gate = jnp.where(i1 == lo + j, p1, 0.0) + jnp.where(i2 == lo + j, p2, 0.0)
        hj = _dot(xb, wgu_ref[j])
        acts.append((gate * (jax.nn.silu(hj[:, :D_EXPERT]) * hj[:, D_EXPERT:])).astype(BF16))
    ys_ref[...] = _dot(jnp.concatenate(acts, axis=1), wd_ref[0])


def _moe_unsort_kernel(pos_ref, x_ref, ys_ref, g_ref, b_ref, o_ref, ybuf, sems):
    ts = x_ref.shape[0]
    i = pl.program_id(0)
    slot = i % 2

    def copy(step, r, s):
        return pltpu.make_async_copy(ys_ref.at[pl.ds(pos_ref[step * ts + r], 1)],
                                     ybuf.at[s, pl.ds(r, 1)], sems.at[s])

    @pl.when(i == 0)
    def _():
        _start_rows(lambda r: copy(0, r, 0), ts)

    @pl.when(i < pl.num_programs(0) - 1)
    def _():
        _start_rows(lambda r: copy(i + 1, r, 1 - slot), ts)

    _wait_rows(copy(i, 0, slot), ts)
    o_ref[...] = _layer_norm(DN_ALPHA * x_ref[...] + ybuf[slot], g_ref[...], b_ref[...])


def _moe(x, grp, wr, br, wgu, wd, g, b, ts, tile):
    t = x.shape[0]
    pos, tile_group, n_tiles = _route_plan(grp, tile)
    any_spec = pl.BlockSpec(memory_space=pl.ANY)
    xs = pl.pallas_call(
        _moe_sort_kernel,
        grid_spec=pltpu.PrefetchScalarGridSpec(
            num_scalar_prefetch=1, grid=(t // ts,),
            in_specs=[pl.BlockSpec((ts, D_MODEL), lambda i, pos: (i, 0)), any_spec],
            out_specs=any_spec,
            scratch_shapes=[pltpu.VMEM((2, ts, D_MODEL), F32), pltpu.SemaphoreType.DMA((2,))]),
        out_shape=jax.ShapeDtypeStruct((n_tiles * tile, D_MODEL), F32),
        input_output_aliases={2: 0},
        compiler_params=_params(("arbitrary",)),
        name="moe_sort",
    )(pos, x, jnp.zeros((n_tiles * tile, D_MODEL), F32))
    ys = pl.pallas_call(
        _moe_expert_kernel,
        grid_spec=pltpu.PrefetchScalarGridSpec(
            num_scalar_prefetch=1, grid=(n_tiles,),
            in_specs=[
                pl.BlockSpec((tile, D_MODEL), lambda i, tg: (i, 0)),
                pl.BlockSpec((D_MODEL, LANES), lambda i, tg: (0, 0)),
                pl.BlockSpec((1, LANES), lambda i, tg: (0, 0)),
                pl.BlockSpec((EXPERTS_PER_GROUP, D_MODEL, 2 * D_EXPERT), lambda i, tg: (tg[i], 0, 0)),
                pl.BlockSpec((1, EXPERTS_PER_GROUP * D_EXPERT, D_MODEL), lambda i, tg: (tg[i], 0, 0)),
            ],
            out_specs=pl.BlockSpec((tile, D_MODEL), lambda i, tg: (i, 0))),
        out_shape=jax.ShapeDtypeStruct((n_tiles * tile, D_MODEL), F32),
        compiler_params=_params(("arbitrary",)),
        name="moe_experts",
    )(tile_group, xs, wr, br, wgu, wd)
    return pl.pallas_call(
        _moe_unsort_kernel,
        grid_spec=pltpu.PrefetchScalarGridSpec(
            num_scalar_prefetch=1, grid=(t // ts,),
            in_specs=[pl.BlockSpec((ts, D_MODEL), lambda i, pos: (i, 0)), any_spec,
                      pl.BlockSpec((1, D_MODEL), lambda i, pos: (0, 0)),
                      pl.BlockSpec((1, D_MODEL), lambda i, pos: (0, 0))],
            out_specs=pl.BlockSpec((ts, D_MODEL), lambda i, pos: (i, 0)),
            scratch_shapes=[pltpu.VMEM((2, ts, D_MODEL), F32), pltpu.SemaphoreType.DMA((2,))]),
        out_shape=jax.ShapeDtypeStruct((t, D_MODEL), F32),
        compiler_params=_params(("arbitrary",)),
        name="moe_unsort",
    )(pos, x, ys, g, b)


def _rope_tables(pos):
    half = HEAD_DIM // 2
    freqs = ROPE_THETA ** (-jnp.arange(half, dtype=F32) / half)
    ang = pos.astype(F32)[:, None] * freqs[None, :]
    cos, sin = jnp.cos(ang), jnp.sin(ang)
    cosn = jnp.tile(cos, (1, LANES // half))
    sinn = jnp.tile(jnp.concatenate([-sin, sin], axis=1), (1, LANES // HEAD_DIM))
    return cosn, sinn, cos.T, sin.T


def _pad_axis(a, axis, size):
    pad = [(0, 0)] * a.ndim
    pad[axis] = (0, size - a.shape[axis])
    return jnp.pad(a, pad)


def kernel(x_prompt, x_sample, cache_k, cache_v, cache_k_idx, state_conv, state_h, w_in, conv_w, conv_b, w_rg_a, b_rg_a, w_rg_x, b_rg_x, lru_lambda, w_branch_attn, w_branch_rnn, w_out, ln1_g, ln1_b, w_router_group, b_router_group, w_router_expert, b_router_expert, w_gate_up, w_down, ln2_g, ln2_b):
    assert w_in.shape[0] == DEPTH == 1
    bp, sp, _ = x_prompt.shape
    bs, ss, _ = x_sample.shape
    past = cache_k.shape[2]
    ns_tok = bs * ss
    assert ns_tok == Q_BLOCK and sp % Q_BLOCK == 0 and sp % KEY_TILE == 0

    wq, wk, wv, wqi, wki, wwi, wxr, wgr, wga, wgb = jnp.split(w_in[0], SPLIT_POINTS, axis=1)
    wt = jnp.concatenate([wq.T, wqi.T, wv.T, _pad_axis(wwi.T, 0, 16)], axis=0).astype(BF16)
    wn = jnp.concatenate([wk, wv, _pad_axis(wki, 1, LANES)], axis=1).astype(BF16)
    row = lambda v: v.reshape(1, -1)
    rg = (wxr.astype(BF16), wgr.astype(BF16), conv_w[0], row(conv_b[0]),
          w_rg_a[0].astype(BF16), row(b_rg_a[0]), w_rg_x[0].astype(BF16), row(b_rg_x[0]),
          row(lru_lambda[0]))
    mg = (wga.astype(BF16), wgb.astype(BF16), w_branch_attn[0].astype(BF16),
          w_branch_rnn[0].astype(BF16), w_out[0].astype(BF16), row(ln1_g[0]), row(ln1_b[0]))
    wr = _pad_axis(jnp.concatenate([w_router_group[0], w_router_expert[0]], axis=1), 1, LANES).astype(BF16)
    br = _pad_axis(jnp.concatenate([b_router_group[0], b_router_expert[0]]).reshape(1, -1), 1, LANES)
    wgt = _pad_axis(w_router_group[0].T, 0, 16).astype(BF16)
    bgt = _pad_axis(b_router_group[0].reshape(-1, 1), 0, 16)
    wd = w_down[0].reshape(N_GROUPS, EXPERTS_PER_GROUP * D_EXPERT, D_MODEL).astype(BF16)
    mo = (wr, br, w_gate_up[0].astype(BF16), wd, row(ln2_g[0]), row(ln2_b[0]))

    pos_p = jnp.arange(sp, dtype=I32)
    qT, qiT, vT, wiT, k_p, kb_p, v_p, ki_p, kib_p = _proj(x_prompt, wt, wn, _rope_tables(pos_p), 512, KEY_TILE)
    lim_p = jnp.minimum((pos_p // CHUNK + 1) * CHUNK, sp).reshape(sp // Q_BLOCK, 1, Q_BLOCK)
    nk_p = (jnp.max(lim_p, axis=(1, 2)) + KEY_TILE - 1) // KEY_TILE
    attn_p = _attn(nk_p.astype(I32), lim_p, qT, qiT, wiT, kb_p, vT, kib_p)
    x1_p, grp_p, co_p, hl_p = _mixer(x_prompt, attn_p, jnp.zeros((bp, SUBLANES, D_RNN), F32),
                                     jnp.zeros((bp, 1, D_RNN), F32), *rg, *mg, wgt, bgt, ts=256)
    out_p = _moe(x1_p, grp_p, *mo, ts=256, tile=512)

    pos_s = past + (jnp.arange(ns_tok, dtype=I32) % ss)
    xs_flat = x_sample.reshape(1, ns_tok, D_MODEL)
    qT_s, qiT_s, _, wiT_s, k_s, kb_s, v_s, ki_s, kib_s = _proj(xs_flat, wt, wn, _rope_tables(pos_s), ns_tok, ns_tok)

    def per_batch_lanes(a):
        r = a.shape[1]
        return _pad_axis(a[0].reshape(r, bs, ss).transpose(1, 0, 2), 2, Q_BLOCK)

    l_all = past + ss
    lk = -(-l_all // KEY_TILE) * KEY_TILE
    k_all = jnp.concatenate([cache_k[0].reshape(bs, past, ATT_WIDTH).astype(BF16),
                             kb_s.reshape(bs, ss, ATT_WIDTH)], axis=1)
    v_all = jnp.concatenate([cache_v[0].reshape(bs, past, ATT_WIDTH),
                             v_s.reshape(bs, ss, ATT_WIDTH)], axis=1).astype(BF16)
    ki_all = jnp.concatenate([cache_k_idx[0].astype(BF16), kib_s.reshape(bs, ss, IDX_DIM)], axis=1)
    k_all, v_all, ki_all = (_pad_axis(a, 1, lk) for a in (k_all, v_all, ki_all))
    vT_all = v_all.reshape(bs, lk // KEY_TILE, KEY_TILE, ATT_WIDTH).transpose(0, 1, 3, 2)
    limit_s = min((past // CHUNK + 1) * CHUNK, l_all)
    assert (past + ss - 1) // CHUNK == past // CHUNK
    lim_s = jnp.where(jnp.arange(Q_BLOCK) < ss, limit_s, CHUNK).astype(I32).reshape(1, 1, Q_BLOCK)
    nk_s = jnp.full((1,), -(-limit_s // KEY_TILE), I32)
    attn_s = _attn(nk_s, lim_s, per_batch_lanes(qT_s), per_batch_lanes(qiT_s), per_batch_lanes(wiT_s),
                   k_all, vT_all, ki_all)[:, :ss]
    cs8 = jnp.pad(state_conv[0], ((0, 0), (SUBLANES - (CONV_WIDTH - 1), 0), (0, 0)))
    x1_s, grp_s, co_s, hl_s = _mixer(x_sample, attn_s, cs8, state_h[0][:, None, :], *rg, *mg, wgt, bgt, ts=ss)
    out_s = _moe(x1_s, grp_s, *mo, ts=ns_tok, tile=ns_tok)

    keep = CONV_WIDTH - 1
    return (out_p.reshape(bp, sp, D_MODEL), out_s.reshape(bs, ss, D_MODEL),
            k_p.reshape(1, bp, sp, N_HEADS, HEAD_DIM), v_p.reshape(1, bp, sp, N_HEADS, HEAD_DIM),
            ki_p[None], co_p[:, SUBLANES - keep:][None], hl_p[:, 0][None],
            k_s.reshape(1, bs, ss, N_HEADS, HEAD_DIM), v_s.reshape(1, bs, ss, N_HEADS, HEAD_DIM),
            ki_s.reshape(1, bs, ss, IDX_DIM), co_s[:, SUBLANES - keep:][None], hl_s[:, 0][None])
```

```python
import functools

import jax
import jax.numpy as jnp
import numpy as np
from jax import lax
from jax.experimental import pallas as pl
from jax.experimental.pallas import tpu as pltpu

F32 = jnp.float32
BF16 = jnp.bfloat16
I32 = jnp.int32

D_MODEL = 1024
N_HEADS = 8
HEAD_DIM = 64
ATT_WIDTH = N_HEADS * HEAD_DIM
N_IDX_HEADS = 8
IDX_DIM = 64
MAX_TOPK = 256
CHUNK = 64
D_RNN = D_MODEL
N_RNN_BLOCKS = 8
RNN_BLOCK = D_RNN // N_RNN_BLOCKS
CONV_WIDTH = 4
LRU_C = 8.0
N_GROUPS = 4
EXPERTS_PER_GROUP = 4
N_EXPERTS = N_GROUPS * EXPERTS_PER_GROUP
D_EXPERT = 256
ROPE_THETA = 10000.0
LN_EPS = 1e-5
DEPTH = 1
DN_ALPHA = (2.0 * DEPTH) ** 0.25
SPLITS = (ATT_WIDTH, ATT_WIDTH, ATT_WIDTH, N_IDX_HEADS * IDX_DIM, IDX_DIM, N_IDX_HEADS,
          D_RNN, D_RNN, D_MODEL, D_MODEL)
SPLIT_POINTS = tuple(int(v) for v in np.cumsum(SPLITS)[:-1])

LANES = 128
SUBLANES = 8
VMEM_LIMIT = 56 * 1024 * 1024
Q_BLOCK = LANES
PROMPT_Q_BLOCK = 256
KEY_TILE = 256
INT_MIN = np.int32(-2 ** 31)
NEG_BIG = -1e30
KEY_LOWEST_FINITE = np.int32(-2 ** 31 + 2 ** 23)
LOG2_E = 1.4426950408889634
LATE_BITS = 4
WT_ROWS = 3 * ATT_WIDTH + 16
WN_COLS = 2 * ATT_WIDTH + LANES


def _params(semantics):
    return pltpu.CompilerParams(dimension_semantics=semantics, vmem_limit_bytes=VMEM_LIMIT)


def _dot(a, b):
    return jnp.dot(a, b, preferred_element_type=F32)


def _layer_norm(x, g, b):
    mu = jnp.mean(x, axis=-1, keepdims=True)
    xc = x - mu
    var = jnp.mean(xc * xc, axis=-1, keepdims=True)
    return xc * lax.rsqrt(var + LN_EPS) * g + b


def _proj_kernel(x_ref, wt_ref, wn_ref, cosn_ref, sinn_ref, cost_ref, sint_ref,
                 qT_ref, qiT_ref, vT_ref, wiT_ref, k_ref, kb_ref, v_ref, ki_ref, kib_ref):
    tm = x_ref.shape[1]
    kt_out = vT_ref.shape[3]
    xb = x_ref[0].astype(BF16)
    zt = lax.dot_general(wt_ref[...], xb, (((1,), (1,)), ((), ())), preferred_element_type=F32)
    zn = _dot(xb, wn_ref[...])
    cost = cost_ref[...]
    sint = sint_ref[...]
    half = HEAD_DIM // 2
    for base, ref, scale in ((0, qT_ref, HEAD_DIM ** -0.5 * LOG2_E), (ATT_WIDTH, qiT_ref, IDX_DIM ** -0.5)):
        for h in range(N_HEADS):
            r = h * HEAD_DIM
            x1 = zt[base + r:base + r + half]
            x2 = zt[base + r + half:base + r + HEAD_DIM]
            ref[0, r:r + half, :] = ((x1 * cost - x2 * sint) * scale).astype(BF16)
            ref[0, r + half:r + HEAD_DIM, :] = ((x1 * sint + x2 * cost) * scale).astype(BF16)
    for c in range(tm // kt_out):
        vT_ref[0, c] = zt[2 * ATT_WIDTH:3 * ATT_WIDTH, c * kt_out:(c + 1) * kt_out].astype(BF16)
    wiT_ref[0] = zt[3 * ATT_WIDTH:3 * ATT_WIDTH + N_IDX_HEADS] * (N_IDX_HEADS ** -0.5)

    cosn = cosn_ref[...]
    sinn = sinn_ref[...]
    lane = lax.broadcasted_iota(I32, (tm, LANES), 1)
    first_half = (lane & half) == 0

    def rope_n(z):
        partner = jnp.where(first_half, pltpu.roll(z, LANES - half, 1), pltpu.roll(z, half, 1))
        return z * cosn + partner * sinn

    for g in range(ATT_WIDTH // LANES):
        kg = rope_n(zn[:, g * LANES:(g + 1) * LANES])
        k_ref[0, :, g * LANES:(g + 1) * LANES] = kg
        kb_ref[0, :, g * LANES:(g + 1) * LANES] = kg.astype(BF16)
    v_ref[0] = zn[:, ATT_WIDTH:2 * ATT_WIDTH]
    kig = rope_n(zn[:, 2 * ATT_WIDTH:2 * ATT_WIDTH + LANES])
    ki_ref[0] = kig[:, :IDX_DIM]
    kib_ref[0] = kig[:, :IDX_DIM].astype(BF16)


def _proj(x, wt, wn, tables, tm, kt_out):
    b, s, _ = x.shape
    cosn, sinn, cost, sint = tables
    ns = s // tm
    out_shape = (
        jax.ShapeDtypeStruct((b, ATT_WIDTH, s), BF16),
        jax.ShapeDtypeStruct((b, ATT_WIDTH, s), BF16),
        jax.ShapeDtypeStruct((b, s // kt_out, ATT_WIDTH, kt_out), BF16),
        jax.ShapeDtypeStruct((b, N_IDX_HEADS, s), F32),
        jax.ShapeDtypeStruct((b, s, ATT_WIDTH), F32),
        jax.ShapeDtypeStruct((b, s, ATT_WIDTH), BF16),
        jax.ShapeDtypeStruct((b, s, ATT_WIDTH), F32),
        jax.ShapeDtypeStruct((b, s, IDX_DIM), F32),
        jax.ShapeDtypeStruct((b, s, IDX_DIM), BF16),
    )
    tok = lambda w: pl.BlockSpec((1, tm, w), lambda i, j: (i, j, 0))
    feat = lambda r: pl.BlockSpec((1, r, tm), lambda i, j: (i, 0, j))
    return pl.pallas_call(
        _proj_kernel,
        grid=(b, ns),
        in_specs=[
            tok(D_MODEL),
            pl.BlockSpec((WT_ROWS, D_MODEL), lambda i, j: (0, 0)),
            pl.BlockSpec((D_MODEL, WN_COLS), lambda i, j: (0, 0)),
            pl.BlockSpec((tm, LANES), lambda i, j: (j, 0)),
            pl.BlockSpec((tm, LANES), lambda i, j: (j, 0)),
            pl.BlockSpec((HEAD_DIM // 2, tm), lambda i, j: (0, j)),
            pl.BlockSpec((HEAD_DIM // 2, tm), lambda i, j: (0, j)),
        ],
        out_specs=(
            feat(ATT_WIDTH), feat(ATT_WIDTH),
            pl.BlockSpec((1, tm // kt_out, ATT_WIDTH, kt_out), lambda i, j: (i, j, 0, 0)),
            feat(N_IDX_HEADS),
            tok(ATT_WIDTH), tok(ATT_WIDTH), tok(ATT_WIDTH), tok(IDX_DIM), tok(IDX_DIM),
        ),
        out_shape=out_shape,
        compiler_params=_params(("parallel", "parallel")),
        name="proj",
    )(x, wt, wn, cosn, sinn, cost, sint)


def _order_key_to_f32(key):
    return lax.bitcast_convert_type(jnp.where(key < 0, key ^ 0x7FFFFFFF, key), F32)


def _attn_kernel(nk_ref, lim_ref, qT_ref, qiT_ref, wiT_ref, k_ref, vT_ref, ki_ref, o_ref,
                 score_scr, bias_scr, logit_scr, acc_scr):
    nk = nk_ref[pl.program_id(1)]
    lim = lim_ref[0]
    w = wiT_ref[0]
    qi = qiT_ref[0]
    q = qT_ref[0]
    qb = q.shape[1]
    groups = KEY_TILE // SUBLANES

    def rows(kt):
        return pl.ds(pl.multiple_of(kt * KEY_TILE, KEY_TILE), KEY_TILE)

    def fold(x, op):
        return op(x.reshape(groups, SUBLANES, qb), axis=0)

    def tile_loop(body, init):
        if qb > LANES:
            return lax.fori_loop(0, nk, body, init)
        c = lax.fori_loop(0, nk // 2, lambda i, c: body(2 * i + 1, body(2 * i, c)), init)
        return lax.cond(nk % 2 == 1, lambda c: body(nk - 1, c), lambda c: c, c)

    qi_pairs = [jnp.concatenate([qi[(2 * p) * IDX_DIM:(2 * p + 1) * IDX_DIM],
                                 qi[(2 * p + 1) * IDX_DIM:(2 * p + 2) * IDX_DIM]], axis=1)
                for p in range(N_IDX_HEADS // 2)]
    row_iota = lax.broadcasted_iota(I32, (KEY_TILE, qb), 0)

    def score_tile(kt, carry):
        ki_t = ki_ref[0, rows(kt), :]
        acc = jnp.zeros((KEY_TILE, qb), F32)
        for p in range(N_IDX_HEADS // 2):
            d = _dot(ki_t, qi_pairs[p])
            acc = acc + jnp.maximum(d[:, :qb], 0.0) * w[2 * p:2 * p + 1]
            acc = acc + jnp.maximum(d[:, qb:], 0.0) * w[2 * p + 1:2 * p + 2]
        score_scr[rows(kt), :] = jnp.where(row_iota + kt * KEY_TILE < lim, acc, -jnp.inf)
        return carry

    tile_loop(score_tile, 0)

    def count(pred):
        def body(kt, c):
            return c + fold(jnp.where(pred(score_scr[rows(kt), :]), 1, 0), jnp.sum)

        return jnp.sum(tile_loop(body, jnp.zeros((SUBLANES, qb), I32)), axis=0, keepdims=True)

    def bisect_for(n_tiles):
        def bisect(i, carry):
            t, cnt_t = carry
            cand = t + lax.shift_left(jnp.int32(1), 31 - i)
            cand_f = _order_key_to_f32(cand)
            c = jnp.zeros((SUBLANES, qb), I32)
            for kt in range(n_tiles):
                c = c + fold(jnp.where(score_scr[kt * KEY_TILE:(kt + 1) * KEY_TILE, :] >= cand_f, 1, 0), jnp.sum)
            cnt = jnp.sum(c, axis=0, keepdims=True)
            ok = cnt >= MAX_TOPK
            return jnp.where(ok, cand, t), jnp.where(ok, cnt, cnt_t)

        def run():
            t0 = jnp.full((1, qb), INT_MIN, I32)
            c0 = jnp.full((1, qb), n_tiles * KEY_TILE, I32)
            t, c = lax.fori_loop(0, 32 - LATE_BITS, bisect, (t0, c0))
            settled = jnp.logical_or(c == MAX_TOPK, lim <= MAX_TOPK)
            all_settled = jnp.min(jnp.where(settled, 1.0, 0.0)) > 0.0
            return lax.cond(all_settled, lambda tc: tc,
                            lambda tc: lax.fori_loop(32 - LATE_BITS, 32, bisect, tc), (t, c))
        return run

    max_tiles = score_scr.shape[0] // KEY_TILE
    thr, cnt_thr = lax.switch(nk - 1, [bisect_for(n) for n in range(1, max_tiles + 1)])
    found = thr >= KEY_LOWEST_FINITE
    thr_f = jnp.where(found, _order_key_to_f32(thr), jnp.finfo(F32).min)

    def bias_tile(kt, carry):
        bias_scr[rows(kt), :] = jnp.where(score_scr[rows(kt), :] >= thr_f, 0.0, NEG_BIG)
        return carry

    tile_loop(bias_tile, 0)

    tie = jnp.logical_and(cnt_thr > MAX_TOPK, found)
    any_tie = jnp.max(jnp.where(tie, 1.0, 0.0)) > 0.0

    @pl.when(any_tie)
    def _():
        need = (MAX_TOPK - count(lambda s: s > thr_f)).astype(F32)
        tri = (lax.broadcasted_iota(I32, (KEY_TILE, KEY_TILE), 1)
               <= lax.broadcasted_iota(I32, (KEY_TILE, KEY_TILE), 0))
        tri = jnp.where(tri, 1.0, 0.0).astype(BF16)

        def tie_tile(kt, before):
            s = score_scr[rows(kt), :]
            eq = s == thr_f
            rank = _dot(tri, jnp.where(eq, 1.0, 0.0).astype(BF16)) + before
            sel = jnp.logical_or(s > thr_f, jnp.logical_and(eq, rank <= need))
            bias_scr[rows(kt), :] = jnp.where(sel, 0.0, NEG_BIG)
            return rank[KEY_TILE - 1:KEY_TILE, :]

        lax.fori_loop(0, nk, tie_tile, jnp.zeros((1, qb), F32))

    zeros_half = jnp.zeros((HEAD_DIM, qb), BF16)
    rhs = [jnp.concatenate(
        [jnp.concatenate([q[(2 * p) * HEAD_DIM:(2 * p + 1) * HEAD_DIM], zeros_half], axis=0),
         jnp.concatenate([zeros_half, q[(2 * p + 1) * HEAD_DIM:(2 * p + 2) * HEAD_DIM]], axis=0)], axis=1)
        for p in range(N_HEADS // 2)]

    def logits_tile(kt, m_parts):
        bias = bias_scr[rows(kt), :]
        out = []
        for p in range(N_HEADS // 2):
            s2 = _dot(k_ref[0, rows(kt), p * LANES:(p + 1) * LANES], rhs[p])
            for e in range(2):
                h = 2 * p + e
                s = s2[:, e * qb:(e + 1) * qb] + bias
                logit_scr[h, rows(kt), :] = s
                out.append(jnp.maximum(m_parts[h], fold(s, jnp.max)))
        return tuple(out)

    m_parts = tile_loop(logits_tile, tuple(jnp.full((SUBLANES, qb), NEG_BIG, F32) for _ in range(N_HEADS)))
    m = [jnp.max(mp, axis=0, keepdims=True) for mp in m_parts]
    acc_scr[...] = jnp.zeros_like(acc_scr)

    def pv_tile(kt, l_parts):
        out = []
        for h in range(N_HEADS):
            p = jnp.exp2(logit_scr[h, rows(kt), :] - m[h])
            out.append(l_parts[h] + fold(p, jnp.sum))
            v_t = vT_ref[0, kt, h * HEAD_DIM:(h + 1) * HEAD_DIM, :]
            acc_scr[h * HEAD_DIM:(h + 1) * HEAD_DIM, :] += _dot(v_t, p.astype(BF16))
        return tuple(out)

    l_parts = tile_loop(pv_tile, tuple(jnp.zeros((SUBLANES, qb), F32) for _ in range(N_HEADS)))
    outs = [acc_scr[h * HEAD_DIM:(h + 1) * HEAD_DIM, :] / jnp.sum(l_parts[h], axis=0, keepdims=True)
            for h in range(N_HEADS)]
    o_ref[0] = jnp.concatenate(outs, axis=0).T.astype(BF16)


def _attn(nk, limits, qT, qiT, wiT, kb, vT, kib):
    b, _, sq = qT.shape
    qb = limits.shape[-1]
    nq = sq // qb
    lk = kb.shape[1]
    grid_spec = pltpu.PrefetchScalarGridSpec(
        num_scalar_prefetch=1,
        grid=(b, nq),
        in_specs=[
            pl.BlockSpec((1, 1, qb), lambda i, j, nk: (j, 0, 0)),
            pl.BlockSpec((1, ATT_WIDTH, qb), lambda i, j, nk: (i, 0, j)),
            pl.BlockSpec((1, ATT_WIDTH, qb), lambda i, j, nk: (i, 0, j)),
            pl.BlockSpec((1, N_IDX_HEADS, qb), lambda i, j, nk: (i, 0, j)),
            pl.BlockSpec((1, lk, ATT_WIDTH), lambda i, j, nk: (i, 0, 0)),
            pl.BlockSpec((1, lk // KEY_TILE, ATT_WIDTH, KEY_TILE), lambda i, j, nk: (i, 0, 0, 0)),
            pl.BlockSpec((1, lk, IDX_DIM), lambda i, j, nk: (i, 0, 0)),
        ],
        out_specs=pl.BlockSpec((1, qb, ATT_WIDTH), lambda i, j, nk: (i, j, 0)),
        scratch_shapes=[pltpu.VMEM((lk, qb), F32), pltpu.VMEM((lk, qb), F32),
                        pltpu.VMEM((N_HEADS, lk, qb), F32), pltpu.VMEM((ATT_WIDTH, qb), F32)],
    )
    return pl.pallas_call(
        _attn_kernel,
        grid_spec=grid_spec,
        out_shape=jax.ShapeDtypeStruct((b, sq, ATT_WIDTH), BF16),
        compiler_params=_params(("parallel", "arbitrary")),
        name="attn",
    )(nk, limits, qT, qiT, wiT, kb, vT, kib)


def _mixer_kernel(x_ref, attn_ref, cs_ref, h0_ref, wxr_ref, wgr_ref, cw_ref, cb_ref, wa_ref, ba_ref,
                  wx_ref, bx_ref, lam_ref, wga_ref, wgb_ref, wba_ref, wbr_ref, wout_ref, g_ref, b_ref,
                  wgt_ref, bgt_ref, o_ref, grp_ref, co_ref, hl_ref,
                  xbuf, a_scr, u_scr, h_scr, hc_scr):
    ts = x_ref.shape[1]

    @pl.when(pl.program_id(1) == 0)
    def _():
        xbuf[0:SUBLANES, :] = cs_ref[0]
        hc_scr[...] = h0_ref[0]

    x = x_ref[0]
    xb = x.astype(BF16)
    xr = _dot(xb, wxr_ref[...])
    gr = _dot(xb, wgr_ref[...])
    xbuf[SUBLANES:SUBLANES + ts, :] = xr
    cw = cw_ref[...]
    xc = cb_ref[...] + xbuf[5:5 + ts, :] * cw[0:1]
    xc = xc + xbuf[6:6 + ts, :] * cw[1:2]
    xc = xc + xbuf[7:7 + ts, :] * cw[2:3]
    xc = xc + xr * cw[3:4]
    tail = xbuf[ts:ts + SUBLANES, :]
    xbuf[0:SUBLANES, :] = tail
    co_ref[0] = tail

    xcb = xc.astype(BF16)

    def gate(w_ref, b_ref):
        parts = [_dot(xcb[:, n * RNN_BLOCK:(n + 1) * RNN_BLOCK], w_ref[n]) for n in range(N_RNN_BLOCKS)]
        return jax.nn.sigmoid(jnp.concatenate(parts, axis=1) + b_ref[...])

    r = gate(wa_ref, ba_ref)
    i = gate(wx_ref, bx_ref)
    nl = -lam_ref[...]
    softplus = jnp.maximum(nl, 0.0) + jnp.log1p(jnp.exp(-jnp.abs(nl)))
    log_a = (-LRU_C * r) * softplus
    a_scr[...] = jnp.exp(log_a)
    th = jnp.tanh(log_a)
    u_scr[...] = jnp.exp2(0.5 * jnp.log2(-2.0 * th / (1.0 - th))) * (i * xc)

    row = lax.broadcasted_iota(I32, (SUBLANES, D_RNN), 0)
    h_prev = hc_scr[...]
    for g in range(ts // SUBLANES):
        rs = slice(g * SUBLANES, (g + 1) * SUBLANES)
        a = a_scr[rs, :]
        b = u_scr[rs, :]
        for d in (1, 2, 4):
            a_sh = jnp.where(row >= d, pltpu.roll(a, d, 0), 1.0)
            b_sh = jnp.where(row >= d, pltpu.roll(b, d, 0), 0.0)
            b = a * b_sh + b
            a = a * a_sh
        h = b + a * h_prev
        h_scr[rs, :] = h
        h_prev = h[SUBLANES - 1:SUBLANES, :]
    hc_scr[...] = h_prev
    hl_ref[0] = h_prev
    y = (h_scr[...] * jax.nn.gelu(gr)).astype(BF16)

    ga = _dot(xb, wga_ref[...])
    gb = _dot(xb, wgb_ref[...])
    merged = (jax.nn.sigmoid(ga) * _dot(attn_ref[0], wba_ref[...])
              + jax.nn.sigmoid(gb) * _dot(y, wbr_ref[...]))
    mix = _dot(merged.astype(BF16), wout_ref[...])
    x1 = _layer_norm(DN_ALPHA * x + mix, g_ref[...], b_ref[...])
    o_ref[0] = x1
    logits = lax.dot_general(wgt_ref[...], x1.astype(BF16), (((1,), (1,)), ((), ())),
                             preferred_element_type=F32) + bgt_ref[...]
    best = logits[0:1]
    g_sel = jnp.zeros(best.shape, I32)
    for k in range(1, N_GROUPS):
        g_sel = jnp.where(logits[k:k + 1] > best, k, g_sel)
        best = jnp.maximum(best, logits[k:k + 1])
    grp_ref[0] = g_sel


def _mixer(x, attn, conv_state8, h0, wxr, wgr, conv_w, conv_b, wa, ba, wx, bx, lam,
           wga, wgb, wba, wbr, wout, g, b, wgt, bgt, ts):
    bsz, s, _ = x.shape
    ns = s // ts
    once = pl.Buffered(1)
    const = lambda a: pl.BlockSpec(a.shape, lambda i, j, nd=a.ndim: (0,) * nd, pipeline_mode=once)
    per_b = lambda r: pl.BlockSpec((1, r, D_RNN), lambda i, j: (i, 0, 0))
    tok = lambda w: pl.BlockSpec((1, ts, w), lambda i, j: (i, j, 0))
    weights = (wxr, wgr, conv_w, conv_b, wa, ba, wx, bx, lam, wga, wgb, wba, wbr, wout, g, b, wgt, bgt)
    x1, grp, co, hl = pl.pallas_call(
        _mixer_kernel,
        grid=(bsz, ns),
        in_specs=[tok(D_MODEL), tok(ATT_WIDTH), per_b(SUBLANES), per_b(1)] + [const(a) for a in weights],
        out_specs=(tok(D_MODEL), pl.BlockSpec((1, 1, ts), lambda i, j: (i * ns + j, 0, 0)),
                   per_b(SUBLANES), per_b(1)),
        out_shape=(
            jax.ShapeDtypeStruct((bsz, s, D_MODEL), F32),
            jax.ShapeDtypeStruct((bsz * ns, 1, ts), I32),
            jax.ShapeDtypeStruct((bsz, SUBLANES, D_RNN), F32),
            jax.ShapeDtypeStruct((bsz, 1, D_RNN), F32),
        ),
        scratch_shapes=[
            pltpu.VMEM((ts + SUBLANES, D_RNN), F32),
            pltpu.VMEM((ts, D_RNN), F32), pltpu.VMEM((ts, D_RNN), F32), pltpu.VMEM((ts, D_RNN), F32),
            pltpu.VMEM((1, D_RNN), F32),
        ],
        compiler_params=_params(("parallel", "arbitrary")),
        name="mixer",
    )(x, attn, conv_state8, h0, *weights)
    return x1.reshape(bsz * s, D_MODEL), grp.reshape(bsz * s), co, hl


def _route_plan(g, tile):
    t = g.shape[0]
    onehot = (g[:, None] == jnp.arange(N_GROUPS, dtype=I32)[None, :]).astype(I32)
    csum = jnp.cumsum(onehot, axis=0)
    padded = (csum[-1] + tile - 1) // tile * tile
    ends = jnp.cumsum(padded)
    pos = jnp.sum((ends - padded)[None, :] * onehot, axis=1) + jnp.sum(csum * onehot, axis=1) - 1
    n_tiles = t // tile + N_GROUPS
    tile_start = jnp.arange(n_tiles, dtype=I32) * tile
    tile_group = jnp.minimum(jnp.sum((tile_start[:, None] >= ends[None, :]).astype(I32), axis=1), N_GROUPS - 1)
    return pos.astype(I32), tile_group.astype(I32), n_tiles


ROW_DMA_UNROLL = 8


def _start_rows(copy, n):
    for r in range(n):
        copy(r).start(priority=r % 2)


def _wait_rows(copy, n):
    def batch(i, c):
        for _ in range(ROW_DMA_UNROLL):
            copy.wait()
        return c

    lax.fori_loop(0, n // ROW_DMA_UNROLL, batch, 0)


def _moe_sort_kernel(pos_ref, x_ref, init_ref, xs_ref, sbuf, sems):
    del init_ref
    ts = x_ref.shape[0]
    i = pl.program_id(0)
    base = i * ts
    slot = i % 2
    sbuf[slot] = x_ref[...]

    def copy(r, s):
        return pltpu.make_async_copy(sbuf.at[s, pl.ds(r, 1)],
                                     xs_ref.at[pl.ds(pos_ref[base + r], 1)], sems.at[s])

    _start_rows(lambda r: copy(r, slot), ts)

    @pl.when(i > 0)
    def _():
        _wait_rows(copy(0, 1 - slot), ts)

    @pl.when(i == pl.num_programs(0) - 1)
    def _():
        _wait_rows(copy(0, slot), ts)


def _moe_expert_kernel(tg_ref, xs_ref, wr_ref, br_ref, wgu_ref, wd_ref, ys_ref):
    g = tg_ref[pl.program_id(0)]
    xb = xs_ref[...].astype(BF16)
    tm = xb.shape[0]
    lane = lax.broadcasted_iota(I32, (tm, LANES), 1)
    logits = _dot(xb, wr_ref[...]) + br_ref[...]
    neg_inf = -jnp.inf
    eg = jnp.exp(jnp.where(lane < N_GROUPS, logits, neg_inf)
                 - jnp.max(jnp.where(lane < N_GROUPS, logits, neg_inf), axis=1, keepdims=True))
    p_group = (jnp.sum(jnp.where(lane == g, eg, 0.0), axis=1, keepdims=True)
               / jnp.sum(eg, axis=1, keepdims=True))
    lo = N_GROUPS + EXPERTS_PER_GROUP * g
    el = jnp.where(jnp.logical_and(lane >= lo, lane < lo + EXPERTS_PER_GROUP), logits, neg_inf)
    v1 = jnp.max(el, axis=1, keepdims=True)
    i1 = jnp.min(jnp.where(el == v1, lane, LANES), axis=1, keepdims=True)
    el2 = jnp.where(lane == i1, neg_inf, el)
    v2 = jnp.max(el2, axis=1, keepdims=True)
    i2 = jnp.min(jnp.where(el2 == v2, lane, LANES), axis=1, keepdims=True)
    e2 = jnp.exp(v2 - v1)
    p1 = p_group / (1.0 + e2)
    p2 = e2 * p1
    acts = []
    for j in range(EXPERTS_PER_GROUP):
        gate = jnp.where(i1 == lo + j, p1, 0.0) + jnp.where(i2 == lo + j, p2, 0.0)
        hj = _dot(xb, wgu_ref[j])
        acts.append((gate * (jax.nn.silu(hj[:, :D_EXPERT]) * hj[:, D_EXPERT:])).astype(BF16))
    ys_ref[...] = _dot(jnp.concatenate(acts, axis=1), wd_ref[0])


def _moe_unsort_kernel(pos_ref, x_ref, ys_ref, g_ref, b_ref, o_ref, ybuf, sems):
    ts = x_ref.shape[0]
    i = pl.program_id(0)
    slot = i % 2

    def copy(step, r, s):
        return pltpu.make_async_copy(ys_ref.at[pl.ds(pos_ref[step * ts + r], 1)],
                                     ybuf.at[s, pl.ds(r, 1)], sems.at[s])

    @pl.when(i == 0)
    def _():
        _start_rows(lambda r: copy(0, r, 0), ts)

    @pl.when(i < pl.num_programs(0) - 1)
    def _():
        _start_rows(lambda r: copy(i + 1, r, 1 - slot), ts)

    _wait_rows(copy(i, 0, slot), ts)
    o_ref[...] = _layer_norm(DN_ALPHA * x_ref[...] + ybuf[slot], g_ref[...], b_ref[...])


def _moe(x, grp, wr, br, wgu, wd, g, b, ts, tile):
    t = x.shape[0]
    pos, tile_group, n_tiles = _route_plan(grp, tile)
    any_spec = pl.BlockSpec(memory_space=pl.ANY)
    xs = pl.pallas_call(
        _moe_sort_kernel,
        grid_spec=pltpu.PrefetchScalarGridSpec(
            num_scalar_prefetch=1, grid=(t // ts,),
            in_specs=[pl.BlockSpec((ts, D_MODEL), lambda i, pos: (i, 0)), any_spec],
            out_specs=any_spec,
            scratch_shapes=[pltpu.VMEM((2, ts, D_MODEL), F32), pltpu.SemaphoreType.DMA((2,))]),
        out_shape=jax.ShapeDtypeStruct((n_tiles * tile, D_MODEL), F32),
        input_output_aliases={2: 0},
        compiler_params=_params(("arbitrary",)),
        name="moe_sort",
    )(pos, x, jnp.zeros((n_tiles * tile, D_MODEL), F32))
    ys = pl.pallas_call(
        _moe_expert_kernel,
        grid_spec=pltpu.PrefetchScalarGridSpec(
            num_scalar_prefetch=1, grid=(n_tiles,),
            in_specs=[
                pl.BlockSpec((tile, D_MODEL), lambda i, tg: (i, 0)),
                pl.BlockSpec((D_MODEL, LANES), lambda i, tg: (0, 0)),
                pl.BlockSpec((1, LANES), lambda i, tg: (0, 0)),
                pl.BlockSpec((EXPERTS_PER_GROUP, D_MODEL, 2 * D_EXPERT), lambda i, tg: (tg[i], 0, 0)),
                pl.BlockSpec((1, EXPERTS_PER_GROUP * D_EXPERT, D_MODEL), lambda i, tg: (tg[i], 0, 0)),
            ],
            out_specs=pl.BlockSpec((tile, D_MODEL), lambda i, tg: (i, 0))),
        out_shape=jax.ShapeDtypeStruct((n_tiles * tile, D_MODEL), F32),
        compiler_params=_params(("arbitrary",)),
        name="moe_experts",
    )(tile_group, xs, wr, br, wgu, wd)
    return pl.pallas_call(
        _moe_unsort_kernel,
        grid_spec=pltpu.PrefetchScalarGridSpec(
            num_scalar_prefetch=1, grid=(t // ts,),
            in_specs=[pl.BlockSpec((ts, D_MODEL), lambda i, pos: (i, 0)), any_spec,
                      pl.BlockSpec((1, D_MODEL), lambda i, pos: (0, 0)),
                      pl.BlockSpec((1, D_MODEL), lambda i, pos: (0, 0))],
            out_specs=pl.BlockSpec((ts, D_MODEL), lambda i, pos: (i, 0)),
            scratch_shapes=[pltpu.VMEM((2, ts, D_MODEL), F32), pltpu.SemaphoreType.DMA((2,))]),
        out_shape=jax.ShapeDtypeStruct((t, D_MODEL), F32),
        compiler_params=_params(("arbitrary",)),
        name="moe_unsort",
    )(pos, x, ys, g, b)


def _rope_tables(pos):
    half = HEAD_DIM // 2
    freqs = ROPE_THETA ** (-jnp.arange(half, dtype=F32) / half)
    ang = pos.astype(F32)[:, None] * freqs[None, :]
    cos, sin = jnp.cos(ang), jnp.sin(ang)
    cosn = jnp.tile(cos, (1, LANES // half))
    sinn = jnp.tile(jnp.concatenate([-sin, sin], axis=1), (1, LANES // HEAD_DIM))
    return cosn, sinn, cos.T, sin.T


def _pad_axis(a, axis, size):
    pad = [(0, 0)] * a.ndim
    pad[axis] = (0, size - a.shape[axis])
    return jnp.pad(a, pad)


def kernel(x_prompt, x_sample, cache_k, cache_v, cache_k_idx, state_conv, state_h, w_in, conv_w, conv_b, w_rg_a, b_rg_a, w_rg_x, b_rg_x, lru_lambda, w_branch_attn, w_branch_rnn, w_out, ln1_g, ln1_b, w_router_group, b_router_group, w_router_expert, b_router_expert, w_gate_up, w_down, ln2_g, ln2_b):
    assert w_in.shape[0] == DEPTH == 1
    bp, sp, _ = x_prompt.shape
    bs, ss, _ = x_sample.shape
    past = cache_k.shape[2]
    ns_tok = bs * ss
    assert ns_tok == Q_BLOCK and sp % Q_BLOCK == 0 and sp % KEY_TILE == 0

    wq, wk, wv, wqi, wki, wwi, wxr, wgr, wga, wgb = jnp.split(w_in[0], SPLIT_POINTS, axis=1)
    wt = jnp.concatenate([wq.T, wqi.T, wv.T, _pad_axis(wwi.T, 0, 16)], axis=0).astype(BF16)
    wn = jnp.concatenate([wk, wv, _pad_axis(wki, 1, LANES)], axis=1).astype(BF16)
    row = lambda v: v.reshape(1, -1)
    rg = (wxr.astype(BF16), wgr.astype(BF16), conv_w[0], row(conv_b[0]),
          w_rg_a[0].astype(BF16), row(b_rg_a[0]), w_rg_x[0].astype(BF16), row(b_rg_x[0]),
          row(lru_lambda[0]))
    mg = (wga.astype(BF16), wgb.astype(BF16), w_branch_attn[0].astype(BF16),
          w_branch_rnn[0].astype(BF16), w_out[0].astype(BF16), row(ln1_g[0]), row(ln1_b[0]))
    wr = _pad_axis(jnp.concatenate([w_router_group[0], w_router_expert[0]], axis=1), 1, LANES).astype(BF16)
    br = _pad_axis(jnp.concatenate([b_router_group[0], b_router_expert[0]]).reshape(1, -1), 1, LANES)
    wgt = _pad_axis(w_router_group[0].T, 0, 16).astype(BF16)
    bgt = _pad_axis(b_router_group[0].reshape(-1, 1), 0, 16)
    wd = w_down[0].reshape(N_GROUPS, EXPERTS_PER_GROUP * D_EXPERT, D_MODEL).astype(BF16)
    mo = (wr, br, w_gate_up[0].astype(BF16), wd, row(ln2_g[0]), row(ln2_b[0]))

    pos_p = jnp.arange(sp, dtype=I32)
    qT, qiT, vT, wiT, k_p, kb_p, v_p, ki_p, kib_p = _proj(x_prompt, wt, wn, _rope_tables(pos_p), 512, KEY_TILE)
    lim_p = jnp.minimum((pos_p // CHUNK + 1) * CHUNK, sp).reshape(sp // PROMPT_Q_BLOCK, 1, PROMPT_Q_BLOCK)
    nk_p = (jnp.max(lim_p, axis=(1, 2)) + KEY_TILE - 1) // KEY_TILE
    attn_p = _attn(nk_p.astype(I32), lim_p, qT, qiT, wiT, kb_p, vT, kib_p)
    x1_p, grp_p, co_p, hl_p = _mixer(x_prompt, attn_p, jnp.zeros((bp, SUBLANES, D_RNN), F32),
                                     jnp.zeros((bp, 1, D_RNN), F32), *rg, *mg, wgt, bgt, ts=256)
    out_p = _moe(x1_p, grp_p, *mo, ts=256, tile=512)

    pos_s = past + (jnp.arange(ns_tok, dtype=I32) % ss)
    xs_flat = x_sample.reshape(1, ns_tok, D_MODEL)
    qT_s, qiT_s, _, wiT_s, k_s, kb_s, v_s, ki_s, kib_s = _proj(xs_flat, wt, wn, _rope_tables(pos_s), ns_tok, ns_tok)

    def per_batch_lanes(a):
        r = a.shape[1]
        return _pad_axis(a[0].reshape(r, bs, ss).transpose(1, 0, 2), 2, Q_BLOCK)

    l_all = past + ss
    lk = -(-l_all // KEY_TILE) * KEY_TILE
    k_all = jnp.concatenate([cache_k[0].reshape(bs, past, ATT_WIDTH).astype(BF16),
                             kb_s.reshape(bs, ss, ATT_WIDTH)], axis=1)
    v_all = jnp.concatenate([cache_v[0].reshape(bs, past, ATT_WIDTH),
                             v_s.reshape(bs, ss, ATT_WIDTH)], axis=1).astype(BF16)
    ki_all = jnp.concatenate([cache_k_idx[0].astype(BF16), kib_s.reshape(bs, ss, IDX_DIM)], axis=1)
    k_all, v_all, ki_all = (_pad_axis(a, 1, lk) for a in (k_all, v_all, ki_all))
    vT_all = v_all.reshape(bs, lk // KEY_TILE, KEY_TILE, ATT_WIDTH).transpose(0, 1, 3, 2)
    limit_s = min((past // CHUNK + 1) * CHUNK, l_all)
    assert (past + ss - 1) // CHUNK == past // CHUNK
    lim_s = jnp.where(jnp.arange(Q_BLOCK) < ss, limit_s, CHUNK).astype(I32).reshape(1, 1, Q_BLOCK)
    nk_s = jnp.full((1,), -(-limit_s // KEY_TILE), I32)
    attn_s = _attn(nk_s, lim_s, per_batch_lanes(qT_s), per_batch_lanes(qiT_s), per_batch_lanes(wiT_s),
                   k_all, vT_all, ki_all)[:, :ss]
    cs8 = jnp.pad(state_conv[0], ((0, 0), (SUBLANES - (CONV_WIDTH - 1), 0), (0, 0)))
    x1_s, grp_s, co_s, hl_s = _mixer(x_sample, attn_s, cs8, state_h[0][:, None, :], *rg, *mg, wgt, bgt, ts=ss)
    out_s = _moe(x1_s, grp_s, *mo, ts=ns_tok, tile=ns_tok)

    keep = CONV_WIDTH - 1
    return (out_p.reshape(bp, sp, D_MODEL), out_s.reshape(bs, ss, D_MODEL),
            k_p.reshape(1, bp, sp, N_HEADS, HEAD_DIM), v_p.reshape(1, bp, sp, N_HEADS, HEAD_DIM),
            ki_p[None], co_p[:, SUBLANES - keep:][None], hl_p[:, 0][None],
            k_s.reshape(1, bs, ss, N_HEADS, HEAD_DIM), v_s.reshape(1, bs, ss, N_HEADS, HEAD_DIM),
            ki_s.reshape(1, bs, ss, IDX_DIM), co_s[:, SUBLANES - keep:][None], hl_s[:, 0][None])
```

```python
import functools

import jax
import jax.numpy as jnp
import numpy as np
from jax import lax
from jax.experimental import pallas as pl
from jax.experimental.pallas import tpu as pltpu

F32 = jnp.float32
BF16 = jnp.bfloat16
I32 = jnp.int32

D_MODEL = 1024
N_HEADS = 8
HEAD_DIM = 64
ATT_WIDTH = N_HEADS * HEAD_DIM
N_IDX_HEADS = 8
IDX_DIM = 64
MAX_TOPK = 256
CHUNK = 64
D_RNN = D_MODEL
N_RNN_BLOCKS = 8
RNN_BLOCK = D_RNN // N_RNN_BLOCKS
CONV_WIDTH = 4
LRU_C = 8.0
N_GROUPS = 4
EXPERTS_PER_GROUP = 4
N_EXPERTS = N_GROUPS * EXPERTS_PER_GROUP
D_EXPERT = 256
ROPE_THETA = 10000.0
LN_EPS = 1e-5
DEPTH = 1
DN_ALPHA = (2.0 * DEPTH) ** 0.25
SPLITS = (ATT_WIDTH, ATT_WIDTH, ATT_WIDTH, N_IDX_HEADS * IDX_DIM, IDX_DIM, N_IDX_HEADS,
          D_RNN, D_RNN, D_MODEL, D_MODEL)
SPLIT_POINTS = tuple(int(v) for v in np.cumsum(SPLITS)[:-1])

LANES = 128
SUBLANES = 8
VMEM_LIMIT = 56 * 1024 * 1024
Q_BLOCK = LANES
PROMPT_Q_BLOCK = 256
KEY_TILE = 256
INT_MIN = np.int32(-2 ** 31)
NEG_BIG = -1e30
KEY_LOWEST_FINITE = np.int32(-2 ** 31 + 2 ** 23)
LOG2_E = 1.4426950408889634
LATE_BITS = 4
WT_ROWS = 3 * ATT_WIDTH + 16
WN_COLS = 2 * ATT_WIDTH + LANES


def _params(semantics):
    return pltpu.CompilerParams(dimension_semantics=semantics, vmem_limit_bytes=VMEM_LIMIT)


def _dot(a, b):
    return jnp.dot(a, b, preferred_element_type=F32)


def _layer_norm(x, g, b):
    mu = jnp.mean(x, axis=-1, keepdims=True)
    xc = x - mu
    var = jnp.mean(xc * xc, axis=-1, keepdims=True)
    return xc * lax.rsqrt(var + LN_EPS) * g + b


def _proj_kernel(x_ref, wt_ref, wn_ref, cosn_ref, sinn_ref, cost_ref, sint_ref,
                 qT_ref, qiT_ref, vT_ref, wiT_ref, k_ref, kb_ref, v_ref, ki_ref, kib_ref):
    tm = x_ref.shape[1]
    kt_out = vT_ref.shape[3]
    xb = x_ref[0].astype(BF16)
    zt = lax.dot_general(wt_ref[...], xb, (((1,), (1,)), ((), ())), preferred_element_type=F32)
    zn = _dot(xb, wn_ref[...])
    cost = cost_ref[...]
    sint = sint_ref[...]
    half = HEAD_DIM // 2
    for base, ref, scale in ((0, qT_ref, HEAD_DIM ** -0.5 * LOG2_E), (ATT_WIDTH, qiT_ref, IDX_DIM ** -0.5)):
        for h in range(N_HEADS):
            r = h * HEAD_DIM
            x1 = zt[base + r:base + r + half]
            x2 = zt[base + r + half:base + r + HEAD_DIM]
            ref[0, r:r + half, :] = ((x1 * cost - x2 * sint) * scale).astype(BF16)
            ref[0, r + half:r + HEAD_DIM, :] = ((x1 * sint + x2 * cost) * scale).astype(BF16)
    for c in range(tm // kt_out):
        vT_ref[0, c] = zt[2 * ATT_WIDTH:3 * ATT_WIDTH, c * kt_out:(c + 1) * kt_out].astype(BF16)
    wiT_ref[0] = zt[3 * ATT_WIDTH:3 * ATT_WIDTH + N_IDX_HEADS] * (N_IDX_HEADS ** -0.5)

    cosn = cosn_ref[...]
    sinn = sinn_ref[...]
    lane = lax.broadcasted_iota(I32, (tm, LANES), 1)
    first_half = (lane & half) == 0

    def rope_n(z):
        partner = jnp.where(first_half, pltpu.roll(z, LANES - half, 1), pltpu.roll(z, half, 1))
        return z * cosn + partner * sinn

    for g in range(ATT_WIDTH // LANES):
        kg = rope_n(zn[:, g * LANES:(g + 1) * LANES])
        k_ref[0, :, g * LANES:(g + 1) * LANES] = kg
        kb_ref[0, :, g * LANES:(g + 1) * LANES] = kg.astype(BF16)
    v_ref[0] = zn[:, ATT_WIDTH:2 * ATT_WIDTH]
    kig = rope_n(zn[:, 2 * ATT_WIDTH:2 * ATT_WIDTH + LANES])
    ki_ref[0] = kig[:, :IDX_DIM]
    kib_ref[0] = kig[:, :IDX_DIM].astype(BF16)


def _proj(x, wt, wn, tables, tm, kt_out):
    b, s, _ = x.shape
    cosn, sinn, cost, sint = tables
    ns = s // tm
    out_shape = (
        jax.ShapeDtypeStruct((b, ATT_WIDTH, s), BF16),
        jax.ShapeDtypeStruct((b, ATT_WIDTH, s), BF16),
        jax.ShapeDtypeStruct((b, s // kt_out, ATT_WIDTH, kt_out), BF16),
        jax.ShapeDtypeStruct((b, N_IDX_HEADS, s), F32),
        jax.ShapeDtypeStruct((b, s, ATT_WIDTH), F32),
        jax.ShapeDtypeStruct((b, s, ATT_WIDTH), BF16),
        jax.ShapeDtypeStruct((b, s, ATT_WIDTH), F32),
        jax.ShapeDtypeStruct((b, s, IDX_DIM), F32),
        jax.ShapeDtypeStruct((b, s, IDX_DIM), BF16),
    )
    tok = lambda w: pl.BlockSpec((1, tm, w), lambda i, j: (i, j, 0))
    feat = lambda r: pl.BlockSpec((1, r, tm), lambda i, j: (i, 0, j))
    return pl.pallas_call(
        _proj_kernel,
        grid=(b, ns),
        in_specs=[
            tok(D_MODEL),
            pl.BlockSpec((WT_ROWS, D_MODEL), lambda i, j: (0, 0)),
            pl.BlockSpec((D_MODEL, WN_COLS), lambda i, j: (0, 0)),
            pl.BlockSpec((tm, LANES), lambda i, j: (j, 0)),
            pl.BlockSpec((tm, LANES), lambda i, j: (j, 0)),
            pl.BlockSpec((HEAD_DIM // 2, tm), lambda i, j: (0, j)),
            pl.BlockSpec((HEAD_DIM // 2, tm), lambda i, j: (0, j)),
        ],
        out_specs=(
            feat(ATT_WIDTH), feat(ATT_WIDTH),
            pl.BlockSpec((1, tm // kt_out, ATT_WIDTH, kt_out), lambda i, j: (i, j, 0, 0)),
            feat(N_IDX_HEADS),
            tok(ATT_WIDTH), tok(ATT_WIDTH), tok(ATT_WIDTH), tok(IDX_DIM), tok(IDX_DIM),
        ),
        out_shape=out_shape,
        compiler_params=_params(("parallel", "parallel")),
        name="proj",
    )(x, wt, wn, cosn, sinn, cost, sint)


def _order_key_to_f32(key):
    return lax.bitcast_convert_type(jnp.where(key < 0, key ^ 0x7FFFFFFF, key), F32)


def _attn_kernel(nk_ref, lim_ref, qT_ref, qiT_ref, wiT_ref, k_ref, vT_ref, ki_ref, o_ref,
                 score_scr, bias_scr, logit_scr, acc_scr):
    nk = nk_ref[pl.program_id(1)]
    lim = lim_ref[0]
    w = wiT_ref[0]
    qi = qiT_ref[0]
    q = qT_ref[0]
    qb = q.shape[1]
    groups = KEY_TILE // SUBLANES

    def rows(kt):
        return pl.ds(pl.multiple_of(kt * KEY_TILE, KEY_TILE), KEY_TILE)

    def fold(x, op):
        return op(op(x.reshape(4, groups // 4, SUBLANES, qb), axis=1), axis=0)

    def tile_loop(body, init):
        if qb > LANES:
            return lax.fori_loop(0, nk, body, init)
        c = lax.fori_loop(0, nk // 2, lambda i, c: body(2 * i + 1, body(2 * i, c)), init)
        return lax.cond(nk % 2 == 1, lambda c: body(nk - 1, c), lambda c: c, c)

    qi_pairs = [jnp.concatenate([qi[(2 * p) * IDX_DIM:(2 * p + 1) * IDX_DIM],
                                 qi[(2 * p + 1) * IDX_DIM:(2 * p + 2) * IDX_DIM]], axis=1)
                for p in range(N_IDX_HEADS // 2)]
    row_iota = lax.broadcasted_iota(I32, (KEY_TILE, qb), 0)

    def score_tile(kt, carry):
        ki_t = ki_ref[0, rows(kt), :]
        acc = jnp.zeros((KEY_TILE, qb), F32)
        for p in range(N_IDX_HEADS // 2):
            d = _dot(ki_t, qi_pairs[p])
            acc = acc + jnp.maximum(d[:, :qb], 0.0) * w[2 * p:2 * p + 1]
            acc = acc + jnp.maximum(d[:, qb:], 0.0) * w[2 * p + 1:2 * p + 2]
        score_scr[rows(kt), :] = jnp.where(row_iota + kt * KEY_TILE < lim, acc, -jnp.inf)
        return carry

    tile_loop(score_tile, 0)

    def count(pred):
        def body(kt, c):
            return c + fold(jnp.where(pred(score_scr[rows(kt), :]), 1, 0), jnp.sum)

        return jnp.sum(tile_loop(body, jnp.zeros((SUBLANES, qb), I32)), axis=0, keepdims=True)

    def bisect_for(n_tiles):
        def bisect(i, carry):
            t, cnt_t = carry
            cand = t + lax.shift_left(jnp.int32(1), 31 - i)
            cand_f = _order_key_to_f32(cand)
            c = jnp.zeros((SUBLANES, qb), I32)
            for kt in range(n_tiles):
                c = c + fold(jnp.where(score_scr[kt * KEY_TILE:(kt + 1) * KEY_TILE, :] >= cand_f, 1, 0), jnp.sum)
            cnt = jnp.sum(c, axis=0, keepdims=True)
            ok = cnt >= MAX_TOPK
            return jnp.where(ok, cand, t), jnp.where(ok, cnt, cnt_t)

        def run():
            t0 = jnp.full((1, qb), INT_MIN, I32)
            c0 = jnp.full((1, qb), n_tiles * KEY_TILE, I32)
            t, c = lax.fori_loop(0, 32 - LATE_BITS, bisect, (t0, c0))
            settled = jnp.logical_or(c == MAX_TOPK, lim <= MAX_TOPK)
            all_settled = jnp.min(jnp.where(settled, 1.0, 0.0)) > 0.0
            return lax.cond(all_settled, lambda tc: tc,
                            lambda tc: lax.fori_loop(32 - LATE_BITS, 32, bisect, tc), (t, c))
        return run

    max_tiles = score_scr.shape[0] // KEY_TILE
    thr, cnt_thr = lax.switch(nk - 1, [bisect_for(n) for n in range(1, max_tiles + 1)])
    found = thr >= KEY_LOWEST_FINITE
    thr_f = jnp.where(found, _order_key_to_f32(thr), jnp.finfo(F32).min)

    def bias_tile(kt, carry):
        bias_scr[rows(kt), :] = jnp.where(score_scr[rows(kt), :] >= thr_f, 0.0, NEG_BIG)
        return carry

    tile_loop(bias_tile, 0)

    tie = jnp.logical_and(cnt_thr > MAX_TOPK, found)
    any_tie = jnp.max(jnp.where(tie, 1.0, 0.0)) > 0.0

    @pl.when(any_tie)
    def _():
        need = (MAX_TOPK - count(lambda s: s > thr_f)).astype(F32)
        tri = (lax.broadcasted_iota(I32, (KEY_TILE, KEY_TILE), 1)
               <= lax.broadcasted_iota(I32, (KEY_TILE, KEY_TILE), 0))
        tri = jnp.where(tri, 1.0, 0.0).astype(BF16)

        def tie_tile(kt, before):
            s = score_scr[rows(kt), :]
            eq = s == thr_f
            rank = _dot(tri, jnp.where(eq, 1.0, 0.0).astype(BF16)) + before
            sel = jnp.logical_or(s > thr_f, jnp.logical_and(eq, rank <= need))
            bias_scr[rows(kt), :] = jnp.where(sel, 0.0, NEG_BIG)
            return rank[KEY_TILE - 1:KEY_TILE, :]

        lax.fori_loop(0, nk, tie_tile, jnp.zeros((1, qb), F32))

    zeros_half = jnp.zeros((HEAD_DIM, qb), BF16)
    rhs = [jnp.concatenate(
        [jnp.concatenate([q[(2 * p) * HEAD_DIM:(2 * p + 1) * HEAD_DIM], zeros_half], axis=0),
         jnp.concatenate([zeros_half, q[(2 * p + 1) * HEAD_DIM:(2 * p + 2) * HEAD_DIM]], axis=0)], axis=1)
        for p in range(N_HEADS // 2)]

    def logits_tile(kt, m_parts):
        bias = bias_scr[rows(kt), :]
        out = []
        for p in range(N_HEADS // 2):
            s2 = _dot(k_ref[0, rows(kt), p * LANES:(p + 1) * LANES], rhs[p])
            for e in range(2):
                h = 2 * p + e
                s = s2[:, e * qb:(e + 1) * qb] + bias
                logit_scr[h, rows(kt), :] = s
                out.append(jnp.maximum(m_parts[h], fold(s, jnp.max)))
        return tuple(out)

    m_parts = tile_loop(logits_tile, tuple(jnp.full((SUBLANES, qb), NEG_BIG, F32) for _ in range(N_HEADS)))
    m = [jnp.max(mp, axis=0, keepdims=True) for mp in m_parts]
    acc_scr[...] = jnp.zeros_like(acc_scr)

    def pv_tile(kt, l_parts):
        out = []
        for h in range(N_HEADS):
            p = jnp.exp2(logit_scr[h, rows(kt), :] - m[h])
            out.append(l_parts[h] + fold(p, jnp.sum))
            v_t = vT_ref[0, kt, h * HEAD_DIM:(h + 1) * HEAD_DIM, :]
            acc_scr[h * HEAD_DIM:(h + 1) * HEAD_DIM, :] += _dot(v_t, p.astype(BF16))
        return tuple(out)

    l_parts = tile_loop(pv_tile, tuple(jnp.zeros((SUBLANES, qb), F32) for _ in range(N_HEADS)))
    outs = [acc_scr[h * HEAD_DIM:(h + 1) * HEAD_DIM, :] / jnp.sum(l_parts[h], axis=0, keepdims=True)
            for h in range(N_HEADS)]
    o_ref[0] = jnp.concatenate(outs, axis=0).T.astype(BF16)


def _attn(nk, limits, qT, qiT, wiT, kb, vT, kib):
    b, _, sq = qT.shape
    qb = limits.shape[-1]
    nq = sq // qb
    lk = kb.shape[1]
    grid_spec = pltpu.PrefetchScalarGridSpec(
        num_scalar_prefetch=1,
        grid=(b, nq),
        in_specs=[
            pl.BlockSpec((1, 1, qb), lambda i, j, nk: (j, 0, 0)),
            pl.BlockSpec((1, ATT_WIDTH, qb), lambda i, j, nk: (i, 0, j)),
            pl.BlockSpec((1, ATT_WIDTH, qb), lambda i, j, nk: (i, 0, j)),
            pl.BlockSpec((1, N_IDX_HEADS, qb), lambda i, j, nk: (i, 0, j)),
            pl.BlockSpec((1, lk, ATT_WIDTH), lambda i, j, nk: (i, 0, 0)),
            pl.BlockSpec((1, lk // KEY_TILE, ATT_WIDTH, KEY_TILE), lambda i, j, nk: (i, 0, 0, 0)),
            pl.BlockSpec((1, lk, IDX_DIM), lambda i, j, nk: (i, 0, 0)),
        ],
        out_specs=pl.BlockSpec((1, qb, ATT_WIDTH), lambda i, j, nk: (i, j, 0)),
        scratch_shapes=[pltpu.VMEM((lk, qb), F32), pltpu.VMEM((lk, qb), F32),
                        pltpu.VMEM((N_HEADS, lk, qb), F32), pltpu.VMEM((ATT_WIDTH, qb), F32)],
    )
    return pl.pallas_call(
        _attn_kernel,
        grid_spec=grid_spec,
        out_shape=jax.ShapeDtypeStruct((b, sq, ATT_WIDTH), BF16),
        compiler_params=_params(("parallel", "arbitrary")),
        name="attn",
    )(nk, limits, qT, qiT, wiT, kb, vT, kib)


def _mixer_kernel(x_ref, attn_ref, cs_ref, h0_ref, wxr_ref, wgr_ref, cw_ref, cb_ref, wa_ref, ba_ref,
                  wx_ref, bx_ref, lam_ref, wga_ref, wgb_ref, wba_ref, wbr_ref, wout_ref, g_ref, b_ref,
                  wgt_ref, bgt_ref, o_ref, grp_ref, co_ref, hl_ref,
                  xbuf, a_scr, u_scr, h_scr, hc_scr):
    ts = x_ref.shape[1]

    @pl.when(pl.program_id(1) == 0)
    def _():
        xbuf[0:SUBLANES, :] = cs_ref[0]
        hc_scr[...] = h0_ref[0]

    x = x_ref[0]
    xb = x.astype(BF16)
    xr = _dot(xb, wxr_ref[...])
    gr = _dot(xb, wgr_ref[...])
    xbuf[SUBLANES:SUBLANES + ts, :] = xr
    cw = cw_ref[...]
    xc = cb_ref[...] + xbuf[5:5 + ts, :] * cw[0:1]
    xc = xc + xbuf[6:6 + ts, :] * cw[1:2]
    xc = xc + xbuf[7:7 + ts, :] * cw[2:3]
    xc = xc + xr * cw[3:4]
    tail = xbuf[ts:ts + SUBLANES, :]
    xbuf[0:SUBLANES, :] = tail
    co_ref[0] = tail

    xcb = xc.astype(BF16)

    def gate(w_ref, b_ref):
        parts = [_dot(xcb[:, n * RNN_BLOCK:(n + 1) * RNN_BLOCK], w_ref[n]) for n in range(N_RNN_BLOCKS)]
        return jax.nn.sigmoid(jnp.concatenate(parts, axis=1) + b_ref[...])

    r = gate(wa_ref, ba_ref)
    i = gate(wx_ref, bx_ref)
    nl = -lam_ref[...]
    softplus = jnp.maximum(nl, 0.0) + jnp.log1p(jnp.exp(-jnp.abs(nl)))
    log_a = (-LRU_C * r) * softplus
    a_scr[...] = jnp.exp(log_a)
    th = jnp.tanh(log_a)
    u_scr[...] = jnp.exp2(0.5 * jnp.log2(-2.0 * th / (1.0 - th))) * (i * xc)

    row = lax.broadcasted_iota(I32, (SUBLANES, D_RNN), 0)
    h_prev = hc_scr[...]
    for g in range(ts // SUBLANES):
        rs = slice(g * SUBLANES, (g + 1) * SUBLANES)
        a = a_scr[rs, :]
        b = u_scr[rs, :]
        for d in (1, 2, 4):
            a_sh = jnp.where(row >= d, pltpu.roll(a, d, 0), 1.0)
            b_sh = jnp.where(row >= d, pltpu.roll(b, d, 0), 0.0)
            b = a * b_sh + b
            a = a * a_sh
        h = b + a * h_prev
        h_scr[rs, :] = h
        h_prev = h[SUBLANES - 1:SUBLANES, :]
    hc_scr[...] = h_prev
    hl_ref[0] = h_prev
    y = (h_scr[...] * jax.nn.gelu(gr)).astype(BF16)

    ga = _dot(xb, wga_ref[...])
    gb = _dot(xb, wgb_ref[...])
    merged = (jax.nn.sigmoid(ga) * _dot(attn_ref[0], wba_ref[...])
              + jax.nn.sigmoid(gb) * _dot(y, wbr_ref[...]))
    mix = _dot(merged.astype(BF16), wout_ref[...])
    x1 = _layer_norm(DN_ALPHA * x + mix, g_ref[...], b_ref[...])
    o_ref[0] = x1
    logits = lax.dot_general(wgt_ref[...], x1.astype(BF16), (((1,), (1,)), ((), ())),
                             preferred_element_type=F32) + bgt_ref[...]
    best = logits[0:1]
    g_sel = jnp.zeros(best.shape, I32)
    for k in range(1, N_GROUPS):
        g_sel = jnp.where(logits[k:k + 1] > best, k, g_sel)
        best = jnp.maximum(best, logits[k:k + 1])
    grp_ref[0] = g_sel


def _mixer(x, attn, conv_state8, h0, wxr, wgr, conv_w, conv_b, wa, ba, wx, bx, lam,
           wga, wgb, wba, wbr, wout, g, b, wgt, bgt, ts):
    bsz, s, _ = x.shape
    ns = s // ts
    once = pl.Buffered(1)
    const = lambda a: pl.BlockSpec(a.shape, lambda i, j, nd=a.ndim: (0,) * nd, pipeline_mode=once)
    per_b = lambda r: pl.BlockSpec((1, r, D_RNN), lambda i, j: (i, 0, 0))
    tok = lambda w: pl.BlockSpec((1, ts, w), lambda i, j: (i, j, 0))
    weights = (wxr, wgr, conv_w, conv_b, wa, ba, wx, bx, lam, wga, wgb, wba, wbr, wout, g, b, wgt, bgt)
    x1, grp, co, hl = pl.pallas_call(
        _mixer_kernel,
        grid=(bsz, ns),
        in_specs=[tok(D_MODEL), tok(ATT_WIDTH), per_b(SUBLANES), per_b(1)] + [const(a) for a in weights],
        out_specs=(tok(D_MODEL), pl.BlockSpec((1, 1, ts), lambda i, j: (i * ns + j, 0, 0)),
                   per_b(SUBLANES), per_b(1)),
        out_shape=(
            jax.ShapeDtypeStruct((bsz, s, D_MODEL), F32),
            jax.ShapeDtypeStruct((bsz * ns, 1, ts), I32),
            jax.ShapeDtypeStruct((bsz, SUBLANES, D_RNN), F32),
            jax.ShapeDtypeStruct((bsz, 1, D_RNN), F32),
        ),
        scratch_shapes=[
            pltpu.VMEM((ts + SUBLANES, D_RNN), F32),
            pltpu.VMEM((ts, D_RNN), F32), pltpu.VMEM((ts, D_RNN), F32), pltpu.VMEM((ts, D_RNN), F32),
            pltpu.VMEM((1, D_RNN), F32),
        ],
        compiler_params=_params(("parallel", "arbitrary")),
        name="mixer",
    )(x, attn, conv_state8, h0, *weights)
    return x1.reshape(bsz * s, D_MODEL), grp.reshape(bsz * s), co, hl


def _route_plan(g, tile):
    t = g.shape[0]
    onehot = (g[:, None] == jnp.arange(N_GROUPS, dtype=I32)[None, :]).astype(I32)
    csum = jnp.cumsum(onehot, axis=0)
    padded = (csum[-1] + tile - 1) // tile * tile
    ends = jnp.cumsum(padded)
    pos = jnp.sum((ends - padded)[None, :] * onehot, axis=1) + jnp.sum(csum * onehot, axis=1) - 1
    n_tiles = t // tile + N_GROUPS
    tile_start = jnp.arange(n_tiles, dtype=I32) * tile
    tile_group = jnp.minimum(jnp.sum((tile_start[:, None] >= ends[None, :]).astype(I32), axis=1), N_GROUPS - 1)
    last_of_group = jnp.where(padded > 0, ends - tile, -1)
    spare = ends[-1] + jnp.arange(N_GROUPS, dtype=I32) * tile
    fill = jnp.concatenate([last_of_group, jnp.where(spare < n_tiles * tile, spare, -1)])
    return pos.astype(I32), tile_group.astype(I32), fill.astype(I32), n_tiles


ROW_DMA_UNROLL = 8


def _start_rows(copy, n):
    for r in range(n):
        copy(r).start(priority=r % 2)


def _wait_rows(copy, n):
    def batch(i, c):
        for _ in range(ROW_DMA_UNROLL):
            copy.wait()
        return c

    lax.fori_loop(0, n // ROW_DMA_UNROLL, batch, 0)


def _moe_sort_kernel(pos_ref, fill_ref, x_ref, xs_ref, sbuf, zbuf, sems, zsem):
    ts = x_ref.shape[0]
    tile = zbuf.shape[0]
    i = pl.program_id(0)
    base = i * ts
    slot = i % 2

    @pl.when(i == 0)
    def _():
        zbuf[...] = jnp.zeros_like(zbuf)
        for e in range(2 * N_GROUPS):
            @pl.when(fill_ref[e] >= 0)
            def _(e=e):
                start = pl.multiple_of(fill_ref[e], SUBLANES)
                zero_fill = pltpu.make_async_copy(zbuf, xs_ref.at[pl.ds(start, tile)], zsem)
                zero_fill.start()
                zero_fill.wait()

    sbuf[slot] = x_ref[...]

    def copy(r, s):
        return pltpu.make_async_copy(sbuf.at[s, pl.ds(r, 1)],
                                     xs_ref.at[pl.ds(pos_ref[base + r], 1)], sems.at[s])

    _start_rows(lambda r: copy(r, slot), ts)

    @pl.when(i > 0)
    def _():
        _wait_rows(copy(0, 1 - slot), ts)

    @pl.when(i == pl.num_programs(0) - 1)
    def _():
        _wait_rows(copy(0, slot), ts)


def _moe_expert_kernel(tg_ref, xs_ref, wr_ref, br_ref, wgu_ref, wd_ref, ys_ref):
    g = tg_ref[pl.program_id(0)]
    xb = xs_ref[...].astype(BF16)
    tm = xb.shape[0]
    lane = lax.broadcasted_iota(I32, (tm, LANES), 1)
    logits = _dot(xb, wr_ref[...]) + br_ref[...]
    neg_inf = -jnp.inf
    eg = jnp.exp(jnp.where(lane < N_GROUPS, logits, neg_inf)
                 - jnp.max(jnp.where(lane < N_GROUPS, logits, neg_inf), axis=1, keepdims=True))
    p_group = (jnp.sum(jnp.where(lane == g, eg, 0.0), axis=1, keepdims=True)
               / jnp.sum(eg, axis=1, keepdims=True))
    lo = N_GROUPS + EXPERTS_PER_GROUP * g
    el = jnp.where(jnp.logical_and(lane >= lo, lane < lo + EXPERTS_PER_GROUP), logits, neg_inf)
    v1 = jnp.max(el, axis=1, keepdims=True)
    i1 = jnp.min(jnp.where(el == v1, lane, LANES), axis=1, keepdims=True)
    el2 = jnp.where(lane == i1, neg_inf, el)
    v2 = jnp.max(el2, axis=1, keepdims=True)
    i2 = jnp.min(jnp.where(el2 == v2, lane, LANES), axis=1, keepdims=True)
    e2 = jnp.exp(v2 - v1)
    p1 = p_group / (1.0 + e2)
    p2 = e2 * p1
    acts = []
    for j in range(EXPERTS_PER_GROUP):
        gate = jnp.where(i1 == lo + j, p1, 0.0) + jnp.where(i2 == lo + j, p2, 0.0)
        hj = _dot(xb, wgu_ref[j].astype(BF16))
        acts.append((gate * (jax.nn.silu(hj[:, :D_EXPERT]) * hj[:, D_EXPERT:])).astype(BF16))
    ys_ref[...] = _dot(jnp.concatenate(acts, axis=1), wd_ref[0].astype(BF16))


def _moe_unsort_kernel(pos_ref, x_ref, ys_ref, g_ref, b_ref, o_ref, ybuf, sems):
    ts = x_ref.shape[0]
    i = pl.program_id(0)
    slot = i % 2

    def copy(step, r, s):
        return pltpu.make_async_copy(ys_ref.at[pl.ds(pos_ref[step * ts + r], 1)],
                                     ybuf.at[s, pl.ds(r, 1)], sems.at[s])

    @pl.when(i == 0)
    def _():
        _start_rows(lambda r: copy(0, r, 0), ts)

    @pl.when(i < pl.num_programs(0) - 1)
    def _():
        _start_rows(lambda r: copy(i + 1, r, 1 - slot), ts)

    _wait_rows(copy(i, 0, slot), ts)
    o_ref[...] = _layer_norm(DN_ALPHA * x_ref[...] + ybuf[slot], g_ref[...], b_ref[...])


def _moe(x, grp, wr, br, wgu, wd, g, b, ts, tile):
    t = x.shape[0]
    pos, tile_group, fill, n_tiles = _route_plan(grp, tile)
    any_spec = pl.BlockSpec(memory_space=pl.ANY)
    xs = pl.pallas_call(
        _moe_sort_kernel,
        grid_spec=pltpu.PrefetchScalarGridSpec(
            num_scalar_prefetch=2, grid=(t // ts,),
            in_specs=[pl.BlockSpec((ts, D_MODEL), lambda i, pos, fill: (i, 0))],
            out_specs=any_spec,
            scratch_shapes=[pltpu.VMEM((2, ts, D_MODEL), F32), pltpu.VMEM((tile, D_MODEL), F32),
                            pltpu.SemaphoreType.DMA((2,)), pltpu.SemaphoreType.DMA(())]),
        out_shape=jax.ShapeDtypeStruct((n_tiles * tile, D_MODEL), F32),
        compiler_params=_params(("arbitrary",)),
        name="moe_sort",
    )(pos, fill, x)
    ys = pl.pallas_call(
        _moe_expert_kernel,
        grid_spec=pltpu.PrefetchScalarGridSpec(
            num_scalar_prefetch=1, grid=(n_tiles,),
            in_specs=[
                pl.BlockSpec((tile, D_MODEL), lambda i, tg: (i, 0)),
                pl.BlockSpec((D_MODEL, LANES), lambda i, tg: (0, 0)),
                pl.BlockSpec((1, LANES), lambda i, tg: (0, 0)),
                pl.BlockSpec((EXPERTS_PER_GROUP, D_MODEL, 2 * D_EXPERT), lambda i, tg: (tg[i], 0, 0)),
                pl.BlockSpec((1, EXPERTS_PER_GROUP * D_EXPERT, D_MODEL), lambda i, tg: (tg[i], 0, 0)),
            ],
            out_specs=pl.BlockSpec((tile, D_MODEL), lambda i, tg: (i, 0))),
        out_shape=jax.ShapeDtypeStruct((n_tiles * tile, D_MODEL), F32),
        compiler_params=_params(("arbitrary",)),
        name="moe_experts",
    )(tile_group, xs, wr, br, wgu, wd)
    return pl.pallas_call(
        _moe_unsort_kernel,
        grid_spec=pltpu.PrefetchScalarGridSpec(
            num_scalar_prefetch=1, grid=(t // ts,),
            in_specs=[pl.BlockSpec((ts, D_MODEL), lambda i, pos: (i, 0)), any_spec,
                      pl.BlockSpec((1, D_MODEL), lambda i, pos: (0, 0)),
                      pl.BlockSpec((1, D_MODEL), lambda i, pos: (0, 0))],
            out_specs=pl.BlockSpec((ts, D_MODEL), lambda i, pos: (i, 0)),
            scratch_shapes=[pltpu.VMEM((2, ts, D_MODEL), F32), pltpu.SemaphoreType.DMA((2,))]),
        out_shape=jax.ShapeDtypeStruct((t, D_MODEL), F32),
        compiler_params=_params(("arbitrary",)),
        name="moe_unsort",
    )(pos, x, ys, g, b)


def _rope_tables(pos):
    half = HEAD_DIM // 2
    freqs = ROPE_THETA ** (-jnp.arange(half, dtype=F32) / half)
    ang = pos.astype(F32)[:, None] * freqs[None, :]
    cos, sin = jnp.cos(ang), jnp.sin(ang)
    cosn = jnp.tile(cos, (1, LANES // half))
    sinn = jnp.tile(jnp.concatenate([-sin, sin], axis=1), (1, LANES // HEAD_DIM))
    return cosn, sinn, cos.T, sin.T


def _pad_axis(a, axis, size):
    pad = [(0, 0)] * a.ndim
    pad[axis] = (0, size - a.shape[axis])
    return jnp.pad(a, pad)


def kernel(x_prompt, x_sample, cache_k, cache_v, cache_k_idx, state_conv, state_h, w_in, conv_w, conv_b, w_rg_a, b_rg_a, w_rg_x, b_rg_x, lru_lambda, w_branch_attn, w_branch_rnn, w_out, ln1_g, ln1_b, w_router_group, b_router_group, w_router_expert, b_router_expert, w_gate_up, w_down, ln2_g, ln2_b):
    assert w_in.shape[0] == DEPTH == 1
    bp, sp, _ = x_prompt.shape
    bs, ss, _ = x_sample.shape
    past = cache_k.shape[2]
    ns_tok = bs * ss
    assert ns_tok == Q_BLOCK and sp % Q_BLOCK == 0 and sp % KEY_TILE == 0

    wq, wk, wv, wqi, wki, wwi, wxr, wgr, wga, wgb = jnp.split(w_in[0], SPLIT_POINTS, axis=1)
    wt = jnp.concatenate([wq.T, wqi.T, wv.T, _pad_axis(wwi.T, 0, 16)], axis=0).astype(BF16)
    wn = jnp.concatenate([wk, wv, _pad_axis(wki, 1, LANES)], axis=1).astype(BF16)
    row = lambda v: v.reshape(1, -1)
    rg = (wxr.astype(BF16), wgr.astype(BF16), conv_w[0], row(conv_b[0]),
          w_rg_a[0].astype(BF16), row(b_rg_a[0]), w_rg_x[0].astype(BF16), row(b_rg_x[0]),
          row(lru_lambda[0]))
    mg = (wga.astype(BF16), wgb.astype(BF16), w_branch_attn[0].astype(BF16),
          w_branch_rnn[0].astype(BF16), w_out[0].astype(BF16), row(ln1_g[0]), row(ln1_b[0]))
    wr = _pad_axis(jnp.concatenate([w_router_group[0], w_router_expert[0]], axis=1), 1, LANES).astype(BF16)
    br = _pad_axis(jnp.concatenate([b_router_group[0], b_router_expert[0]]).reshape(1, -1), 1, LANES)
    wgt = _pad_axis(w_router_group[0].T, 0, 16).astype(BF16)
    bgt = _pad_axis(b_router_group[0].reshape(-1, 1), 0, 16)
    wd = w_down[0].reshape(N_GROUPS, EXPERTS_PER_GROUP * D_EXPERT, D_MODEL)
    mo = (wr, br, w_gate_up[0], wd, row(ln2_g[0]), row(ln2_b[0]))

    pos_p = jnp.arange(sp, dtype=I32)
    qT, qiT, vT, wiT, k_p, kb_p, v_p, ki_p, kib_p = _proj(x_prompt, wt, wn, _rope_tables(pos_p), 512, KEY_TILE)
    lim_p = jnp.minimum((pos_p // CHUNK + 1) * CHUNK, sp).reshape(sp // PROMPT_Q_BLOCK, 1, PROMPT_Q_BLOCK)
    nk_p = (jnp.max(lim_p, axis=(1, 2)) + KEY_TILE - 1) // KEY_TILE
    attn_p = _attn(nk_p.astype(I32), lim_p, qT, qiT, wiT, kb_p, vT, kib_p)
    x1_p, grp_p, co_p, hl_p = _mixer(x_prompt, attn_p, jnp.zeros((bp, SUBLANES, D_RNN), F32),
                                     jnp.zeros((bp, 1, D_RNN), F32), *rg, *mg, wgt, bgt, ts=256)
    out_p = _moe(x1_p, grp_p, *mo, ts=256, tile=512)

    pos_s = past + (jnp.arange(ns_tok, dtype=I32) % ss)
    xs_flat = x_sample.reshape(1, ns_tok, D_MODEL)
    qT_s, qiT_s, _, wiT_s, k_s, kb_s, v_s, ki_s, kib_s = _proj(xs_flat, wt, wn, _rope_tables(pos_s), ns_tok, ns_tok)

    def per_batch_lanes(a):
        r = a.shape[1]
        return _pad_axis(a[0].reshape(r, bs, ss).transpose(1, 0, 2), 2, Q_BLOCK)

    l_all = past + ss
    lk = -(-l_all // KEY_TILE) * KEY_TILE
    k_all = jnp.concatenate([cache_k[0].reshape(bs, past, ATT_WIDTH).astype(BF16),
                             kb_s.reshape(bs, ss, ATT_WIDTH)], axis=1)
    v_all = jnp.concatenate([cache_v[0].reshape(bs, past, ATT_WIDTH),
                             v_s.reshape(bs, ss, ATT_WIDTH)], axis=1).astype(BF16)
    ki_all = jnp.concatenate([cache_k_idx[0].astype(BF16), kib_s.reshape(bs, ss, IDX_DIM)], axis=1)
    k_all, v_all, ki_all = (_pad_axis(a, 1, lk) for a in (k_all, v_all, ki_all))
    vT_all = v_all.reshape(bs, lk // KEY_TILE, KEY_TILE, ATT_WIDTH).transpose(0, 1, 3, 2)
    limit_s = min((past // CHUNK + 1) * CHUNK, l_all)
    assert (past + ss - 1) // CHUNK == past // CHUNK
    lim_s = jnp.where(jnp.arange(Q_BLOCK) < ss, limit_s, CHUNK).astype(I32).reshape(1, 1, Q_BLOCK)
    nk_s = jnp.full((1,), -(-limit_s // KEY_TILE), I32)
    attn_s = _attn(nk_s, lim_s, per_batch_lanes(qT_s), per_batch_lanes(qiT_s), per_batch_lanes(wiT_s),
                   k_all, vT_all, ki_all)[:, :ss]
    cs8 = jnp.pad(state_conv[0], ((0, 0), (SUBLANES - (CONV_WIDTH - 1), 0), (0, 0)))
    x1_s, grp_s, co_s, hl_s = _mixer(x_sample, attn_s, cs8, state_h[0][:, None, :], *rg, *mg, wgt, bgt, ts=ss)
    out_s = _moe(x1_s, grp_s, *mo, ts=ns_tok, tile=ns_tok)

    keep = CONV_WIDTH - 1
    return (out_p.reshape(bp, sp, D_MODEL), out_s.reshape(bs, ss, D_MODEL),
            k_p.reshape(1, bp, sp, N_HEADS, HEAD_DIM), v_p.reshape(1, bp, sp, N_HEADS, HEAD_DIM),
            ki_p[None], co_p[:, SUBLANES - keep:][None], hl_p[:, 0][None],
            k_s.reshape(1, bs, ss, N_HEADS, HEAD_DIM), v_s.reshape(1, bs, ss, N_HEADS, HEAD_DIM),
            ki_s.reshape(1, bs, ss, IDX_DIM), co_s[:, SUBLANES - keep:][None], hl_s[:, 0][None])
```

```python
import functools

import jax
import jax.numpy as jnp
import numpy as np
from jax import lax
from jax.experimental import pallas as pl
from jax.experimental.pallas import tpu as pltpu

F32 = jnp.float32
BF16 = jnp.bfloat16
I32 = jnp.int32

D_MODEL = 1024
N_HEADS = 8
HEAD_DIM = 64
ATT_WIDTH = N_HEADS * HEAD_DIM
N_IDX_HEADS = 8
IDX_DIM = 64
MAX_TOPK = 256
CHUNK = 64
D_RNN = D_MODEL
N_RNN_BLOCKS = 8
RNN_BLOCK = D_RNN // N_RNN_BLOCKS
CONV_WIDTH = 4
LRU_C = 8.0
N_GROUPS = 4
EXPERTS_PER_GROUP = 4
N_EXPERTS = N_GROUPS * EXPERTS_PER_GROUP
D_EXPERT = 256
ROPE_THETA = 10000.0
LN_EPS = 1e-5
DEPTH = 1
DN_ALPHA = (2.0 * DEPTH) ** 0.25
SPLITS = (ATT_WIDTH, ATT_WIDTH, ATT_WIDTH, N_IDX_HEADS * IDX_DIM, IDX_DIM, N_IDX_HEADS,
          D_RNN, D_RNN, D_MODEL, D_MODEL)
SPLIT_POINTS = tuple(int(v) for v in np.cumsum(SPLITS)[:-1])

LANES = 128
SUBLANES = 8
VMEM_LIMIT = 56 * 1024 * 1024
Q_BLOCK = LANES
PROMPT_Q_BLOCK = 256
KEY_TILE = 256
INT_MIN = np.int32(-2 ** 31)
NEG_BIG = -1e30
KEY_LOWEST_FINITE = np.int32(-2 ** 31 + 2 ** 23)
LOG2_E = 1.4426950408889634
LATE_BITS = 4
WT_ROWS = 3 * ATT_WIDTH + 16
WN_COLS = 2 * ATT_WIDTH + LANES


def _params(semantics):
    return pltpu.CompilerParams(dimension_semantics=semantics, vmem_limit_bytes=VMEM_LIMIT)


def _dot(a, b):
    return jnp.dot(a, b, preferred_element_type=F32)


def _layer_norm(x, g, b):
    mu = jnp.mean(x, axis=-1, keepdims=True)
    xc = x - mu
    var = jnp.mean(xc * xc, axis=-1, keepdims=True)
    return xc * lax.rsqrt(var + LN_EPS) * g + b


def _proj_kernel(x_ref, wt_ref, wn_ref, cosn_ref, sinn_ref, cost_ref, sint_ref,
                 qT_ref, qiT_ref, vT_ref, wiT_ref, k_ref, kb_ref, v_ref, ki_ref, kib_ref):
    tm = x_ref.shape[1]
    kt_out = vT_ref.shape[3]
    xb = x_ref[0].astype(BF16)
    zt = lax.dot_general(wt_ref[...], xb, (((1,), (1,)), ((), ())), preferred_element_type=F32)
    zn = _dot(xb, wn_ref[...])
    cost = cost_ref[...]
    sint = sint_ref[...]
    half = HEAD_DIM // 2
    for base, ref, scale in ((0, qT_ref, HEAD_DIM ** -0.5 * LOG2_E), (ATT_WIDTH, qiT_ref, IDX_DIM ** -0.5)):
        for h in range(N_HEADS):
            r = h * HEAD_DIM
            x1 = zt[base + r:base + r + half]
            x2 = zt[base + r + half:base + r + HEAD_DIM]
            ref[0, r:r + half, :] = ((x1 * cost - x2 * sint) * scale).astype(BF16)
            ref[0, r + half:r + HEAD_DIM, :] = ((x1 * sint + x2 * cost) * scale).astype(BF16)
    for c in range(tm // kt_out):
        vT_ref[0, c] = zt[2 * ATT_WIDTH:3 * ATT_WIDTH, c * kt_out:(c + 1) * kt_out].astype(BF16)
    wiT_ref[0] = zt[3 * ATT_WIDTH:3 * ATT_WIDTH + N_IDX_HEADS] * (N_IDX_HEADS ** -0.5)

    cosn = cosn_ref[...]
    sinn = sinn_ref[...]
    lane = lax.broadcasted_iota(I32, (tm, LANES), 1)
    first_half = (lane & half) == 0

    def rope_n(z):
        partner = jnp.where(first_half, pltpu.roll(z, LANES - half, 1), pltpu.roll(z, half, 1))
        return z * cosn + partner * sinn

    for g in range(ATT_WIDTH // LANES):
        kg = rope_n(zn[:, g * LANES:(g + 1) * LANES])
        k_ref[0, :, g * LANES:(g + 1) * LANES] = kg
        kb_ref[0, :, g * LANES:(g + 1) * LANES] = kg.astype(BF16)
    v_ref[0] = zn[:, ATT_WIDTH:2 * ATT_WIDTH]
    kig = rope_n(zn[:, 2 * ATT_WIDTH:2 * ATT_WIDTH + LANES])
    ki_ref[0] = kig[:, :IDX_DIM]
    kib_ref[0] = kig[:, :IDX_DIM].astype(BF16)


def _proj(x, wt, wn, tables, tm, kt_out):
    b, s, _ = x.shape
    cosn, sinn, cost, sint = tables
    ns = s // tm
    out_shape = (
        jax.ShapeDtypeStruct((b, ATT_WIDTH, s), BF16),
        jax.ShapeDtypeStruct((b, ATT_WIDTH, s), BF16),
        jax.ShapeDtypeStruct((b, s // kt_out, ATT_WIDTH, kt_out), BF16),
        jax.ShapeDtypeStruct((b, N_IDX_HEADS, s), F32),
        jax.ShapeDtypeStruct((b, s, ATT_WIDTH), F32),
        jax.ShapeDtypeStruct((b, s, ATT_WIDTH), BF16),
        jax.ShapeDtypeStruct((b, s, ATT_WIDTH), F32),
        jax.ShapeDtypeStruct((b, s, IDX_DIM), F32),
        jax.ShapeDtypeStruct((b, s, IDX_DIM), BF16),
    )
    tok = lambda w: pl.BlockSpec((1, tm, w), lambda i, j: (i, j, 0))
    feat = lambda r: pl.BlockSpec((1, r, tm), lambda i, j: (i, 0, j))
    return pl.pallas_call(
        _proj_kernel,
        grid=(b, ns),
        in_specs=[
            tok(D_MODEL),
            pl.BlockSpec((WT_ROWS, D_MODEL), lambda i, j: (0, 0)),
            pl.BlockSpec((D_MODEL, WN_COLS), lambda i, j: (0, 0)),
            pl.BlockSpec((tm, LANES), lambda i, j: (j, 0)),
            pl.BlockSpec((tm, LANES), lambda i, j: (j, 0)),
            pl.BlockSpec((HEAD_DIM // 2, tm), lambda i, j: (0, j)),
            pl.BlockSpec((HEAD_DIM // 2, tm), lambda i, j: (0, j)),
        ],
        out_specs=(
            feat(ATT_WIDTH), feat(ATT_WIDTH),
            pl.BlockSpec((1, tm // kt_out, ATT_WIDTH, kt_out), lambda i, j: (i, j, 0, 0)),
            feat(N_IDX_HEADS),
            tok(ATT_WIDTH), tok(ATT_WIDTH), tok(ATT_WIDTH), tok(IDX_DIM), tok(IDX_DIM),
        ),
        out_shape=out_shape,
        compiler_params=_params(("parallel", "parallel")),
        name="proj",
    )(x, wt, wn, cosn, sinn, cost, sint)


def _order_key_to_f32(key):
    return lax.bitcast_convert_type(jnp.where(key < 0, key ^ 0x7FFFFFFF, key), F32)


def _attn_kernel(nk_ref, lim_ref, qT_ref, qiT_ref, wiT_ref, k_ref, vT_ref, ki_ref, o_ref,
                 score_scr, bias_scr, logit_scr, acc_scr):
    nk = nk_ref[pl.program_id(1)]
    lim = lim_ref[0]
    w = wiT_ref[0]
    qi = qiT_ref[0]
    q = qT_ref[0]
    qb = q.shape[1]
    groups = KEY_TILE // SUBLANES

    def rows(kt):
        return pl.ds(pl.multiple_of(kt * KEY_TILE, KEY_TILE), KEY_TILE)

    def fold(x, op):
        return op(op(x.reshape(4, groups // 4, SUBLANES, qb), axis=1), axis=0)

    def tile_loop(body, init):
        if qb > 2 * LANES:
            return lax.fori_loop(0, nk, body, init)
        c = lax.fori_loop(0, nk // 2, lambda i, c: body(2 * i + 1, body(2 * i, c)), init)
        return lax.cond(nk % 2 == 1, lambda c: body(nk - 1, c), lambda c: c, c)

    qi_pairs = [jnp.concatenate([qi[(2 * p) * IDX_DIM:(2 * p + 1) * IDX_DIM],
                                 qi[(2 * p + 1) * IDX_DIM:(2 * p + 2) * IDX_DIM]], axis=1)
                for p in range(N_IDX_HEADS // 2)]
    row_iota = lax.broadcasted_iota(I32, (KEY_TILE, qb), 0)

    def score_tile(kt, carry):
        ki_t = ki_ref[0, rows(kt), :]
        acc = jnp.zeros((KEY_TILE, qb), F32)
        for p in range(N_IDX_HEADS // 2):
            d = _dot(ki_t, qi_pairs[p])
            acc = acc + jnp.maximum(d[:, :qb], 0.0) * w[2 * p:2 * p + 1]
            acc = acc + jnp.maximum(d[:, qb:], 0.0) * w[2 * p + 1:2 * p + 2]
        score_scr[rows(kt), :] = jnp.where(row_iota + kt * KEY_TILE < lim, acc, -jnp.inf)
        return carry

    tile_loop(score_tile, 0)

    def count(pred):
        def body(kt, c):
            return c + fold(jnp.where(pred(score_scr[rows(kt), :]), 1, 0), jnp.sum)

        return jnp.sum(tile_loop(body, jnp.zeros((SUBLANES, qb), I32)), axis=0, keepdims=True)

    def bisect_for(n_tiles):
        def bisect(i, carry):
            t, cnt_t = carry
            cand = t + lax.shift_left(jnp.int32(1), 31 - i)
            cand_f = _order_key_to_f32(cand)
            c = jnp.zeros((SUBLANES, qb), I32)
            for kt in range(n_tiles):
                c = c + fold(jnp.where(score_scr[kt * KEY_TILE:(kt + 1) * KEY_TILE, :] >= cand_f, 1, 0), jnp.sum)
            cnt = jnp.sum(c, axis=0, keepdims=True)
            ok = cnt >= MAX_TOPK
            return jnp.where(ok, cand, t), jnp.where(ok, cnt, cnt_t)

        def run():
            t0 = jnp.full((1, qb), INT_MIN, I32)
            c0 = jnp.full((1, qb), n_tiles * KEY_TILE, I32)
            t, c = lax.fori_loop(0, 32 - LATE_BITS, bisect, (t0, c0))
            settled = jnp.logical_or(c == MAX_TOPK, lim <= MAX_TOPK)
            all_settled = jnp.min(jnp.where(settled, 1.0, 0.0)) > 0.0
            return lax.cond(all_settled, lambda tc: tc,
                            lambda tc: lax.fori_loop(32 - LATE_BITS, 32, bisect, tc), (t, c))
        return run

    max_tiles = score_scr.shape[0] // KEY_TILE
    thr, cnt_thr = lax.switch(nk - 1, [bisect_for(n) for n in range(1, max_tiles + 1)])
    found = thr >= KEY_LOWEST_FINITE
    thr_f = jnp.where(found, _order_key_to_f32(thr), jnp.finfo(F32).min)

    def bias_tile(kt, carry):
        bias_scr[rows(kt), :] = jnp.where(score_scr[rows(kt), :] >= thr_f, 0.0, NEG_BIG)
        return carry

    tile_loop(bias_tile, 0)

    tie = jnp.logical_and(cnt_thr > MAX_TOPK, found)
    any_tie = jnp.max(jnp.where(tie, 1.0, 0.0)) > 0.0

    @pl.when(any_tie)
    def _():
        need = (MAX_TOPK - count(lambda s: s > thr_f)).astype(F32)
        tri = (lax.broadcasted_iota(I32, (KEY_TILE, KEY_TILE), 1)
               <= lax.broadcasted_iota(I32, (KEY_TILE, KEY_TILE), 0))
        tri = jnp.where(tri, 1.0, 0.0).astype(BF16)

        def tie_tile(kt, before):
            s = score_scr[rows(kt), :]
            eq = s == thr_f
            rank = _dot(tri, jnp.where(eq, 1.0, 0.0).astype(BF16)) + before
            sel = jnp.logical_or(s > thr_f, jnp.logical_and(eq, rank <= need))
            bias_scr[rows(kt), :] = jnp.where(sel, 0.0, NEG_BIG)
            return rank[KEY_TILE - 1:KEY_TILE, :]

        lax.fori_loop(0, nk, tie_tile, jnp.zeros((1, qb), F32))

    zeros_half = jnp.zeros((HEAD_DIM, qb), BF16)
    rhs = [jnp.concatenate(
        [jnp.concatenate([q[(2 * p) * HEAD_DIM:(2 * p + 1) * HEAD_DIM], zeros_half], axis=0),
         jnp.concatenate([zeros_half, q[(2 * p + 1) * HEAD_DIM:(2 * p + 2) * HEAD_DIM]], axis=0)], axis=1)
        for p in range(N_HEADS // 2)]

    def logits_tile(kt, m_parts):
        bias = bias_scr[rows(kt), :]
        out = []
        for p in range(N_HEADS // 2):
            s2 = _dot(k_ref[0, rows(kt), p * LANES:(p + 1) * LANES], rhs[p])
            for e in range(2):
                h = 2 * p + e
                s = s2[:, e * qb:(e + 1) * qb] + bias
                logit_scr[h, rows(kt), :] = s
                out.append(jnp.maximum(m_parts[h], fold(s, jnp.max)))
        return tuple(out)

    m_parts = tile_loop(logits_tile, tuple(jnp.full((SUBLANES, qb), NEG_BIG, F32) for _ in range(N_HEADS)))
    m = [jnp.max(mp, axis=0, keepdims=True) for mp in m_parts]
    acc_scr[...] = jnp.zeros_like(acc_scr)

    def pv_tile(kt, l_parts):
        out = []
        for h in range(N_HEADS):
            p = jnp.exp2(logit_scr[h, rows(kt), :] - m[h])
            out.append(l_parts[h] + fold(p, jnp.sum))
            v_t = vT_ref[0, kt, h * HEAD_DIM:(h + 1) * HEAD_DIM, :]
            acc_scr[h * HEAD_DIM:(h + 1) * HEAD_DIM, :] += _dot(v_t, p.astype(BF16))
        return tuple(out)

    l_parts = tile_loop(pv_tile, tuple(jnp.zeros((SUBLANES, qb), F32) for _ in range(N_HEADS)))
    outs = [acc_scr[h * HEAD_DIM:(h + 1) * HEAD_DIM, :] / jnp.sum(l_parts[h], axis=0, keepdims=True)
            for h in range(N_HEADS)]
    o_ref[0] = jnp.concatenate(outs, axis=0).T.astype(BF16)


def _attn(nk, limits, qT, qiT, wiT, kb, vT, kib):
    b, _, sq = qT.shape
    qb = limits.shape[-1]
    nq = sq // qb
    lk = kb.shape[1]
    grid_spec = pltpu.PrefetchScalarGridSpec(
        num_scalar_prefetch=1,
        grid=(b, nq),
        in_specs=[
            pl.BlockSpec((1, 1, qb), lambda i, j, nk: (j, 0, 0)),
            pl.BlockSpec((1, ATT_WIDTH, qb), lambda i, j, nk: (i, 0, j)),
            pl.BlockSpec((1, ATT_WIDTH, qb), lambda i, j, nk: (i, 0, j)),
            pl.BlockSpec((1, N_IDX_HEADS, qb), lambda i, j, nk: (i, 0, j)),
            pl.BlockSpec((1, lk, ATT_WIDTH), lambda i, j, nk: (i, 0, 0)),
            pl.BlockSpec((1, lk // KEY_TILE, ATT_WIDTH, KEY_TILE), lambda i, j, nk: (i, 0, 0, 0)),
            pl.BlockSpec((1, lk, IDX_DIM), lambda i, j, nk: (i, 0, 0)),
        ],
        out_specs=pl.BlockSpec((1, qb, ATT_WIDTH), lambda i, j, nk: (i, j, 0)),
        scratch_shapes=[pltpu.VMEM((lk, qb), F32), pltpu.VMEM((lk, qb), F32),
                        pltpu.VMEM((N_HEADS, lk, qb), F32), pltpu.VMEM((ATT_WIDTH, qb), F32)],
    )
    return pl.pallas_call(
        _attn_kernel,
        grid_spec=grid_spec,
        out_shape=jax.ShapeDtypeStruct((b, sq, ATT_WIDTH), BF16),
        compiler_params=_params(("parallel", "arbitrary")),
        name="attn",
    )(nk, limits, qT, qiT, wiT, kb, vT, kib)


def _mixer_kernel(x_ref, attn_ref, cs_ref, h0_ref, wxr_ref, wgr_ref, cw_ref, cb_ref, wa_ref, ba_ref,
                  wx_ref, bx_ref, lam_ref, wga_ref, wgb_ref, wba_ref, wbr_ref, wout_ref, g_ref, b_ref,
                  wgt_ref, bgt_ref, o_ref, grp_ref, co_ref, hl_ref,
                  xbuf, a_scr, u_scr, h_scr, hc_scr):
    ts = x_ref.shape[1]

    @pl.when(pl.program_id(1) == 0)
    def _():
        xbuf[0:SUBLANES, :] = cs_ref[0]
        hc_scr[...] = h0_ref[0]

    x = x_ref[0]
    xb = x.astype(BF16)
    xr = _dot(xb, wxr_ref[...])
    gr = _dot(xb, wgr_ref[...])
    xbuf[SUBLANES:SUBLANES + ts, :] = xr
    cw = cw_ref[...]
    xc = cb_ref[...] + xbuf[5:5 + ts, :] * cw[0:1]
    xc = xc + xbuf[6:6 + ts, :] * cw[1:2]
    xc = xc + xbuf[7:7 + ts, :] * cw[2:3]
    xc = xc + xr * cw[3:4]
    tail = xbuf[ts:ts + SUBLANES, :]
    xbuf[0:SUBLANES, :] = tail
    co_ref[0] = tail

    xcb = xc.astype(BF16)

    def gate(w_ref, b_ref):
        parts = [_dot(xcb[:, n * RNN_BLOCK:(n + 1) * RNN_BLOCK], w_ref[n]) for n in range(N_RNN_BLOCKS)]
        return jax.nn.sigmoid(jnp.concatenate(parts, axis=1) + b_ref[...])

    r = gate(wa_ref, ba_ref)
    i = gate(wx_ref, bx_ref)
    nl = -lam_ref[...]
    softplus = jnp.maximum(nl, 0.0) + jnp.log1p(jnp.exp(-jnp.abs(nl)))
    log_a = (-LRU_C * r) * softplus
    a_scr[...] = jnp.exp(log_a)
    th = jnp.tanh(log_a)
    u_scr[...] = jnp.exp2(0.5 * jnp.log2(-2.0 * th / (1.0 - th))) * (i * xc)

    row = lax.broadcasted_iota(I32, (SUBLANES, D_RNN), 0)
    h_prev = hc_scr[...]
    for g in range(ts // SUBLANES):
        rs = slice(g * SUBLANES, (g + 1) * SUBLANES)
        a = a_scr[rs, :]
        b = u_scr[rs, :]
        for d in (1, 2, 4):
            a_sh = jnp.where(row >= d, pltpu.roll(a, d, 0), 1.0)
            b_sh = jnp.where(row >= d, pltpu.roll(b, d, 0), 0.0)
            b = a * b_sh + b
            a = a * a_sh
        h = b + a * h_prev
        h_scr[rs, :] = h
        h_prev = h[SUBLANES - 1:SUBLANES, :]
    hc_scr[...] = h_prev
    hl_ref[0] = h_prev
    y = (h_scr[...] * jax.nn.gelu(gr)).astype(BF16)

    ga = _dot(xb, wga_ref[...])
    gb = _dot(xb, wgb_ref[...])
    merged = (jax.nn.sigmoid(ga) * _dot(attn_ref[0], wba_ref[...])
              + jax.nn.sigmoid(gb) * _dot(y, wbr_ref[...]))
    mix = _dot(merged.astype(BF16), wout_ref[...])
    x1 = _layer_norm(DN_ALPHA * x + mix, g_ref[...], b_ref[...])
    o_ref[0] = x1
    logits = lax.dot_general(wgt_ref[...], x1.astype(BF16), (((1,), (1,)), ((), ())),
                             preferred_element_type=F32) + bgt_ref[...]
    best = logits[0:1]
    g_sel = jnp.zeros(best.shape, I32)
    for k in range(1, N_GROUPS):
        g_sel = jnp.where(logits[k:k + 1] > best, k, g_sel)
        best = jnp.maximum(best, logits[k:k + 1])
    grp_ref[0] = g_sel


def _mixer(x, attn, conv_state8, h0, wxr, wgr, conv_w, conv_b, wa, ba, wx, bx, lam,
           wga, wgb, wba, wbr, wout, g, b, wgt, bgt, ts):
    bsz, s, _ = x.shape
    ns = s // ts
    once = pl.Buffered(1)
    const = lambda a: pl.BlockSpec(a.shape, lambda i, j, nd=a.ndim: (0,) * nd, pipeline_mode=once)
    per_b = lambda r: pl.BlockSpec((1, r, D_RNN), lambda i, j: (i, 0, 0))
    tok = lambda w: pl.BlockSpec((1, ts, w), lambda i, j: (i, j, 0))
    weights = (wxr, wgr, conv_w, conv_b, wa, ba, wx, bx, lam, wga, wgb, wba, wbr, wout, g, b, wgt, bgt)
    x1, grp, co, hl = pl.pallas_call(
        _mixer_kernel,
        grid=(bsz, ns),
        in_specs=[tok(D_MODEL), tok(ATT_WIDTH), per_b(SUBLANES), per_b(1)] + [const(a) for a in weights],
        out_specs=(tok(D_MODEL), pl.BlockSpec((1, 1, ts), lambda i, j: (i * ns + j, 0, 0)),
                   per_b(SUBLANES), per_b(1)),
        out_shape=(
            jax.ShapeDtypeStruct((bsz, s, D_MODEL), F32),
            jax.ShapeDtypeStruct((bsz * ns, 1, ts), I32),
            jax.ShapeDtypeStruct((bsz, SUBLANES, D_RNN), F32),
            jax.ShapeDtypeStruct((bsz, 1, D_RNN), F32),
        ),
        scratch_shapes=[
            pltpu.VMEM((ts + SUBLANES, D_RNN), F32),
            pltpu.VMEM((ts, D_RNN), F32), pltpu.VMEM((ts, D_RNN), F32), pltpu.VMEM((ts, D_RNN), F32),
            pltpu.VMEM((1, D_RNN), F32),
        ],
        compiler_params=_params(("parallel", "arbitrary")),
        name="mixer",
    )(x, attn, conv_state8, h0, *weights)
    return x1.reshape(bsz * s, D_MODEL), grp.reshape(bsz * s), co, hl


def _route_plan(g, tile):
    t = g.shape[0]
    onehot = (g[:, None] == jnp.arange(N_GROUPS, dtype=I32)[None, :]).astype(I32)
    csum = jnp.cumsum(onehot, axis=0)
    padded = (csum[-1] + tile - 1) // tile * tile
    ends = jnp.cumsum(padded)
    pos = jnp.sum((ends - padded)[None, :] * onehot, axis=1) + jnp.sum(csum * onehot, axis=1) - 1
    n_tiles = t // tile + N_GROUPS
    tile_start = jnp.arange(n_tiles, dtype=I32) * tile
    tile_group = jnp.minimum(jnp.sum((tile_start[:, None] >= ends[None, :]).astype(I32), axis=1), N_GROUPS - 1)
    last_of_group = jnp.where(padded > 0, ends - tile, -1)
    spare = ends[-1] + jnp.arange(N_GROUPS, dtype=I32) * tile
    fill = jnp.concatenate([last_of_group, jnp.where(spare < n_tiles * tile, spare, -1)])
    return pos.astype(I32), tile_group.astype(I32), fill.astype(I32), n_tiles


ROW_DMA_UNROLL = 8


def _start_rows(copy, n):
    for r in range(n):
        copy(r).start(priority=r % 2)


def _wait_rows(copy, n):
    def batch(i, c):
        for _ in range(ROW_DMA_UNROLL):
            copy.wait()
        return c

    lax.fori_loop(0, n // ROW_DMA_UNROLL, batch, 0)


def _moe_sort_kernel(pos_ref, fill_ref, x_ref, xs_ref, sbuf, zbuf, sems, zsem):
    ts = x_ref.shape[0] // 2
    tile = zbuf.shape[0]
    i = pl.program_id(0)

    @pl.when(i == 0)
    def _():
        zbuf[...] = jnp.zeros_like(zbuf)
        for e in range(2 * N_GROUPS):
            @pl.when(fill_ref[e] >= 0)
            def _(e=e):
                start = pl.multiple_of(fill_ref[e], SUBLANES)
                zero_fill = pltpu.make_async_copy(zbuf, xs_ref.at[pl.ds(start, tile)], zsem)
                zero_fill.start()
                zero_fill.wait()

    for half in range(2):
        base = (2 * i + half) * ts

        def copy(r, half=half, base=base):
            return pltpu.make_async_copy(sbuf.at[half, pl.ds(r, 1)],
                                         xs_ref.at[pl.ds(pos_ref[base + r], 1)], sems.at[half])

        @pl.when(i > 0)
        def _(copy=copy):
            _wait_rows(copy(0), ts)

        sbuf[half] = x_ref[half * ts:(half + 1) * ts, :]
        _start_rows(copy, ts)

    @pl.when(i == pl.num_programs(0) - 1)
    def _():
        for half in range(2):
            _wait_rows(pltpu.make_async_copy(sbuf.at[half, pl.ds(0, 1)], xs_ref.at[pl.ds(0, 1)],
                                             sems.at[half]), ts)


def _moe_expert_kernel(tg_ref, xs_ref, wr_ref, br_ref, wgu_ref, wd_ref, ys_ref):
    g = tg_ref[pl.program_id(0)]
    xb = xs_ref[...].astype(BF16)
    tm = xb.shape[0]
    lane = lax.broadcasted_iota(I32, (tm, LANES), 1)
    logits = _dot(xb, wr_ref[...]) + br_ref[...]
    neg_inf = -jnp.inf
    eg = jnp.exp(jnp.where(lane < N_GROUPS, logits, neg_inf)
                 - jnp.max(jnp.where(lane < N_GROUPS, logits, neg_inf), axis=1, keepdims=True))
    p_group = (jnp.sum(jnp.where(lane == g, eg, 0.0), axis=1, keepdims=True)
               / jnp.sum(eg, axis=1, keepdims=True))
    lo = N_GROUPS + EXPERTS_PER_GROUP * g
    el = jnp.where(jnp.logical_and(lane >= lo, lane < lo + EXPERTS_PER_GROUP), logits, neg_inf)
    v1 = jnp.max(el, axis=1, keepdims=True)
    i1 = jnp.min(jnp.where(el == v1, lane, LANES), axis=1, keepdims=True)
    el2 = jnp.where(lane == i1, neg_inf, el)
    v2 = jnp.max(el2, axis=1, keepdims=True)
    i2 = jnp.min(jnp.where(el2 == v2, lane, LANES), axis=1, keepdims=True)
    e2 = jnp.exp(v2 - v1)
    p1 = p_group / (1.0 + e2)
    p2 = e2 * p1
    acts = []
    for j in range(EXPERTS_PER_GROUP):
        gate = jnp.where(i1 == lo + j, p1, 0.0) + jnp.where(i2 == lo + j, p2, 0.0)
        hj = _dot(xb, wgu_ref[j].astype(BF16))
        acts.append((gate * (jax.nn.silu(hj[:, :D_EXPERT]) * hj[:, D_EXPERT:])).astype(BF16))
    ys_ref[...] = _dot(jnp.concatenate(acts, axis=1), wd_ref[0].astype(BF16))


def _moe_unsort_kernel(pos_ref, x_ref, ys_ref, g_ref, b_ref, o_ref, ybuf, sems):
    ts = x_ref.shape[0] // 2
    i = pl.program_id(0)

    def fetch(half_tile, buf):
        _start_rows(lambda r: pltpu.make_async_copy(ys_ref.at[pl.ds(pos_ref[half_tile * ts + r], 1)],
                                                    ybuf.at[buf, pl.ds(r, 1)], sems.at[buf]), ts)

    def finish(buf):
        _wait_rows(pltpu.make_async_copy(ys_ref.at[pl.ds(0, 1)], ybuf.at[buf, pl.ds(0, 1)], sems.at[buf]), ts)
        rows = slice(buf * ts, (buf + 1) * ts)
        o_ref[rows, :] = _layer_norm(DN_ALPHA * x_ref[rows, :] + ybuf[buf], g_ref[...], b_ref[...])

    @pl.when(i == 0)
    def _():
        fetch(0, 0)

    fetch(2 * i + 1, 1)
    finish(0)

    @pl.when(i < pl.num_programs(0) - 1)
    def _():
        fetch(2 * i + 2, 0)

    finish(1)


def _moe(x, grp, wr, br, wgu, wd, g, b, ts, tile):
    t = x.shape[0]
    pos, tile_group, fill, n_tiles = _route_plan(grp, tile)
    any_spec = pl.BlockSpec(memory_space=pl.ANY)
    xs = pl.pallas_call(
        _moe_sort_kernel,
        grid_spec=pltpu.PrefetchScalarGridSpec(
            num_scalar_prefetch=2, grid=(t // ts,),
            in_specs=[pl.BlockSpec((ts, D_MODEL), lambda i, pos, fill: (i, 0))],
            out_specs=any_spec,
            scratch_shapes=[pltpu.VMEM((2, ts // 2, D_MODEL), F32), pltpu.VMEM((tile, D_MODEL), F32),
                            pltpu.SemaphoreType.DMA((2,)), pltpu.SemaphoreType.DMA(())]),
        out_shape=jax.ShapeDtypeStruct((n_tiles * tile, D_MODEL), F32),
        compiler_params=_params(("arbitrary",)),
        name="moe_sort",
    )(pos, fill, x)
    ys = pl.pallas_call(
        _moe_expert_kernel,
        grid_spec=pltpu.PrefetchScalarGridSpec(
            num_scalar_prefetch=1, grid=(n_tiles,),
            in_specs=[
                pl.BlockSpec((tile, D_MODEL), lambda i, tg: (i, 0)),
                pl.BlockSpec((D_MODEL, LANES), lambda i, tg: (0, 0)),
                pl.BlockSpec((1, LANES), lambda i, tg: (0, 0)),
                pl.BlockSpec((EXPERTS_PER_GROUP, D_MODEL, 2 * D_EXPERT), lambda i, tg: (tg[i], 0, 0)),
                pl.BlockSpec((1, EXPERTS_PER_GROUP * D_EXPERT, D_MODEL), lambda i, tg: (tg[i], 0, 0)),
            ],
            out_specs=pl.BlockSpec((tile, D_MODEL), lambda i, tg: (i, 0))),
        out_shape=jax.ShapeDtypeStruct((n_tiles * tile, D_MODEL), F32),
        compiler_params=_params(("arbitrary",)),
        name="moe_experts",
    )(tile_group, xs, wr, br, wgu, wd)
    return pl.pallas_call(
        _moe_unsort_kernel,
        grid_spec=pltpu.PrefetchScalarGridSpec(
            num_scalar_prefetch=1, grid=(t // ts,),
            in_specs=[pl.BlockSpec((ts, D_MODEL), lambda i, pos: (i, 0)), any_spec,
                      pl.BlockSpec((1, D_MODEL), lambda i, pos: (0, 0)),
                      pl.BlockSpec((1, D_MODEL), lambda i, pos: (0, 0))],
            out_specs=pl.BlockSpec((ts, D_MODEL), lambda i, pos: (i, 0)),
            scratch_shapes=[pltpu.VMEM((2, ts // 2, D_MODEL), F32), pltpu.SemaphoreType.DMA((2,))]),
        out_shape=jax.ShapeDtypeStruct((t, D_MODEL), F32),
        compiler_params=_params(("arbitrary",)),
        name="moe_unsort",
    )(pos, x, ys, g, b)


def _rope_tables(pos):
    half = HEAD_DIM // 2
    freqs = ROPE_THETA ** (-jnp.arange(half, dtype=F32) / half)
    ang = pos.astype(F32)[:, None] * freqs[None, :]
    cos, sin = jnp.cos(ang), jnp.sin(ang)
    cosn = jnp.tile(cos, (1, LANES // half))
    sinn = jnp.tile(jnp.concatenate([-sin, sin], axis=1), (1, LANES // HEAD_DIM))
    return cosn, sinn, cos.T, sin.T


def _pad_axis(a, axis, size):
    pad = [(0, 0)] * a.ndim
    pad[axis] = (0, size - a.shape[axis])
    return jnp.pad(a, pad)


def kernel(x_prompt, x_sample, cache_k, cache_v, cache_k_idx, state_conv, state_h, w_in, conv_w, conv_b, w_rg_a, b_rg_a, w_rg_x, b_rg_x, lru_lambda, w_branch_attn, w_branch_rnn, w_out, ln1_g, ln1_b, w_router_group, b_router_group, w_router_expert, b_router_expert, w_gate_up, w_down, ln2_g, ln2_b):
    assert w_in.shape[0] == DEPTH == 1
    bp, sp, _ = x_prompt.shape
    bs, ss, _ = x_sample.shape
    past = cache_k.shape[2]
    ns_tok = bs * ss
    assert ns_tok == Q_BLOCK and sp % Q_BLOCK == 0 and sp % KEY_TILE == 0

    wq, wk, wv, wqi, wki, wwi, wxr, wgr, wga, wgb = jnp.split(w_in[0], SPLIT_POINTS, axis=1)
    wt = jnp.concatenate([wq.T, wqi.T, wv.T, _pad_axis(wwi.T, 0, 16)], axis=0).astype(BF16)
    wn = jnp.concatenate([wk, wv, _pad_axis(wki, 1, LANES)], axis=1).astype(BF16)
    row = lambda v: v.reshape(1, -1)
    rg = (wxr.astype(BF16), wgr.astype(BF16), conv_w[0], row(conv_b[0]),
          w_rg_a[0].astype(BF16), row(b_rg_a[0]), w_rg_x[0].astype(BF16), row(b_rg_x[0]),
          row(lru_lambda[0]))
    mg = (wga.astype(BF16), wgb.astype(BF16), w_branch_attn[0].astype(BF16),
          w_branch_rnn[0].astype(BF16), w_out[0].astype(BF16), row(ln1_g[0]), row(ln1_b[0]))
    wr = _pad_axis(jnp.concatenate([w_router_group[0], w_router_expert[0]], axis=1), 1, LANES).astype(BF16)
    br = _pad_axis(jnp.concatenate([b_router_group[0], b_router_expert[0]]).reshape(1, -1), 1, LANES)
    wgt = _pad_axis(w_router_group[0].T, 0, 16).astype(BF16)
    bgt = _pad_axis(b_router_group[0].reshape(-1, 1), 0, 16)
    wd = w_down[0].reshape(N_GROUPS, EXPERTS_PER_GROUP * D_EXPERT, D_MODEL)
    mo = (wr, br, w_gate_up[0], wd, row(ln2_g[0]), row(ln2_b[0]))

    pos_p = jnp.arange(sp, dtype=I32)
    qT, qiT, vT, wiT, k_p, kb_p, v_p, ki_p, kib_p = _proj(x_prompt, wt, wn, _rope_tables(pos_p), 512, KEY_TILE)
    lim_p = jnp.minimum((pos_p // CHUNK + 1) * CHUNK, sp).reshape(sp // PROMPT_Q_BLOCK, 1, PROMPT_Q_BLOCK)
    nk_p = (jnp.max(lim_p, axis=(1, 2)) + KEY_TILE - 1) // KEY_TILE
    attn_p = _attn(nk_p.astype(I32), lim_p, qT, qiT, wiT, kb_p, vT, kib_p)
    x1_p, grp_p, co_p, hl_p = _mixer(x_prompt, attn_p, jnp.zeros((bp, SUBLANES, D_RNN), F32),
                                     jnp.zeros((bp, 1, D_RNN), F32), *rg, *mg, wgt, bgt, ts=256)
    out_p = _moe(x1_p, grp_p, *mo, ts=512, tile=512)

    pos_s = past + (jnp.arange(ns_tok, dtype=I32) % ss)
    xs_flat = x_sample.reshape(1, ns_tok, D_MODEL)
    qT_s, qiT_s, _, wiT_s, k_s, kb_s, v_s, ki_s, kib_s = _proj(xs_flat, wt, wn, _rope_tables(pos_s), ns_tok, ns_tok)

    def per_batch_lanes(a):
        r = a.shape[1]
        return _pad_axis(a[0].reshape(r, bs, ss).transpose(1, 0, 2), 2, Q_BLOCK)

    l_all = past + ss
    lk = -(-l_all // KEY_TILE) * KEY_TILE
    k_all = jnp.concatenate([cache_k[0].reshape(bs, past, ATT_WIDTH).astype(BF16),
                             kb_s.reshape(bs, ss, ATT_WIDTH)], axis=1)
    v_all = jnp.concatenate([cache_v[0].reshape(bs, past, ATT_WIDTH),
                             v_s.reshape(bs, ss, ATT_WIDTH)], axis=1).astype(BF16)
    ki_all = jnp.concatenate([cache_k_idx[0].astype(BF16), kib_s.reshape(bs, ss, IDX_DIM)], axis=1)
    k_all, v_all, ki_all = (_pad_axis(a, 1, lk) for a in (k_all, v_all, ki_all))
    vT_all = v_all.reshape(bs, lk // KEY_TILE, KEY_TILE, ATT_WIDTH).transpose(0, 1, 3, 2)
    limit_s = min((past // CHUNK + 1) * CHUNK, l_all)
    assert (past + ss - 1) // CHUNK == past // CHUNK
    lim_s = jnp.where(jnp.arange(Q_BLOCK) < ss, limit_s, CHUNK).astype(I32).reshape(1, 1, Q_BLOCK)
    nk_s = jnp.full((1,), -(-limit_s // KEY_TILE), I32)
    attn_s = _attn(nk_s, lim_s, per_batch_lanes(qT_s), per_batch_lanes(qiT_s), per_batch_lanes(wiT_s),
                   k_all, vT_all, ki_all)[:, :ss]
    cs8 = jnp.pad(state_conv[0], ((0, 0), (SUBLANES - (CONV_WIDTH - 1), 0), (0, 0)))
    x1_s, grp_s, co_s, hl_s = _mixer(x_sample, attn_s, cs8, state_h[0][:, None, :], *rg, *mg, wgt, bgt, ts=ss)
    out_s = _moe(x1_s, grp_s, *mo, ts=ns_tok, tile=ns_tok)

    keep = CONV_WIDTH - 1
    return (out_p.reshape(bp, sp, D_MODEL), out_s.reshape(bs, ss, D_MODEL),
            k_p.reshape(1, bp, sp, N_HEADS, HEAD_DIM), v_p.reshape(1, bp, sp, N_HEADS, HEAD_DIM),
            ki_p[None], co_p[:, SUBLANES - keep:][None], hl_p[:, 0][None],
            k_s.reshape(1, bs, ss, N_HEADS, HEAD_DIM), v_s.reshape(1, bs, ss, N_HEADS, HEAD_DIM),
            ki_s.reshape(1, bs, ss, IDX_DIM), co_s[:, SUBLANES - keep:][None], hl_s[:, 0][None])
```

```python
import functools

import jax
import jax.numpy as jnp
import numpy as np
from jax import lax
from jax.experimental import pallas as pl
from jax.experimental.pallas import tpu as pltpu

F32 = jnp.float32
BF16 = jnp.bfloat16
I32 = jnp.int32

D_MODEL = 1024
N_HEADS = 8
HEAD_DIM = 64
ATT_WIDTH = N_HEADS * HEAD_DIM
N_IDX_HEADS = 8
IDX_DIM = 64
MAX_TOPK = 256
CHUNK = 64
D_RNN = D_MODEL
N_RNN_BLOCKS = 8
RNN_BLOCK = D_RNN // N_RNN_BLOCKS
CONV_WIDTH = 4
LRU_C = 8.0
N_GROUPS = 4
EXPERTS_PER_GROUP = 4
N_EXPERTS = N_GROUPS * EXPERTS_PER_GROUP
D_EXPERT = 256
ROPE_THETA = 10000.0
LN_EPS = 1e-5
DEPTH = 1
DN_ALPHA = (2.0 * DEPTH) ** 0.25
SPLITS = (ATT_WIDTH, ATT_WIDTH, ATT_WIDTH, N_IDX_HEADS * IDX_DIM, IDX_DIM, N_IDX_HEADS,
          D_RNN, D_RNN, D_MODEL, D_MODEL)
SPLIT_POINTS = tuple(int(v) for v in np.cumsum(SPLITS)[:-1])

LANES = 128
SUBLANES = 8
VMEM_LIMIT = 56 * 1024 * 1024
Q_BLOCK = LANES
PROMPT_Q_BLOCK = 256
KEY_TILE = 256
INT_MIN = np.int32(-2 ** 31)
NEG_BIG = -1e30
KEY_LOWEST_FINITE = np.int32(-2 ** 31 + 2 ** 23)
LOG2_E = 1.4426950408889634
LATE_BITS = 4
WT_ROWS = 3 * ATT_WIDTH + 16
WN_COLS = 2 * ATT_WIDTH + LANES


def _params(semantics):
    return pltpu.CompilerParams(dimension_semantics=semantics, vmem_limit_bytes=VMEM_LIMIT)


def _dot(a, b):
    return jnp.dot(a, b, preferred_element_type=F32)


def _layer_norm(x, g, b):
    mu = jnp.mean(x, axis=-1, keepdims=True)
    xc = x - mu
    var = jnp.mean(xc * xc, axis=-1, keepdims=True)
    return xc * lax.rsqrt(var + LN_EPS) * g + b


def _proj_kernel(x_ref, wt_ref, wn_ref, cosn_ref, sinn_ref, cost_ref, sint_ref,
                 qT_ref, qiT_ref, vT_ref, wiT_ref, k_ref, kb_ref, v_ref, ki_ref, kib_ref):
    tm = x_ref.shape[1]
    kt_out = vT_ref.shape[3]
    xb = x_ref[0].astype(BF16)
    zt = lax.dot_general(wt_ref[...], xb, (((1,), (1,)), ((), ())), preferred_element_type=F32)
    zn = _dot(xb, wn_ref[...])
    cost = cost_ref[...]
    sint = sint_ref[...]
    half = HEAD_DIM // 2
    for base, ref, scale in ((0, qT_ref, HEAD_DIM ** -0.5 * LOG2_E), (ATT_WIDTH, qiT_ref, IDX_DIM ** -0.5)):
        for h in range(N_HEADS):
            r = h * HEAD_DIM
            x1 = zt[base + r:base + r + half]
            x2 = zt[base + r + half:base + r + HEAD_DIM]
            ref[0, r:r + half, :] = ((x1 * cost - x2 * sint) * scale).astype(BF16)
            ref[0, r + half:r + HEAD_DIM, :] = ((x1 * sint + x2 * cost) * scale).astype(BF16)
    for c in range(tm // kt_out):
        vT_ref[0, c] = zt[2 * ATT_WIDTH:3 * ATT_WIDTH, c * kt_out:(c + 1) * kt_out].astype(BF16)
    wiT_ref[0] = zt[3 * ATT_WIDTH:3 * ATT_WIDTH + N_IDX_HEADS] * (N_IDX_HEADS ** -0.5)

    cosn = cosn_ref[...]
    sinn = sinn_ref[...]
    lane = lax.broadcasted_iota(I32, (tm, LANES), 1)
    first_half = (lane & half) == 0

    def rope_n(z):
        partner = jnp.where(first_half, pltpu.roll(z, LANES - half, 1), pltpu.roll(z, half, 1))
        return z * cosn + partner * sinn

    for g in range(ATT_WIDTH // LANES):
        kg = rope_n(zn[:, g * LANES:(g + 1) * LANES])
        k_ref[0, :, g * LANES:(g + 1) * LANES] = kg
        kb_ref[0, :, g * LANES:(g + 1) * LANES] = kg.astype(BF16)
    v_ref[0] = zn[:, ATT_WIDTH:2 * ATT_WIDTH]
    kig = rope_n(zn[:, 2 * ATT_WIDTH:2 * ATT_WIDTH + LANES])
    ki_ref[0] = kig[:, :IDX_DIM]
    kib_ref[0] = kig[:, :IDX_DIM].astype(BF16)


def _proj(x, wt, wn, tables, tm, kt_out):
    b, s, _ = x.shape
    cosn, sinn, cost, sint = tables
    ns = s // tm
    out_shape = (
        jax.ShapeDtypeStruct((b, ATT_WIDTH, s), BF16),
        jax.ShapeDtypeStruct((b, ATT_WIDTH, s), BF16),
        jax.ShapeDtypeStruct((b, s // kt_out, ATT_WIDTH, kt_out), BF16),
        jax.ShapeDtypeStruct((b, N_IDX_HEADS, s), F32),
        jax.ShapeDtypeStruct((b, s, ATT_WIDTH), F32),
        jax.ShapeDtypeStruct((b, s, ATT_WIDTH), BF16),
        jax.ShapeDtypeStruct((b, s, ATT_WIDTH), F32),
        jax.ShapeDtypeStruct((b, s, IDX_DIM), F32),
        jax.ShapeDtypeStruct((b, s, IDX_DIM), BF16),
    )
    tok = lambda w: pl.BlockSpec((1, tm, w), lambda i, j: (i, j, 0))
    feat = lambda r: pl.BlockSpec((1, r, tm), lambda i, j: (i, 0, j))
    return pl.pallas_call(
        _proj_kernel,
        grid=(b, ns),
        in_specs=[
            tok(D_MODEL),
            pl.BlockSpec((WT_ROWS, D_MODEL), lambda i, j: (0, 0)),
            pl.BlockSpec((D_MODEL, WN_COLS), lambda i, j: (0, 0)),
            pl.BlockSpec((tm, LANES), lambda i, j: (j, 0)),
            pl.BlockSpec((tm, LANES), lambda i, j: (j, 0)),
            pl.BlockSpec((HEAD_DIM // 2, tm), lambda i, j: (0, j)),
            pl.BlockSpec((HEAD_DIM // 2, tm), lambda i, j: (0, j)),
        ],
        out_specs=(
            feat(ATT_WIDTH), feat(ATT_WIDTH),
            pl.BlockSpec((1, tm // kt_out, ATT_WIDTH, kt_out), lambda i, j: (i, j, 0, 0)),
            feat(N_IDX_HEADS),
            tok(ATT_WIDTH), tok(ATT_WIDTH), tok(ATT_WIDTH), tok(IDX_DIM), tok(IDX_DIM),
        ),
        out_shape=out_shape,
        compiler_params=_params(("parallel", "parallel")),
        name="proj",
    )(x, wt, wn, cosn, sinn, cost, sint)


def _order_key_to_f32(key):
    return lax.bitcast_convert_type(jnp.where(key < 0, key ^ 0x7FFFFFFF, key), F32)


def _attn_kernel(nk_ref, lim_ref, qT_ref, qiT_ref, wiT_ref, k_ref, vT_ref, ki_ref, o_ref,
                 score_scr, bias_scr, logit_scr, acc_scr):
    nk = nk_ref[pl.program_id(1)]
    lim = lim_ref[0]
    w = wiT_ref[0]
    qi = qiT_ref[0]
    q = qT_ref[0]
    qb = q.shape[1]
    groups = KEY_TILE // SUBLANES

    def rows(kt):
        return pl.ds(pl.multiple_of(kt * KEY_TILE, KEY_TILE), KEY_TILE)

    def fold(x, op):
        return op(op(x.reshape(4, groups // 4, SUBLANES, qb), axis=1), axis=0)

    def tile_loop(body, init):
        if qb > 2 * LANES:
            return lax.fori_loop(0, nk, body, init)
        c = lax.fori_loop(0, nk // 2, lambda i, c: body(2 * i + 1, body(2 * i, c)), init)
        return lax.cond(nk % 2 == 1, lambda c: body(nk - 1, c), lambda c: c, c)

    qi_pairs = [jnp.concatenate([qi[(2 * p) * IDX_DIM:(2 * p + 1) * IDX_DIM],
                                 qi[(2 * p + 1) * IDX_DIM:(2 * p + 2) * IDX_DIM]], axis=1)
                for p in range(N_IDX_HEADS // 2)]
    row_iota = lax.broadcasted_iota(I32, (KEY_TILE, qb), 0)

    def score_tile(kt, carry):
        ki_t = ki_ref[0, rows(kt), :]
        acc = jnp.zeros((KEY_TILE, qb), F32)
        for p in range(N_IDX_HEADS // 2):
            d = _dot(ki_t, qi_pairs[p])
            acc = acc + jnp.maximum(d[:, :qb], 0.0) * w[2 * p:2 * p + 1]
            acc = acc + jnp.maximum(d[:, qb:], 0.0) * w[2 * p + 1:2 * p + 2]
        score_scr[rows(kt), :] = jnp.where(row_iota + kt * KEY_TILE < lim, acc, -jnp.inf)
        return carry

    tile_loop(score_tile, 0)

    def count(pred):
        def body(kt, c):
            return c + fold(jnp.where(pred(score_scr[rows(kt), :]), 1, 0), jnp.sum)

        return jnp.sum(tile_loop(body, jnp.zeros((SUBLANES, qb), I32)), axis=0, keepdims=True)

    def bisect_for(n_tiles):
        def bisect(i, carry):
            t, cnt_t = carry
            cand = t + lax.shift_left(jnp.int32(1), 31 - i)
            cand_f = _order_key_to_f32(cand)
            c = jnp.zeros((SUBLANES, qb), I32)
            for kt in range(n_tiles):
                c = c + fold(jnp.where(score_scr[kt * KEY_TILE:(kt + 1) * KEY_TILE, :] >= cand_f, 1, 0), jnp.sum)
            cnt = jnp.sum(c, axis=0, keepdims=True)
            ok = cnt >= MAX_TOPK
            return jnp.where(ok, cand, t), jnp.where(ok, cnt, cnt_t)

        def run():
            t0 = jnp.full((1, qb), INT_MIN, I32)
            c0 = jnp.full((1, qb), n_tiles * KEY_TILE, I32)
            t, c = lax.fori_loop(0, 32 - LATE_BITS, bisect, (t0, c0))
            settled = jnp.logical_or(c == MAX_TOPK, lim <= MAX_TOPK)
            all_settled = jnp.min(jnp.where(settled, 1.0, 0.0)) > 0.0
            return lax.cond(all_settled, lambda tc: tc,
                            lambda tc: lax.fori_loop(32 - LATE_BITS, 32, bisect, tc), (t, c))
        return run

    max_tiles = score_scr.shape[0] // KEY_TILE
    thr, cnt_thr = lax.switch(nk - 1, [bisect_for(n) for n in range(1, max_tiles + 1)])
    found = thr >= KEY_LOWEST_FINITE
    thr_f = jnp.where(found, _order_key_to_f32(thr), jnp.finfo(F32).min)

    def bias_tile(kt, carry):
        bias_scr[rows(kt), :] = jnp.where(score_scr[rows(kt), :] >= thr_f, 0.0, NEG_BIG)
        return carry

    tile_loop(bias_tile, 0)

    tie = jnp.logical_and(cnt_thr > MAX_TOPK, found)
    any_tie = jnp.max(jnp.where(tie, 1.0, 0.0)) > 0.0

    @pl.when(any_tie)
    def _():
        need = (MAX_TOPK - count(lambda s: s > thr_f)).astype(F32)
        tri = (lax.broadcasted_iota(I32, (KEY_TILE, KEY_TILE), 1)
               <= lax.broadcasted_iota(I32, (KEY_TILE, KEY_TILE), 0))
        tri = jnp.where(tri, 1.0, 0.0).astype(BF16)

        def tie_tile(kt, before):
            s = score_scr[rows(kt), :]
            eq = s == thr_f
            rank = _dot(tri, jnp.where(eq, 1.0, 0.0).astype(BF16)) + before
            sel = jnp.logical_or(s > thr_f, jnp.logical_and(eq, rank <= need))
            bias_scr[rows(kt), :] = jnp.where(sel, 0.0, NEG_BIG)
            return rank[KEY_TILE - 1:KEY_TILE, :]

        lax.fori_loop(0, nk, tie_tile, jnp.zeros((1, qb), F32))

    zeros_half = jnp.zeros((HEAD_DIM, qb), BF16)
    rhs = [jnp.concatenate(
        [jnp.concatenate([q[(2 * p) * HEAD_DIM:(2 * p + 1) * HEAD_DIM], zeros_half], axis=0),
         jnp.concatenate([zeros_half, q[(2 * p + 1) * HEAD_DIM:(2 * p + 2) * HEAD_DIM]], axis=0)], axis=1)
        for p in range(N_HEADS // 2)]

    def logits_tile(kt, m_parts):
        bias = bias_scr[rows(kt), :]
        out = []
        for p in range(N_HEADS // 2):
            s2 = _dot(k_ref[0, rows(kt), p * LANES:(p + 1) * LANES], rhs[p])
            for e in range(2):
                h = 2 * p + e
                s = s2[:, e * qb:(e + 1) * qb] + bias
                logit_scr[h, rows(kt), :] = s
                out.append(jnp.maximum(m_parts[h], fold(s, jnp.max)))
        return tuple(out)

    m_parts = tile_loop(logits_tile, tuple(jnp.full((SUBLANES, qb), NEG_BIG, F32) for _ in range(N_HEADS)))
    m = [jnp.max(mp, axis=0, keepdims=True) for mp in m_parts]
    acc_scr[...] = jnp.zeros_like(acc_scr)

    def pv_tile(kt, l_parts):
        out = []
        for h in range(N_HEADS):
            p = jnp.exp2(logit_scr[h, rows(kt), :] - m[h])
            out.append(l_parts[h] + fold(p, jnp.sum))
            v_t = vT_ref[0, kt, h * HEAD_DIM:(h + 1) * HEAD_DIM, :]
            acc_scr[h * HEAD_DIM:(h + 1) * HEAD_DIM, :] += _dot(v_t, p.astype(BF16))
        return tuple(out)

    l_parts = tile_loop(pv_tile, tuple(jnp.zeros((SUBLANES, qb), F32) for _ in range(N_HEADS)))
    outs = [acc_scr[h * HEAD_DIM:(h + 1) * HEAD_DIM, :] / jnp.sum(l_parts[h], axis=0, keepdims=True)
            for h in range(N_HEADS)]
    o_ref[0] = jnp.concatenate(outs, axis=0).T.astype(BF16)


def _attn(nk, limits, qT, qiT, wiT, kb, vT, kib):
    b, _, sq = qT.shape
    qb = limits.shape[-1]
    nq = sq // qb
    lk = kb.shape[1]
    grid_spec = pltpu.PrefetchScalarGridSpec(
        num_scalar_prefetch=1,
        grid=(b, nq),
        in_specs=[
            pl.BlockSpec((1, 1, qb), lambda i, j, nk: (j, 0, 0)),
            pl.BlockSpec((1, ATT_WIDTH, qb), lambda i, j, nk: (i, 0, j)),
            pl.BlockSpec((1, ATT_WIDTH, qb), lambda i, j, nk: (i, 0, j)),
            pl.BlockSpec((1, N_IDX_HEADS, qb), lambda i, j, nk: (i, 0, j)),
            pl.BlockSpec((1, lk, ATT_WIDTH), lambda i, j, nk: (i, 0, 0)),
            pl.BlockSpec((1, lk // KEY_TILE, ATT_WIDTH, KEY_TILE), lambda i, j, nk: (i, 0, 0, 0)),
            pl.BlockSpec((1, lk, IDX_DIM), lambda i, j, nk: (i, 0, 0)),
        ],
        out_specs=pl.BlockSpec((1, qb, ATT_WIDTH), lambda i, j, nk: (i, j, 0)),
        scratch_shapes=[pltpu.VMEM((lk, qb), F32), pltpu.VMEM((lk, qb), F32),
                        pltpu.VMEM((N_HEADS, lk, qb), F32), pltpu.VMEM((ATT_WIDTH, qb), F32)],
    )
    return pl.pallas_call(
        _attn_kernel,
        grid_spec=grid_spec,
        out_shape=jax.ShapeDtypeStruct((b, sq, ATT_WIDTH), BF16),
        compiler_params=_params(("parallel", "arbitrary")),
        name="attn",
    )(nk, limits, qT, qiT, wiT, kb, vT, kib)


def _mixer_kernel(x_ref, attn_ref, cs_ref, h0_ref, wxr_ref, wgr_ref, cw_ref, cb_ref, wa_ref, ba_ref,
                  wx_ref, bx_ref, lam_ref, wga_ref, wgb_ref, wba_ref, wbr_ref, wout_ref, g_ref, b_ref,
                  wgt_ref, bgt_ref, o_ref, grp_ref, co_ref, hl_ref,
                  xbuf, a_scr, u_scr, h_scr, hc_scr):
    ts = x_ref.shape[1]

    @pl.when(pl.program_id(1) == 0)
    def _():
        xbuf[0:SUBLANES, :] = cs_ref[0]
        hc_scr[...] = h0_ref[0]

    x = x_ref[0]
    xb = x.astype(BF16)
    xr = _dot(xb, wxr_ref[...])
    gr = _dot(xb, wgr_ref[...])
    xbuf[SUBLANES:SUBLANES + ts, :] = xr
    cw = cw_ref[...]
    xc = cb_ref[...] + xbuf[5:5 + ts, :] * cw[0:1]
    xc = xc + xbuf[6:6 + ts, :] * cw[1:2]
    xc = xc + xbuf[7:7 + ts, :] * cw[2:3]
    xc = xc + xr * cw[3:4]
    tail = xbuf[ts:ts + SUBLANES, :]
    xbuf[0:SUBLANES, :] = tail
    co_ref[0] = tail

    xcb = xc.astype(BF16)

    def gate(w_ref, b_ref):
        parts = [_dot(xcb[:, n * RNN_BLOCK:(n + 1) * RNN_BLOCK], w_ref[n]) for n in range(N_RNN_BLOCKS)]
        return jax.nn.sigmoid(jnp.concatenate(parts, axis=1) + b_ref[...])

    r = gate(wa_ref, ba_ref)
    i = gate(wx_ref, bx_ref)
    nl = -lam_ref[...]
    softplus = jnp.maximum(nl, 0.0) + jnp.log1p(jnp.exp(-jnp.abs(nl)))
    log_a = (-LRU_C * r) * softplus
    a_scr[...] = jnp.exp(log_a)
    th = jnp.tanh(log_a)
    u_scr[...] = jnp.exp2(0.5 * jnp.log2(-2.0 * th / (1.0 - th))) * (i * xc)

    row = lax.broadcasted_iota(I32, (SUBLANES, D_RNN), 0)
    h_prev = hc_scr[...]
    for g in range(ts // SUBLANES):
        rs = slice(g * SUBLANES, (g + 1) * SUBLANES)
        a = a_scr[rs, :]
        b = u_scr[rs, :]
        for d in (1, 2, 4):
            a_sh = jnp.where(row >= d, pltpu.roll(a, d, 0), 1.0)
            b_sh = jnp.where(row >= d, pltpu.roll(b, d, 0), 0.0)
            b = a * b_sh + b
            a = a * a_sh
        h = b + a * h_prev
        h_scr[rs, :] = h
        h_prev = h[SUBLANES - 1:SUBLANES, :]
    hc_scr[...] = h_prev
    hl_ref[0] = h_prev
    y = (h_scr[...] * jax.nn.gelu(gr)).astype(BF16)

    ga = _dot(xb, wga_ref[...])
    gb = _dot(xb, wgb_ref[...])
    merged = (jax.nn.sigmoid(ga) * _dot(attn_ref[0], wba_ref[...])
              + jax.nn.sigmoid(gb) * _dot(y, wbr_ref[...]))
    mix = _dot(merged.astype(BF16), wout_ref[...])
    x1 = _layer_norm(DN_ALPHA * x + mix, g_ref[...], b_ref[...])
    o_ref[0] = x1
    logits = lax.dot_general(wgt_ref[...], x1.astype(BF16), (((1,), (1,)), ((), ())),
                             preferred_element_type=F32) + bgt_ref[...]
    best = logits[0:1]
    g_sel = jnp.zeros(best.shape, I32)
    for k in range(1, N_GROUPS):
        g_sel = jnp.where(logits[k:k + 1] > best, k, g_sel)
        best = jnp.maximum(best, logits[k:k + 1])
    grp_ref[0] = g_sel


def _mixer(x, attn, conv_state8, h0, wxr, wgr, conv_w, conv_b, wa, ba, wx, bx, lam,
           wga, wgb, wba, wbr, wout, g, b, wgt, bgt, ts):
    bsz, s, _ = x.shape
    ns = s // ts
    once = pl.Buffered(1)
    const = lambda a: pl.BlockSpec(a.shape, lambda i, j, nd=a.ndim: (0,) * nd, pipeline_mode=once)
    per_b = lambda r: pl.BlockSpec((1, r, D_RNN), lambda i, j: (i, 0, 0))
    tok = lambda w: pl.BlockSpec((1, ts, w), lambda i, j: (i, j, 0))
    weights = (wxr, wgr, conv_w, conv_b, wa, ba, wx, bx, lam, wga, wgb, wba, wbr, wout, g, b, wgt, bgt)
    x1, grp, co, hl = pl.pallas_call(
        _mixer_kernel,
        grid=(bsz, ns),
        in_specs=[tok(D_MODEL), tok(ATT_WIDTH), per_b(SUBLANES), per_b(1)] + [const(a) for a in weights],
        out_specs=(tok(D_MODEL), pl.BlockSpec((1, 1, ts), lambda i, j: (i * ns + j, 0, 0)),
                   per_b(SUBLANES), per_b(1)),
        out_shape=(
            jax.ShapeDtypeStruct((bsz, s, D_MODEL), F32),
            jax.ShapeDtypeStruct((bsz * ns, 1, ts), I32),
            jax.ShapeDtypeStruct((bsz, SUBLANES, D_RNN), F32),
            jax.ShapeDtypeStruct((bsz, 1, D_RNN), F32),
        ),
        scratch_shapes=[
            pltpu.VMEM((ts + SUBLANES, D_RNN), F32),
            pltpu.VMEM((ts, D_RNN), F32), pltpu.VMEM((ts, D_RNN), F32), pltpu.VMEM((ts, D_RNN), F32),
            pltpu.VMEM((1, D_RNN), F32),
        ],
        compiler_params=_params(("parallel", "arbitrary")),
        name="mixer",
    )(x, attn, conv_state8, h0, *weights)
    return x1.reshape(bsz * s, D_MODEL), grp.reshape(bsz * s), co, hl


def _route_plan(g, tile):
    t = g.shape[0]
    onehot = (g[:, None] == jnp.arange(N_GROUPS, dtype=I32)[None, :]).astype(I32)
    csum = jnp.cumsum(onehot, axis=0)
    padded = (csum[-1] + tile - 1) // tile * tile
    ends = jnp.cumsum(padded)
    pos = jnp.sum((ends - padded)[None, :] * onehot, axis=1) + jnp.sum(csum * onehot, axis=1) - 1
    n_tiles = t // tile + N_GROUPS
    tile_start = jnp.arange(n_tiles, dtype=I32) * tile
    tile_group = jnp.minimum(jnp.sum((tile_start[:, None] >= ends[None, :]).astype(I32), axis=1), N_GROUPS - 1)
    last_of_group = jnp.where(padded > 0, ends - tile, -1)
    spare = ends[-1] + jnp.arange(N_GROUPS, dtype=I32) * tile
    fill = jnp.concatenate([last_of_group, jnp.where(spare < n_tiles * tile, spare, -1)])
    return pos.astype(I32), tile_group.astype(I32), fill.astype(I32), n_tiles


ROW_DMA_UNROLL = 8


def _start_rows(copy, n):
    for r in range(n):
        copy(r).start(priority=r % 2)


def _wait_rows(copy, n):
    def batch(i, c):
        for _ in range(ROW_DMA_UNROLL):
            copy.wait()
        return c

    lax.fori_loop(0, n // ROW_DMA_UNROLL, batch, 0)


def _moe_sort_kernel(pos_ref, fill_ref, x_ref, xs_ref, sbuf, zbuf, sems, zsem):
    ts = x_ref.shape[0] // 2
    tile = zbuf.shape[0]
    i = pl.program_id(0)

    @pl.when(i == 0)
    def _():
        zbuf[...] = jnp.zeros_like(zbuf)
        for e in range(2 * N_GROUPS):
            @pl.when(fill_ref[e] >= 0)
            def _(e=e):
                start = pl.multiple_of(fill_ref[e], SUBLANES)
                zero_fill = pltpu.make_async_copy(zbuf, xs_ref.at[pl.ds(start, tile)], zsem)
                zero_fill.start()
                zero_fill.wait()

    for half in range(2):
        base = (2 * i + half) * ts

        def copy(r, half=half, base=base):
            return pltpu.make_async_copy(sbuf.at[half, pl.ds(r, 1)],
                                         xs_ref.at[pl.ds(pos_ref[base + r], 1)], sems.at[half])

        @pl.when(i > 0)
        def _(copy=copy):
            _wait_rows(copy(0), ts)

        sbuf[half] = x_ref[half * ts:(half + 1) * ts, :]
        _start_rows(copy, ts)

    @pl.when(i == pl.num_programs(0) - 1)
    def _():
        for half in range(2):
            _wait_rows(pltpu.make_async_copy(sbuf.at[half, pl.ds(0, 1)], xs_ref.at[pl.ds(0, 1)],
                                             sems.at[half]), ts)


def _moe_expert_kernel(tg_ref, xs_ref, wr_ref, br_ref, wgu_ref, wd_ref, ys_ref):
    g = tg_ref[pl.program_id(0)]
    xb = xs_ref[...].astype(BF16)
    tm = xb.shape[0]
    lane = lax.broadcasted_iota(I32, (tm, LANES), 1)
    logits = _dot(xb, wr_ref[...]) + br_ref[...]
    neg_inf = -jnp.inf
    eg = jnp.exp(jnp.where(lane < N_GROUPS, logits, neg_inf)
                 - jnp.max(jnp.where(lane < N_GROUPS, logits, neg_inf), axis=1, keepdims=True))
    p_group = (jnp.sum(jnp.where(lane == g, eg, 0.0), axis=1, keepdims=True)
               / jnp.sum(eg, axis=1, keepdims=True))
    lo = N_GROUPS + EXPERTS_PER_GROUP * g
    el = jnp.where(jnp.logical_and(lane >= lo, lane < lo + EXPERTS_PER_GROUP), logits, neg_inf)
    v1 = jnp.max(el, axis=1, keepdims=True)
    i1 = jnp.min(jnp.where(el == v1, lane, LANES), axis=1, keepdims=True)
    el2 = jnp.where(lane == i1, neg_inf, el)
    v2 = jnp.max(el2, axis=1, keepdims=True)
    i2 = jnp.min(jnp.where(el2 == v2, lane, LANES), axis=1, keepdims=True)
    e2 = jnp.exp(v2 - v1)
    p1 = p_group / (1.0 + e2)
    p2 = e2 * p1
    acts = []
    for j in range(EXPERTS_PER_GROUP):
        gate = jnp.where(i1 == lo + j, p1, 0.0) + jnp.where(i2 == lo + j, p2, 0.0)
        hj = _dot(xb, wgu_ref[j].astype(BF16))
        acts.append((gate * (jax.nn.silu(hj[:, :D_EXPERT]) * hj[:, D_EXPERT:])).astype(BF16))
    ys_ref[...] = _dot(jnp.concatenate(acts, axis=1), wd_ref[0].astype(BF16))


def _moe_unsort_kernel(pos_ref, x_ref, ys_ref, g_ref, b_ref, o_ref, ybuf, sems):
    ts = x_ref.shape[0] // 2
    i = pl.program_id(0)

    def fetch(half_tile, buf):
        _start_rows(lambda r: pltpu.make_async_copy(ys_ref.at[pl.ds(pos_ref[half_tile * ts + r], 1)],
                                                    ybuf.at[buf, pl.ds(r, 1)], sems.at[buf]), ts)

    def finish(buf):
        _wait_rows(pltpu.make_async_copy(ys_ref.at[pl.ds(0, 1)], ybuf.at[buf, pl.ds(0, 1)], sems.at[buf]), ts)
        rows = slice(buf * ts, (buf + 1) * ts)
        o_ref[rows, :] = _layer_norm(DN_ALPHA * x_ref[rows, :] + ybuf[buf], g_ref[...], b_ref[...])

    @pl.when(i == 0)
    def _():
        fetch(0, 0)

    fetch(2 * i + 1, 1)
    finish(0)

    @pl.when(i < pl.num_programs(0) - 1)
    def _():
        fetch(2 * i + 2, 0)

    finish(1)


def _moe(x, grp, wr, br, wgu, wd, g, b, ts, tile):
    t = x.shape[0]
    pos, tile_group, fill, n_tiles = _route_plan(grp, tile)
    any_spec = pl.BlockSpec(memory_space=pl.ANY)
    xs = pl.pallas_call(
        _moe_sort_kernel,
        grid_spec=pltpu.PrefetchScalarGridSpec(
            num_scalar_prefetch=2, grid=(t // ts,),
            in_specs=[pl.BlockSpec((ts, D_MODEL), lambda i, pos, fill: (i, 0))],
            out_specs=any_spec,
            scratch_shapes=[pltpu.VMEM((2, ts // 2, D_MODEL), F32), pltpu.VMEM((tile, D_MODEL), F32),
                            pltpu.SemaphoreType.DMA((2,)), pltpu.SemaphoreType.DMA(())]),
        out_shape=jax.ShapeDtypeStruct((n_tiles * tile, D_MODEL), F32),
        compiler_params=_params(("arbitrary",)),
        name="moe_sort",
    )(pos, fill, x)
    ys = pl.pallas_call(
        _moe_expert_kernel,
        grid_spec=pltpu.PrefetchScalarGridSpec(
            num_scalar_prefetch=1, grid=(n_tiles,),
            in_specs=[
                pl.BlockSpec((tile, D_MODEL), lambda i, tg: (i, 0)),
                pl.BlockSpec((D_MODEL, LANES), lambda i, tg: (0, 0)),
                pl.BlockSpec((1, LANES), lambda i, tg: (0, 0)),
                pl.BlockSpec((EXPERTS_PER_GROUP, D_MODEL, 2 * D_EXPERT), lambda i, tg: (tg[i], 0, 0)),
                pl.BlockSpec((1, EXPERTS_PER_GROUP * D_EXPERT, D_MODEL), lambda i, tg: (tg[i], 0, 0)),
            ],
            out_specs=pl.BlockSpec((tile, D_MODEL), lambda i, tg: (i, 0))),
        out_shape=jax.ShapeDtypeStruct((n_tiles * tile, D_MODEL), F32),
        compiler_params=_params(("arbitrary",)),
        name="moe_experts",
    )(tile_group, xs, wr, br, wgu, wd)
    return pl.pallas_call(
        _moe_unsort_kernel,
        grid_spec=pltpu.PrefetchScalarGridSpec(
            num_scalar_prefetch=1, grid=(t // ts,),
            in_specs=[pl.BlockSpec((ts, D_MODEL), lambda i, pos: (i, 0)), any_spec,
                      pl.BlockSpec((1, D_MODEL), lambda i, pos: (0, 0)),
                      pl.BlockSpec((1, D_MODEL), lambda i, pos: (0, 0))],
            out_specs=pl.BlockSpec((ts, D_MODEL), lambda i, pos: (i, 0)),
            scratch_shapes=[pltpu.VMEM((2, ts // 2, D_MODEL), F32), pltpu.SemaphoreType.DMA((2,))]),
        out_shape=jax.ShapeDtypeStruct((t, D_MODEL), F32),
        compiler_params=_params(("arbitrary",)),
        name="moe_unsort",
    )(pos, x, ys, g, b)


def _rope_tables(pos):
    half = HEAD_DIM // 2
    freqs = ROPE_THETA ** (-jnp.arange(half, dtype=F32) / half)
    ang = pos.astype(F32)[:, None] * freqs[None, :]
    cos, sin = jnp.cos(ang), jnp.sin(ang)
    cosn = jnp.tile(cos, (1, LANES // half))
    sinn = jnp.tile(jnp.concatenate([-sin, sin], axis=1), (1, LANES // HEAD_DIM))
    return cosn, sinn, cos.T, sin.T


def _pad_axis(a, axis, size):
    pad = [(0, 0)] * a.ndim
    pad[axis] = (0, size - a.shape[axis])
    return jnp.pad(a, pad)


def kernel(x_prompt, x_sample, cache_k, cache_v, cache_k_idx, state_conv, state_h, w_in, conv_w, conv_b, w_rg_a, b_rg_a, w_rg_x, b_rg_x, lru_lambda, w_branch_attn, w_branch_rnn, w_out, ln1_g, ln1_b, w_router_group, b_router_group, w_router_expert, b_router_expert, w_gate_up, w_down, ln2_g, ln2_b):
    assert w_in.shape[0] == DEPTH == 1
    bp, sp, _ = x_prompt.shape
    bs, ss, _ = x_sample.shape
    past = cache_k.shape[2]
    ns_tok = bs * ss
    assert ns_tok == Q_BLOCK and sp % Q_BLOCK == 0 and sp % KEY_TILE == 0

    wq, wk, wv, wqi, wki, wwi, wxr, wgr, wga, wgb = jnp.split(w_in[0], SPLIT_POINTS, axis=1)
    wt = jnp.concatenate([wq.T, wqi.T, wv.T, _pad_axis(wwi.T, 0, 16)], axis=0).astype(BF16)
    wn = jnp.concatenate([wk, wv, _pad_axis(wki, 1, LANES)], axis=1).astype(BF16)
    row = lambda v: v.reshape(1, -1)
    rg = (wxr.astype(BF16), wgr.astype(BF16), conv_w[0], row(conv_b[0]),
          w_rg_a[0].astype(BF16), row(b_rg_a[0]), w_rg_x[0].astype(BF16), row(b_rg_x[0]),
          row(lru_lambda[0]))
    mg = (wga.astype(BF16), wgb.astype(BF16), w_branch_attn[0].astype(BF16),
          w_branch_rnn[0].astype(BF16), w_out[0].astype(BF16), row(ln1_g[0]), row(ln1_b[0]))
    wr = _pad_axis(jnp.concatenate([w_router_group[0], w_router_expert[0]], axis=1), 1, LANES).astype(BF16)
    br = _pad_axis(jnp.concatenate([b_router_group[0], b_router_expert[0]]).reshape(1, -1), 1, LANES)
    wgt = _pad_axis(w_router_group[0].T, 0, 16).astype(BF16)
    bgt = _pad_axis(b_router_group[0].reshape(-1, 1), 0, 16)
    wd = w_down[0].reshape(N_GROUPS, EXPERTS_PER_GROUP * D_EXPERT, D_MODEL)
    mo = (wr, br, w_gate_up[0], wd, row(ln2_g[0]), row(ln2_b[0]))

    pos_p = jnp.arange(sp, dtype=I32)
    qT, qiT, vT, wiT, k_p, kb_p, v_p, ki_p, kib_p = _proj(x_prompt, wt, wn, _rope_tables(pos_p), 512, KEY_TILE)
    lim_p = jnp.minimum((pos_p // CHUNK + 1) * CHUNK, sp).reshape(sp // PROMPT_Q_BLOCK, 1, PROMPT_Q_BLOCK)
    nk_p = (jnp.max(lim_p, axis=(1, 2)) + KEY_TILE - 1) // KEY_TILE
    attn_p = _attn(nk_p.astype(I32), lim_p, qT, qiT, wiT, kb_p, vT, kib_p)
    x1_p, grp_p, co_p, hl_p = _mixer(x_prompt, attn_p, jnp.zeros((bp, SUBLANES, D_RNN), F32),
                                     jnp.zeros((bp, 1, D_RNN), F32), *rg, *mg, wgt, bgt, ts=512)
    out_p = _moe(x1_p, grp_p, *mo, ts=512, tile=512)

    pos_s = past + (jnp.arange(ns_tok, dtype=I32) % ss)
    xs_flat = x_sample.reshape(1, ns_tok, D_MODEL)
    qT_s, qiT_s, _, wiT_s, k_s, kb_s, v_s, ki_s, kib_s = _proj(xs_flat, wt, wn, _rope_tables(pos_s), ns_tok, ns_tok)

    def per_batch_lanes(a):
        r = a.shape[1]
        return _pad_axis(a[0].reshape(r, bs, ss).transpose(1, 0, 2), 2, Q_BLOCK)

    l_all = past + ss
    lk = -(-l_all // KEY_TILE) * KEY_TILE
    k_all = jnp.concatenate([cache_k[0].reshape(bs, past, ATT_WIDTH).astype(BF16),
                             kb_s.reshape(bs, ss, ATT_WIDTH)], axis=1)
    v_all = jnp.concatenate([cache_v[0].reshape(bs, past, ATT_WIDTH),
                             v_s.reshape(bs, ss, ATT_WIDTH)], axis=1).astype(BF16)
    ki_all = jnp.concatenate([cache_k_idx[0].astype(BF16), kib_s.reshape(bs, ss, IDX_DIM)], axis=1)
    k_all, v_all, ki_all = (_pad_axis(a, 1, lk) for a in (k_all, v_all, ki_all))
    vT_all = v_all.reshape(bs, lk // KEY_TILE, KEY_TILE, ATT_WIDTH).transpose(0, 1, 3, 2)
    limit_s = min((past // CHUNK + 1) * CHUNK, l_all)
    assert (past + ss - 1) // CHUNK == past // CHUNK
    lim_s = jnp.where(jnp.arange(Q_BLOCK) < ss, limit_s, CHUNK).astype(I32).reshape(1, 1, Q_BLOCK)
    nk_s = jnp.full((1,), -(-limit_s // KEY_TILE), I32)
    attn_s = _attn(nk_s, lim_s, per_batch_lanes(qT_s), per_batch_lanes(qiT_s), per_batch_lanes(wiT_s),
                   k_all, vT_all, ki_all)[:, :ss]
    cs8 = jnp.pad(state_conv[0], ((0, 0), (SUBLANES - (CONV_WIDTH - 1), 0), (0, 0)))
    x1_s, grp_s, co_s, hl_s = _mixer(x_sample, attn_s, cs8, state_h[0][:, None, :], *rg, *mg, wgt, bgt, ts=ss)
    out_s = _moe(x1_s, grp_s, *mo, ts=ns_tok, tile=ns_tok)

    keep = CONV_WIDTH - 1
    return (out_p.reshape(bp, sp, D_MODEL), out_s.reshape(bs, ss, D_MODEL),
            k_p.reshape(1, bp, sp, N_HEADS, HEAD_DIM), v_p.reshape(1, bp, sp, N_HEADS, HEAD_DIM),
            ki_p[None], co_p[:, SUBLANES - keep:][None], hl_p[:, 0][None],
            k_s.reshape(1, bs, ss, N_HEADS, HEAD_DIM), v_s.reshape(1, bs, ss, N_HEADS, HEAD_DIM),
            ki_s.reshape(1, bs, ss, IDX_DIM), co_s[:, SUBLANES - keep:][None], hl_s[:, 0][None])
```

```python
import functools

import jax
import jax.numpy as jnp
import numpy as np
from jax import lax
from jax.experimental import pallas as pl
from jax.experimental.pallas import tpu as pltpu

F32 = jnp.float32
BF16 = jnp.bfloat16
I32 = jnp.int32

D_MODEL = 1024
N_HEADS = 8
HEAD_DIM = 64
ATT_WIDTH = N_HEADS * HEAD_DIM
N_IDX_HEADS = 8
IDX_DIM = 64
MAX_TOPK = 256
CHUNK = 64
D_RNN = D_MODEL
N_RNN_BLOCKS = 8
RNN_BLOCK = D_RNN // N_RNN_BLOCKS
CONV_WIDTH = 4
LRU_C = 8.0
N_GROUPS = 4
EXPERTS_PER_GROUP = 4
N_EXPERTS = N_GROUPS * EXPERTS_PER_GROUP
D_EXPERT = 256
ROPE_THETA = 10000.0
LN_EPS = 1e-5
DEPTH = 1
DN_ALPHA = (2.0 * DEPTH) ** 0.25
SPLITS = (ATT_WIDTH, ATT_WIDTH, ATT_WIDTH, N_IDX_HEADS * IDX_DIM, IDX_DIM, N_IDX_HEADS,
          D_RNN, D_RNN, D_MODEL, D_MODEL)
SPLIT_POINTS = tuple(int(v) for v in np.cumsum(SPLITS)[:-1])

LANES = 128
SUBLANES = 8
VMEM_LIMIT = 56 * 1024 * 1024
Q_BLOCK = LANES
PROMPT_Q_BLOCK = 256
KEY_TILE = 256
INT_MIN = np.int32(-2 ** 31)
NEG_BIG = -1e30
KEY_LOWEST_FINITE = np.int32(-2 ** 31 + 2 ** 23)
LOG2_E = 1.4426950408889634
LATE_BITS = 4
WT_ROWS = 3 * ATT_WIDTH + 16
WN_COLS = 2 * ATT_WIDTH + LANES


def _params(semantics):
    return pltpu.CompilerParams(dimension_semantics=semantics, vmem_limit_bytes=VMEM_LIMIT)


def _dot(a, b):
    return jnp.dot(a, b, preferred_element_type=F32)


def _layer_norm(x, g, b):
    mu = jnp.mean(x, axis=-1, keepdims=True)
    xc = x - mu
    var = jnp.mean(xc * xc, axis=-1, keepdims=True)
    return xc * lax.rsqrt(var + LN_EPS) * g + b


def _proj_kernel(x_ref, wt_ref, wn_ref, cosn_ref, sinn_ref, cost_ref, sint_ref,
                 qT_ref, qiT_ref, vT_ref, wiT_ref, k_ref, kb_ref, v_ref, ki_ref, kib_ref):
    tm = x_ref.shape[1]
    kt_out = vT_ref.shape[3]
    xb = x_ref[0].astype(BF16)
    zt = lax.dot_general(wt_ref[...], xb, (((1,), (1,)), ((), ())), preferred_element_type=F32)
    zn = _dot(xb, wn_ref[...])
    cost = cost_ref[...]
    sint = sint_ref[...]
    half = HEAD_DIM // 2
    for base, ref, scale in ((0, qT_ref, HEAD_DIM ** -0.5 * LOG2_E), (ATT_WIDTH, qiT_ref, IDX_DIM ** -0.5)):
        for h in range(N_HEADS):
            r = h * HEAD_DIM
            x1 = zt[base + r:base + r + half]
            x2 = zt[base + r + half:base + r + HEAD_DIM]
            ref[0, r:r + half, :] = ((x1 * cost - x2 * sint) * scale).astype(BF16)
            ref[0, r + half:r + HEAD_DIM, :] = ((x1 * sint + x2 * cost) * scale).astype(BF16)
    for c in range(tm // kt_out):
        vT_ref[0, c] = zt[2 * ATT_WIDTH:3 * ATT_WIDTH, c * kt_out:(c + 1) * kt_out].astype(BF16)
    wiT_ref[0] = zt[3 * ATT_WIDTH:3 * ATT_WIDTH + N_IDX_HEADS] * (N_IDX_HEADS ** -0.5)

    cosn = cosn_ref[...]
    sinn = sinn_ref[...]
    lane = lax.broadcasted_iota(I32, (tm, LANES), 1)
    first_half = (lane & half) == 0

    def rope_n(z):
        partner = jnp.where(first_half, pltpu.roll(z, LANES - half, 1), pltpu.roll(z, half, 1))
        return z * cosn + partner * sinn

    for g in range(ATT_WIDTH // LANES):
        kg = rope_n(zn[:, g * LANES:(g + 1) * LANES])
        k_ref[0, :, g * LANES:(g + 1) * LANES] = kg
        kb_ref[0, :, g * LANES:(g + 1) * LANES] = kg.astype(BF16)
    v_ref[0] = zn[:, ATT_WIDTH:2 * ATT_WIDTH]
    kig = rope_n(zn[:, 2 * ATT_WIDTH:2 * ATT_WIDTH + LANES])
    ki_ref[0] = kig[:, :IDX_DIM]
    kib_ref[0] = kig[:, :IDX_DIM].astype(BF16)


def _proj(x, wt, wn, tables, tm, kt_out):
    b, s, _ = x.shape
    cosn, sinn, cost, sint = tables
    ns = s // tm
    out_shape = (
        jax.ShapeDtypeStruct((b, ATT_WIDTH, s), BF16),
        jax.ShapeDtypeStruct((b, ATT_WIDTH, s), BF16),
        jax.ShapeDtypeStruct((b, s // kt_out, ATT_WIDTH, kt_out), BF16),
        jax.ShapeDtypeStruct((b, N_IDX_HEADS, s), F32),
        jax.ShapeDtypeStruct((b, s, ATT_WIDTH), F32),
        jax.ShapeDtypeStruct((b, s, ATT_WIDTH), BF16),
        jax.ShapeDtypeStruct((b, s, ATT_WIDTH), F32),
        jax.ShapeDtypeStruct((b, s, IDX_DIM), F32),
        jax.ShapeDtypeStruct((b, s, IDX_DIM), BF16),
    )
    tok = lambda w: pl.BlockSpec((1, tm, w), lambda i, j: (i, j, 0))
    feat = lambda r: pl.BlockSpec((1, r, tm), lambda i, j: (i, 0, j))
    return pl.pallas_call(
        _proj_kernel,
        grid=(b, ns),
        in_specs=[
            tok(D_MODEL),
            pl.BlockSpec((WT_ROWS, D_MODEL), lambda i, j: (0, 0)),
            pl.BlockSpec((D_MODEL, WN_COLS), lambda i, j: (0, 0)),
            pl.BlockSpec((tm, LANES), lambda i, j: (j, 0)),
            pl.BlockSpec((tm, LANES), lambda i, j: (j, 0)),
            pl.BlockSpec((HEAD_DIM // 2, tm), lambda i, j: (0, j)),
            pl.BlockSpec((HEAD_DIM // 2, tm), lambda i, j: (0, j)),
        ],
        out_specs=(
            feat(ATT_WIDTH), feat(ATT_WIDTH),
            pl.BlockSpec((1, tm // kt_out, ATT_WIDTH, kt_out), lambda i, j: (i, j, 0, 0)),
            feat(N_IDX_HEADS),
            tok(ATT_WIDTH), tok(ATT_WIDTH), tok(ATT_WIDTH), tok(IDX_DIM), tok(IDX_DIM),
        ),
        out_shape=out_shape,
        compiler_params=_params(("parallel", "parallel")),
        name="proj",
    )(x, wt, wn, cosn, sinn, cost, sint)


def _order_key_to_f32(key):
    return lax.bitcast_convert_type(jnp.where(key < 0, key ^ 0x7FFFFFFF, key), F32)


def _attn_kernel(nk_ref, lim_ref, qT_ref, qiT_ref, wiT_ref, k_ref, vT_ref, ki_ref, o_ref,
                 score_scr, bias_scr, logit_scr, acc_scr):
    nk = nk_ref[pl.program_id(1)]
    lim = lim_ref[0]
    w = wiT_ref[0]
    qi = qiT_ref[0]
    q = qT_ref[0]
    qb = q.shape[1]
    groups = KEY_TILE // SUBLANES

    def rows(kt):
        return pl.ds(pl.multiple_of(kt * KEY_TILE, KEY_TILE), KEY_TILE)

    def fold(x, op):
        return op(op(x.reshape(4, groups // 4, SUBLANES, qb), axis=1), axis=0)

    def tile_loop(body, init):
        if qb > 2 * LANES:
            return lax.fori_loop(0, nk, body, init)
        c = lax.fori_loop(0, nk // 2, lambda i, c: body(2 * i + 1, body(2 * i, c)), init)
        return lax.cond(nk % 2 == 1, lambda c: body(nk - 1, c), lambda c: c, c)

    qi_pairs = [jnp.concatenate([qi[(2 * p) * IDX_DIM:(2 * p + 1) * IDX_DIM],
                                 qi[(2 * p + 1) * IDX_DIM:(2 * p + 2) * IDX_DIM]], axis=1)
                for p in range(N_IDX_HEADS // 2)]
    row_iota = lax.broadcasted_iota(I32, (KEY_TILE, qb), 0)

    def score_tile(kt, carry):
        ki_t = ki_ref[0, rows(kt), :]
        acc = jnp.zeros((KEY_TILE, qb), F32)
        for p in range(N_IDX_HEADS // 2):
            d = _dot(ki_t, qi_pairs[p])
            acc = acc + jnp.maximum(d[:, :qb], 0.0) * w[2 * p:2 * p + 1]
            acc = acc + jnp.maximum(d[:, qb:], 0.0) * w[2 * p + 1:2 * p + 2]
        score_scr[rows(kt), :] = jnp.where(row_iota + kt * KEY_TILE < lim, acc, -jnp.inf)
        return carry

    tile_loop(score_tile, 0)

    def count(pred):
        def body(kt, c):
            return c + fold(jnp.where(pred(score_scr[rows(kt), :]), 1, 0), jnp.sum)

        return jnp.sum(tile_loop(body, jnp.zeros((SUBLANES, qb), I32)), axis=0, keepdims=True)

    def bisect_for(n_tiles):
        def bisect(i, carry):
            t, cnt_t = carry
            cand = t + lax.shift_left(jnp.int32(1), 31 - i)
            cand_f = _order_key_to_f32(cand)
            c = jnp.zeros((SUBLANES, qb), I32)
            for kt in range(n_tiles):
                c = c + fold(jnp.where(score_scr[kt * KEY_TILE:(kt + 1) * KEY_TILE, :] >= cand_f, 1, 0), jnp.sum)
            cnt = jnp.sum(c, axis=0, keepdims=True)
            ok = cnt >= MAX_TOPK
            return jnp.where(ok, cand, t), jnp.where(ok, cnt, cnt_t)

        def run():
            t0 = jnp.full((1, qb), INT_MIN, I32)
            c0 = jnp.full((1, qb), n_tiles * KEY_TILE, I32)
            t, c = lax.fori_loop(0, 32 - LATE_BITS, bisect, (t0, c0))
            settled = jnp.logical_or(c == MAX_TOPK, lim <= MAX_TOPK)
            all_settled = jnp.min(jnp.where(settled, 1.0, 0.0)) > 0.0
            return lax.cond(all_settled, lambda tc: tc,
                            lambda tc: lax.fori_loop(32 - LATE_BITS, 32, bisect, tc), (t, c))
        return run

    max_tiles = score_scr.shape[0] // KEY_TILE
    thr, cnt_thr = lax.switch(nk - 1, [bisect_for(n) for n in range(1, max_tiles + 1)])
    found = thr >= KEY_LOWEST_FINITE
    thr_f = jnp.where(found, _order_key_to_f32(thr), jnp.finfo(F32).min)

    def bias_tile(kt, carry):
        bias_scr[rows(kt), :] = jnp.where(score_scr[rows(kt), :] >= thr_f, 0.0, NEG_BIG)
        return carry

    tile_loop(bias_tile, 0)

    tie = jnp.logical_and(cnt_thr > MAX_TOPK, found)
    any_tie = jnp.max(jnp.where(tie, 1.0, 0.0)) > 0.0

    @pl.when(any_tie)
    def _():
        need = (MAX_TOPK - count(lambda s: s > thr_f)).astype(F32)
        tri = (lax.broadcasted_iota(I32, (KEY_TILE, KEY_TILE), 1)
               <= lax.broadcasted_iota(I32, (KEY_TILE, KEY_TILE), 0))
        tri = jnp.where(tri, 1.0, 0.0).astype(BF16)

        def tie_tile(kt, before):
            s = score_scr[rows(kt), :]
            eq = s == thr_f
            rank = _dot(tri, jnp.where(eq, 1.0, 0.0).astype(BF16)) + before
            sel = jnp.logical_or(s > thr_f, jnp.logical_and(eq, rank <= need))
            bias_scr[rows(kt), :] = jnp.where(sel, 0.0, NEG_BIG)
            return rank[KEY_TILE - 1:KEY_TILE, :]

        lax.fori_loop(0, nk, tie_tile, jnp.zeros((1, qb), F32))

    zeros_half = jnp.zeros((HEAD_DIM, qb), BF16)
    rhs = [jnp.concatenate(
        [jnp.concatenate([q[(2 * p) * HEAD_DIM:(2 * p + 1) * HEAD_DIM], zeros_half], axis=0),
         jnp.concatenate([zeros_half, q[(2 * p + 1) * HEAD_DIM:(2 * p + 2) * HEAD_DIM]], axis=0)], axis=1)
        for p in range(N_HEADS // 2)]

    def logits_tile(kt, m_parts):
        bias = bias_scr[rows(kt), :]
        out = []
        for p in range(N_HEADS // 2):
            s2 = _dot(k_ref[0, rows(kt), p * LANES:(p + 1) * LANES], rhs[p])
            for e in range(2):
                h = 2 * p + e
                s = s2[:, e * qb:(e + 1) * qb] + bias
                logit_scr[h, rows(kt), :] = s
                out.append(jnp.maximum(m_parts[h], fold(s, jnp.max)))
        return tuple(out)

    m_parts = tile_loop(logits_tile, tuple(jnp.full((SUBLANES, qb), NEG_BIG, F32) for _ in range(N_HEADS)))
    m = [jnp.max(mp, axis=0, keepdims=True) for mp in m_parts]
    acc_scr[...] = jnp.zeros_like(acc_scr)

    def pv_tile(kt, l_parts):
        out = []
        for h in range(N_HEADS):
            p = jnp.exp2(logit_scr[h, rows(kt), :] - m[h])
            out.append(l_parts[h] + fold(p, jnp.sum))
            v_t = vT_ref[0, kt, h * HEAD_DIM:(h + 1) * HEAD_DIM, :]
            acc_scr[h * HEAD_DIM:(h + 1) * HEAD_DIM, :] += _dot(v_t, p.astype(BF16))
        return tuple(out)

    l_parts = tile_loop(pv_tile, tuple(jnp.zeros((SUBLANES, qb), F32) for _ in range(N_HEADS)))
    outs = [acc_scr[h * HEAD_DIM:(h + 1) * HEAD_DIM, :] / jnp.sum(l_parts[h], axis=0, keepdims=True)
            for h in range(N_HEADS)]
    o_ref[0] = jnp.concatenate(outs, axis=0).T.astype(BF16)


def _attn(nk, limits, qT, qiT, wiT, kb, vT, kib):
    b, _, sq = qT.shape
    qb = limits.shape[-1]
    nq = sq // qb
    lk = kb.shape[1]
    grid_spec = pltpu.PrefetchScalarGridSpec(
        num_scalar_prefetch=1,
        grid=(b, nq),
        in_specs=[
            pl.BlockSpec((1, 1, qb), lambda i, j, nk: (j, 0, 0)),
            pl.BlockSpec((1, ATT_WIDTH, qb), lambda i, j, nk: (i, 0, j)),
            pl.BlockSpec((1, ATT_WIDTH, qb), lambda i, j, nk: (i, 0, j)),
            pl.BlockSpec((1, N_IDX_HEADS, qb), lambda i, j, nk: (i, 0, j)),
            pl.BlockSpec((1, lk, ATT_WIDTH), lambda i, j, nk: (i, 0, 0)),
            pl.BlockSpec((1, lk // KEY_TILE, ATT_WIDTH, KEY_TILE), lambda i, j, nk: (i, 0, 0, 0)),
            pl.BlockSpec((1, lk, IDX_DIM), lambda i, j, nk: (i, 0, 0)),
        ],
        out_specs=pl.BlockSpec((1, qb, ATT_WIDTH), lambda i, j, nk: (i, j, 0)),
        scratch_shapes=[pltpu.VMEM((lk, qb), F32), pltpu.VMEM((lk, qb), F32),
                        pltpu.VMEM((N_HEADS, lk, qb), F32), pltpu.VMEM((ATT_WIDTH, qb), F32)],
    )
    return pl.pallas_call(
        _attn_kernel,
        grid_spec=grid_spec,
        out_shape=jax.ShapeDtypeStruct((b, sq, ATT_WIDTH), BF16),
        compiler_params=_params(("parallel", "arbitrary")),
        name="attn",
    )(nk, limits, qT, qiT, wiT, kb, vT, kib)


def _mixer_kernel(x_ref, attn_ref, cs_ref, h0_ref, wxr_ref, wgr_ref, cw_ref, cb_ref, wa_ref, ba_ref,
                  wx_ref, bx_ref, lam_ref, wga_ref, wgb_ref, wba_ref, wbr_ref, wout_ref, g_ref, b_ref,
                  wgt_ref, bgt_ref, o_ref, grp_ref, co_ref, hl_ref,
                  xbuf, a_scr, u_scr, h_scr, hc_scr):
    ts = x_ref.shape[1]

    @pl.when(pl.program_id(1) == 0)
    def _():
        xbuf[0:SUBLANES, :] = cs_ref[0]
        hc_scr[...] = h0_ref[0]

    x = x_ref[0]
    xb = x.astype(BF16)
    xr = _dot(xb, wxr_ref[...])
    gr = _dot(xb, wgr_ref[...])
    xbuf[SUBLANES:SUBLANES + ts, :] = xr
    cw = cw_ref[...]
    xc = cb_ref[...] + xbuf[5:5 + ts, :] * cw[0:1]
    xc = xc + xbuf[6:6 + ts, :] * cw[1:2]
    xc = xc + xbuf[7:7 + ts, :] * cw[2:3]
    xc = xc + xr * cw[3:4]
    tail = xbuf[ts:ts + SUBLANES, :]
    xbuf[0:SUBLANES, :] = tail
    co_ref[0] = tail

    xcb = xc.astype(BF16)

    def gate(w_ref, b_ref):
        parts = [_dot(xcb[:, n * RNN_BLOCK:(n + 1) * RNN_BLOCK], w_ref[n]) for n in range(N_RNN_BLOCKS)]
        return jax.nn.sigmoid(jnp.concatenate(parts, axis=1) + b_ref[...])

    r = gate(wa_ref, ba_ref)
    i = gate(wx_ref, bx_ref)
    nl = -lam_ref[...]
    softplus = jnp.maximum(nl, 0.0) + jnp.log1p(jnp.exp(-jnp.abs(nl)))
    log_a = (-LRU_C * r) * softplus
    a_scr[...] = jnp.exp(log_a)
    th = jnp.tanh(log_a)
    u_scr[...] = jnp.exp2(0.5 * jnp.log2(-2.0 * th / (1.0 - th))) * (i * xc)

    row = lax.broadcasted_iota(I32, (SUBLANES, D_RNN), 0)
    h_prev = hc_scr[...]
    for g in range(ts // SUBLANES):
        rs = slice(g * SUBLANES, (g + 1) * SUBLANES)
        a = a_scr[rs, :]
        b = u_scr[rs, :]
        for d in (1, 2, 4):
            a_sh = jnp.where(row >= d, pltpu.roll(a, d, 0), 1.0)
            b_sh = jnp.where(row >= d, pltpu.roll(b, d, 0), 0.0)
            b = a * b_sh + b
            a = a * a_sh
        h = b + a * h_prev
        h_scr[rs, :] = h
        h_prev = h[SUBLANES - 1:SUBLANES, :]
    hc_scr[...] = h_prev
    hl_ref[0] = h_prev
    y = (h_scr[...] * jax.nn.gelu(gr)).astype(BF16)

    ga = _dot(xb, wga_ref[...])
    gb = _dot(xb, wgb_ref[...])
    merged = (jax.nn.sigmoid(ga) * _dot(attn_ref[0], wba_ref[...])
              + jax.nn.sigmoid(gb) * _dot(y, wbr_ref[...]))
    mix = _dot(merged.astype(BF16), wout_ref[...])
    x1 = _layer_norm(DN_ALPHA * x + mix, g_ref[...], b_ref[...])
    o_ref[0] = x1
    logits = lax.dot_general(wgt_ref[...], x1.astype(BF16), (((1,), (1,)), ((), ())),
                             preferred_element_type=F32) + bgt_ref[...]
    best = logits[0:1]
    g_sel = jnp.zeros(best.shape, I32)
    for k in range(1, N_GROUPS):
        g_sel = jnp.where(logits[k:k + 1] > best, k, g_sel)
        best = jnp.maximum(best, logits[k:k + 1])
    grp_ref[0] = g_sel


def _mixer(x, attn, conv_state8, h0, wxr, wgr, conv_w, conv_b, wa, ba, wx, bx, lam,
           wga, wgb, wba, wbr, wout, g, b, wgt, bgt, ts):
    bsz, s, _ = x.shape
    ns = s // ts
    once = pl.Buffered(1)
    const = lambda a: pl.BlockSpec(a.shape, lambda i, j, nd=a.ndim: (0,) * nd, pipeline_mode=once)
    per_b = lambda r: pl.BlockSpec((1, r, D_RNN), lambda i, j: (i, 0, 0))
    tok = lambda w: pl.BlockSpec((1, ts, w), lambda i, j: (i, j, 0))
    weights = (wxr, wgr, conv_w, conv_b, wa, ba, wx, bx, lam, wga, wgb, wba, wbr, wout, g, b, wgt, bgt)
    x1, grp, co, hl = pl.pallas_call(
        _mixer_kernel,
        grid=(bsz, ns),
        in_specs=[tok(D_MODEL), tok(ATT_WIDTH), per_b(SUBLANES), per_b(1)] + [const(a) for a in weights],
        out_specs=(tok(D_MODEL), pl.BlockSpec((1, 1, ts), lambda i, j: (i * ns + j, 0, 0)),
                   per_b(SUBLANES), per_b(1)),
        out_shape=(
            jax.ShapeDtypeStruct((bsz, s, D_MODEL), F32),
            jax.ShapeDtypeStruct((bsz * ns, 1, ts), I32),
            jax.ShapeDtypeStruct((bsz, SUBLANES, D_RNN), F32),
            jax.ShapeDtypeStruct((bsz, 1, D_RNN), F32),
        ),
        scratch_shapes=[
            pltpu.VMEM((ts + SUBLANES, D_RNN), F32),
            pltpu.VMEM((ts, D_RNN), F32), pltpu.VMEM((ts, D_RNN), F32), pltpu.VMEM((ts, D_RNN), F32),
            pltpu.VMEM((1, D_RNN), F32),
        ],
        compiler_params=_params(("parallel", "arbitrary")),
        name="mixer",
    )(x, attn, conv_state8, h0, *weights)
    return x1.reshape(bsz * s, D_MODEL), grp.reshape(bsz * s), co, hl


def _route_plan(g, tile):
    t = g.shape[0]
    onehot = (g[:, None] == jnp.arange(N_GROUPS, dtype=I32)[None, :]).astype(I32)
    csum = jnp.cumsum(onehot, axis=0)
    padded = (csum[-1] + tile - 1) // tile * tile
    ends = jnp.cumsum(padded)
    pos = jnp.sum((ends - padded)[None, :] * onehot, axis=1) + jnp.sum(csum * onehot, axis=1) - 1
    n_tiles = -(-t // tile) + N_GROUPS
    tile_start = jnp.arange(n_tiles, dtype=I32) * tile
    tile_group = jnp.minimum(jnp.sum((tile_start[:, None] >= ends[None, :]).astype(I32), axis=1), N_GROUPS - 1)
    last_of_group = jnp.where(padded > 0, ends - tile, -1)
    spare = ends[-1] + jnp.arange(N_GROUPS, dtype=I32) * tile
    fill = jnp.concatenate([last_of_group, jnp.where(spare < n_tiles * tile, spare, -1)])
    return pos.astype(I32), tile_group.astype(I32), fill.astype(I32), n_tiles


ROW_DMA_UNROLL = 8


def _start_rows(copy, n):
    for r in range(n):
        copy(r).start(priority=r % 2)


def _wait_rows(copy, n):
    def batch(i, c):
        for _ in range(ROW_DMA_UNROLL):
            copy.wait()
        return c

    lax.fori_loop(0, n // ROW_DMA_UNROLL, batch, 0)


def _moe_sort_kernel(pos_ref, fill_ref, x_ref, *rest, first):
    xs_ref, sbuf, zbuf, sems, zsem = rest[-5:]
    ts = x_ref.shape[0] // 2
    tile = zbuf.shape[0]
    i = pl.program_id(0)

    @pl.when(jnp.logical_and(i == 0, first))
    def _():
        zbuf[...] = jnp.zeros_like(zbuf)
        for e in range(2 * N_GROUPS):
            @pl.when(fill_ref[e] >= 0)
            def _(e=e):
                start = pl.multiple_of(fill_ref[e], SUBLANES)
                zero_fill = pltpu.make_async_copy(zbuf, xs_ref.at[pl.ds(start, tile)], zsem)
                zero_fill.start()
                zero_fill.wait()

    for half in range(2):
        base = (2 * i + half) * ts

        def copy(r, half=half, base=base):
            return pltpu.make_async_copy(sbuf.at[half, pl.ds(r, 1)],
                                         xs_ref.at[pl.ds(pos_ref[base + r], 1)], sems.at[half])

        @pl.when(i > 0)
        def _(copy=copy):
            _wait_rows(copy(0), ts)

        sbuf[half] = x_ref[half * ts:(half + 1) * ts, :]
        _start_rows(copy, ts)

    @pl.when(i == pl.num_programs(0) - 1)
    def _():
        for half in range(2):
            _wait_rows(pltpu.make_async_copy(sbuf.at[half, pl.ds(0, 1)], xs_ref.at[pl.ds(0, 1)],
                                             sems.at[half]), ts)


def _moe_expert_kernel(tg_ref, xs_ref, wr_ref, br_ref, wgu_ref, wd_ref, ys_ref):
    g = tg_ref[pl.program_id(0)]
    xb = xs_ref[...].astype(BF16)
    tm = xb.shape[0]
    lane = lax.broadcasted_iota(I32, (tm, LANES), 1)
    logits = _dot(xb, wr_ref[...]) + br_ref[...]
    neg_inf = -jnp.inf
    eg = jnp.exp(jnp.where(lane < N_GROUPS, logits, neg_inf)
                 - jnp.max(jnp.where(lane < N_GROUPS, logits, neg_inf), axis=1, keepdims=True))
    p_group = (jnp.sum(jnp.where(lane == g, eg, 0.0), axis=1, keepdims=True)
               / jnp.sum(eg, axis=1, keepdims=True))
    lo = N_GROUPS + EXPERTS_PER_GROUP * g
    el = jnp.where(jnp.logical_and(lane >= lo, lane < lo + EXPERTS_PER_GROUP), logits, neg_inf)
    v1 = jnp.max(el, axis=1, keepdims=True)
    i1 = jnp.min(jnp.where(el == v1, lane, LANES), axis=1, keepdims=True)
    el2 = jnp.where(lane == i1, neg_inf, el)
    v2 = jnp.max(el2, axis=1, keepdims=True)
    i2 = jnp.min(jnp.where(el2 == v2, lane, LANES), axis=1, keepdims=True)
    e2 = jnp.exp(v2 - v1)
    p1 = p_group / (1.0 + e2)
    p2 = e2 * p1
    acts = []
    for j in range(EXPERTS_PER_GROUP):
        gate = jnp.where(i1 == lo + j, p1, 0.0) + jnp.where(i2 == lo + j, p2, 0.0)
        hj = _dot(xb, wgu_ref[j].astype(BF16))
        acts.append((gate * (jax.nn.silu(hj[:, :D_EXPERT]) * hj[:, D_EXPERT:])).astype(BF16))
    ys_ref[...] = _dot(jnp.concatenate(acts, axis=1), wd_ref[0].astype(BF16))


def _moe_unsort_kernel(pos_ref, x_ref, ys_ref, g_ref, b_ref, o_ref, ybuf, sems):
    ts = x_ref.shape[0] // 2
    i = pl.program_id(0)

    def fetch(half_tile, buf):
        _start_rows(lambda r: pltpu.make_async_copy(ys_ref.at[pl.ds(pos_ref[half_tile * ts + r], 1)],
                                                    ybuf.at[buf, pl.ds(r, 1)], sems.at[buf]), ts)

    def finish(buf):
        _wait_rows(pltpu.make_async_copy(ys_ref.at[pl.ds(0, 1)], ybuf.at[buf, pl.ds(0, 1)], sems.at[buf]), ts)
        rows = slice(buf * ts, (buf + 1) * ts)
        o_ref[rows, :] = _layer_norm(DN_ALPHA * x_ref[rows, :] + ybuf[buf], g_ref[...], b_ref[...])

    @pl.when(i == 0)
    def _():
        fetch(0, 0)

    fetch(2 * i + 1, 1)
    finish(0)

    @pl.when(i < pl.num_programs(0) - 1)
    def _():
        fetch(2 * i + 2, 0)

    finish(1)


def _moe(parts, wr, br, wgu, wd, g, b, tile):
    sizes = [x.shape[0] for x, _, _ in parts]
    pos_all, tile_group, fill, n_tiles = _route_plan(jnp.concatenate([grp for _, grp, _ in parts]), tile)
    any_spec = pl.BlockSpec(memory_space=pl.ANY)
    xs, offset = None, 0
    for (x, _, ts), t in zip(parts, sizes):
        first = xs is None
        pos = pos_all[offset:offset + t]
        offset += t
        xs = pl.pallas_call(
            functools.partial(_moe_sort_kernel, first=first),
            grid_spec=pltpu.PrefetchScalarGridSpec(
                num_scalar_prefetch=2, grid=(t // ts,),
                in_specs=[pl.BlockSpec((ts, D_MODEL), lambda i, pos, fill: (i, 0))] + ([] if first else [any_spec]),
                out_specs=any_spec,
                scratch_shapes=[pltpu.VMEM((2, ts // 2, D_MODEL), F32), pltpu.VMEM((tile, D_MODEL), F32),
                                pltpu.SemaphoreType.DMA((2,)), pltpu.SemaphoreType.DMA(())]),
            out_shape=jax.ShapeDtypeStruct((n_tiles * tile, D_MODEL), F32),
            input_output_aliases={} if first else {3: 0},
            compiler_params=_params(("arbitrary",)),
            name="moe_sort",
        )(pos, fill, x, *([] if first else [xs]))
    ys = pl.pallas_call(
        _moe_expert_kernel,
        grid_spec=pltpu.PrefetchScalarGridSpec(
            num_scalar_prefetch=1, grid=(n_tiles,),
            in_specs=[
                pl.BlockSpec((tile, D_MODEL), lambda i, tg: (i, 0)),
                pl.BlockSpec((D_MODEL, LANES), lambda i, tg: (0, 0)),
                pl.BlockSpec((1, LANES), lambda i, tg: (0, 0)),
                pl.BlockSpec((EXPERTS_PER_GROUP, D_MODEL, 2 * D_EXPERT), lambda i, tg: (tg[i], 0, 0)),
                pl.BlockSpec((1, EXPERTS_PER_GROUP * D_EXPERT, D_MODEL), lambda i, tg: (tg[i], 0, 0)),
            ],
            out_specs=pl.BlockSpec((tile, D_MODEL), lambda i, tg: (i, 0))),
        out_shape=jax.ShapeDtypeStruct((n_tiles * tile, D_MODEL), F32),
        compiler_params=_params(("arbitrary",)),
        name="moe_experts",
    )(tile_group, xs, wr, br, wgu, wd)
    outs, offset = [], 0
    for (x, _, ts), t in zip(parts, sizes):
        pos = pos_all[offset:offset + t]
        offset += t
        outs.append(pl.pallas_call(
            _moe_unsort_kernel,
            grid_spec=pltpu.PrefetchScalarGridSpec(
                num_scalar_prefetch=1, grid=(t // ts,),
                in_specs=[pl.BlockSpec((ts, D_MODEL), lambda i, pos: (i, 0)), any_spec,
                          pl.BlockSpec((1, D_MODEL), lambda i, pos: (0, 0)),
                          pl.BlockSpec((1, D_MODEL), lambda i, pos: (0, 0))],
                out_specs=pl.BlockSpec((ts, D_MODEL), lambda i, pos: (i, 0)),
                scratch_shapes=[pltpu.VMEM((2, ts // 2, D_MODEL), F32), pltpu.SemaphoreType.DMA((2,))]),
            out_shape=jax.ShapeDtypeStruct((t, D_MODEL), F32),
            compiler_params=_params(("arbitrary",)),
            name="moe_unsort",
        )(pos, x, ys, g, b))
    return outs


def _rope_tables(pos):
    half = HEAD_DIM // 2
    freqs = ROPE_THETA ** (-jnp.arange(half, dtype=F32) / half)
    ang = pos.astype(F32)[:, None] * freqs[None, :]
    cos, sin = jnp.cos(ang), jnp.sin(ang)
    cosn = jnp.tile(cos, (1, LANES // half))
    sinn = jnp.tile(jnp.concatenate([-sin, sin], axis=1), (1, LANES // HEAD_DIM))
    return cosn, sinn, cos.T, sin.T


def _pad_axis(a, axis, size):
    pad = [(0, 0)] * a.ndim
    pad[axis] = (0, size - a.shape[axis])
    return jnp.pad(a, pad)


def kernel(x_prompt, x_sample, cache_k, cache_v, cache_k_idx, state_conv, state_h, w_in, conv_w, conv_b, w_rg_a, b_rg_a, w_rg_x, b_rg_x, lru_lambda, w_branch_attn, w_branch_rnn, w_out, ln1_g, ln1_b, w_router_group, b_router_group, w_router_expert, b_router_expert, w_gate_up, w_down, ln2_g, ln2_b):
    assert w_in.shape[0] == DEPTH == 1
    bp, sp, _ = x_prompt.shape
    bs, ss, _ = x_sample.shape
    past = cache_k.shape[2]
    ns_tok = bs * ss
    assert ns_tok == Q_BLOCK and sp % Q_BLOCK == 0 and sp % KEY_TILE == 0

    wq, wk, wv, wqi, wki, wwi, wxr, wgr, wga, wgb = jnp.split(w_in[0], SPLIT_POINTS, axis=1)
    wt = jnp.concatenate([wq.T, wqi.T, wv.T, _pad_axis(wwi.T, 0, 16)], axis=0).astype(BF16)
    wn = jnp.concatenate([wk, wv, _pad_axis(wki, 1, LANES)], axis=1).astype(BF16)
    row = lambda v: v.reshape(1, -1)
    rg = (wxr.astype(BF16), wgr.astype(BF16), conv_w[0], row(conv_b[0]),
          w_rg_a[0].astype(BF16), row(b_rg_a[0]), w_rg_x[0].astype(BF16), row(b_rg_x[0]),
          row(lru_lambda[0]))
    mg = (wga.astype(BF16), wgb.astype(BF16), w_branch_attn[0].astype(BF16),
          w_branch_rnn[0].astype(BF16), w_out[0].astype(BF16), row(ln1_g[0]), row(ln1_b[0]))
    wr = _pad_axis(jnp.concatenate([w_router_group[0], w_router_expert[0]], axis=1), 1, LANES).astype(BF16)
    br = _pad_axis(jnp.concatenate([b_router_group[0], b_router_expert[0]]).reshape(1, -1), 1, LANES)
    wgt = _pad_axis(w_router_group[0].T, 0, 16).astype(BF16)
    bgt = _pad_axis(b_router_group[0].reshape(-1, 1), 0, 16)
    wd = w_down[0].reshape(N_GROUPS, EXPERTS_PER_GROUP * D_EXPERT, D_MODEL)
    mo = (wr, br, w_gate_up[0], wd, row(ln2_g[0]), row(ln2_b[0]))

    pos_p = jnp.arange(sp, dtype=I32)
    qT, qiT, vT, wiT, k_p, kb_p, v_p, ki_p, kib_p = _proj(x_prompt, wt, wn, _rope_tables(pos_p), 512, KEY_TILE)
    lim_p = jnp.minimum((pos_p // CHUNK + 1) * CHUNK, sp).reshape(sp // PROMPT_Q_BLOCK, 1, PROMPT_Q_BLOCK)
    nk_p = (jnp.max(lim_p, axis=(1, 2)) + KEY_TILE - 1) // KEY_TILE
    attn_p = _attn(nk_p.astype(I32), lim_p, qT, qiT, wiT, kb_p, vT, kib_p)
    x1_p, grp_p, co_p, hl_p = _mixer(x_prompt, attn_p, jnp.zeros((bp, SUBLANES, D_RNN), F32),
                                     jnp.zeros((bp, 1, D_RNN), F32), *rg, *mg, wgt, bgt, ts=512)

    pos_s = past + (jnp.arange(ns_tok, dtype=I32) % ss)
    xs_flat = x_sample.reshape(1, ns_tok, D_MODEL)
    qT_s, qiT_s, _, wiT_s, k_s, kb_s, v_s, ki_s, kib_s = _proj(xs_flat, wt, wn, _rope_tables(pos_s), ns_tok, ns_tok)

    def per_batch_lanes(a):
        r = a.shape[1]
        return _pad_axis(a[0].reshape(r, bs, ss).transpose(1, 0, 2), 2, Q_BLOCK)

    l_all = past + ss
    lk = -(-l_all // KEY_TILE) * KEY_TILE
    k_all = jnp.concatenate([cache_k[0].reshape(bs, past, ATT_WIDTH).astype(BF16),
                             kb_s.reshape(bs, ss, ATT_WIDTH)], axis=1)
    v_all = jnp.concatenate([cache_v[0].reshape(bs, past, ATT_WIDTH),
                             v_s.reshape(bs, ss, ATT_WIDTH)], axis=1).astype(BF16)
    ki_all = jnp.concatenate([cache_k_idx[0].astype(BF16), kib_s.reshape(bs, ss, IDX_DIM)], axis=1)
    k_all, v_all, ki_all = (_pad_axis(a, 1, lk) for a in (k_all, v_all, ki_all))
    vT_all = v_all.reshape(bs, lk // KEY_TILE, KEY_TILE, ATT_WIDTH).transpose(0, 1, 3, 2)
    limit_s = min((past // CHUNK + 1) * CHUNK, l_all)
    assert (past + ss - 1) // CHUNK == past // CHUNK
    lim_s = jnp.where(jnp.arange(Q_BLOCK) < ss, limit_s, CHUNK).astype(I32).reshape(1, 1, Q_BLOCK)
    nk_s = jnp.full((1,), -(-limit_s // KEY_TILE), I32)
    attn_s = _attn(nk_s, lim_s, per_batch_lanes(qT_s), per_batch_lanes(qiT_s), per_batch_lanes(wiT_s),
                   k_all, vT_all, ki_all)[:, :ss]
    cs8 = jnp.pad(state_conv[0], ((0, 0), (SUBLANES - (CONV_WIDTH - 1), 0), (0, 0)))
    x1_s, grp_s, co_s, hl_s = _mixer(x_sample, attn_s, cs8, state_h[0][:, None, :], *rg, *mg, wgt, bgt, ts=ss)
    out_p, out_s = _moe([(x1_p, grp_p, 512), (x1_s, grp_s, ns_tok)], *mo, tile=512)

    keep = CONV_WIDTH - 1
    return (out_p.reshape(bp, sp, D_MODEL), out_s.reshape(bs, ss, D_MODEL),
            k_p.reshape(1, bp, sp, N_HEADS, HEAD_DIM), v_p.reshape(1, bp, sp, N_HEADS, HEAD_DIM),
            ki_p[None], co_p[:, SUBLANES - keep:][None], hl_p[:, 0][None],
            k_s.reshape(1, bs, ss, N_HEADS, HEAD_DIM), v_s.reshape(1, bs, ss, N_HEADS, HEAD_DIM),
            ki_s.reshape(1, bs, ss, IDX_DIM), co_s[:, SUBLANES - keep:][None], hl_s[:, 0][None])
```
